```python
import math
import jax, jax.numpy as jnp
from jax import lax
import numpy as np

D_MODEL = 1024
BATCH = 4
SEQ = 8192
DEPTH = 2

MLA_HEADS = 8
MLA_Q_LORA = 512
MLA_KV_LORA = 256
MLA_NOPE = 128
MLA_ROPE = 64
MLA_V = 128
ROPE_THETA = 10000.0
DIFF_HEADS = 8
DIFF_HEAD_DIM = 64
D_FF_DENSE = 2816
N_EXPERTS = 8
TOP_K = 2
D_FF_EXPERT = 3584
Q_BLOCK = 128
EPS = 1e-6
N_DENSE = (DEPTH + 1) // 2
N_MOE = DEPTH // 2

W_CQ = MLA_Q_LORA
W_CKV = MLA_KV_LORA
W_KR = MLA_ROPE
W_DQ = DIFF_HEADS * 2 * DIFF_HEAD_DIM
W_DK = DIFF_HEADS * 2 * DIFF_HEAD_DIM
W_DV = DIFF_HEADS * 2 * DIFF_HEAD_DIM
D_IN = W_CQ + W_CKV + W_KR + W_DQ + W_DK + W_DV
MLA_OUT = MLA_HEADS * MLA_V
DIFF_OUT = DIFF_HEADS * 2 * DIFF_HEAD_DIM

kernel_name = "hybrid_mla_diffattn_gated_moe_encoder"


def rms_norm(x, gain):
    xf = x.astype(jnp.float32)
    y = xf * lax.rsqrt(jnp.mean(xf * xf, axis=-1, keepdims=True) + EPS)
    return (y * gain.astype(jnp.float32)).astype(x.dtype)


def apply_rope(x, pos):
    half = x.shape[-1] // 2
    inv_freq = ROPE_THETA ** (-jnp.arange(half, dtype=jnp.float32) / half)
    ang = pos.astype(jnp.float32)[..., None] * inv_freq
    ang = ang.reshape(ang.shape[:2] + (1,) * (x.ndim - 3) + (half,))
    cos, sin = jnp.cos(ang), jnp.sin(ang)
    xf = x.astype(jnp.float32)
    x1, x2 = xf[..., :half], xf[..., half:]
    return jnp.concatenate([x1 * cos - x2 * sin, x2 * cos + x1 * sin], axis=-1).astype(x.dtype)


def alibi_slopes():
    return 2.0 ** (-8.0 * jnp.arange(1, DIFF_HEADS + 1, dtype=jnp.float32) / DIFF_HEADS)


def to_blocks(a):
    b, s = a.shape[:2]
    a = a.reshape((b, s // Q_BLOCK, Q_BLOCK) + a.shape[2:])
    return jnp.moveaxis(a, 1, 0)


def from_blocks(a):
    a = jnp.moveaxis(a, 0, 1)
    return a.reshape((a.shape[0], a.shape[1] * a.shape[2]) + a.shape[3:])


def mla_attention(c_q, c_kv, k_rope, pos, q_norm_g, kv_norm_g, w_uq, w_ukv,
                  qn_nope_g, qn_rope_g, kn_nope_g, kn_rope_g):
    b, s, _ = c_q.shape
    q = (rms_norm(c_q, q_norm_g) @ w_uq).reshape(b, s, MLA_HEADS, MLA_NOPE + MLA_ROPE)
    kv = (rms_norm(c_kv, kv_norm_g) @ w_ukv).reshape(b, s, MLA_HEADS, MLA_NOPE + MLA_V)
    q_nope = rms_norm(q[..., :MLA_NOPE], qn_nope_g)
    q_rope = apply_rope(rms_norm(q[..., MLA_NOPE:], qn_rope_g), pos)
    k_nope = rms_norm(kv[..., :MLA_NOPE], kn_nope_g)
    v = kv[..., MLA_NOPE:]
    k_r = apply_rope(rms_norm(k_rope, kn_rope_g), pos)
    scale = (MLA_NOPE + MLA_ROPE) ** -0.5

    def block(args):
        qn, qr = args
        sc = (jnp.einsum('bqhn,bkhn->bhqk', qn, k_nope)
              + jnp.einsum('bqhr,bkr->bhqk', qr, k_r))
        p = jax.nn.softmax(sc.astype(jnp.float32) * scale, axis=-1).astype(v.dtype)
        return jnp.einsum('bhqk,bkhv->bqhv', p, v)

    o = from_blocks(lax.map(block, (to_blocks(q_nope), to_blocks(q_rope))))
    return o.reshape(b, s, MLA_OUT)


def diff_attention(q, k, v, pos, qn_g, kn_g, lam_q1, lam_k1, lam_q2, lam_k2,
                   subln_g, lambda_init):
    b, s, _ = q.shape
    q = rms_norm(q.reshape(b, s, DIFF_HEADS, 2, DIFF_HEAD_DIM), qn_g)
    k = rms_norm(k.reshape(b, s, DIFF_HEADS, 2, DIFF_HEAD_DIM), kn_g)
    v = v.reshape(b, s, DIFF_HEADS, 2 * DIFF_HEAD_DIM)
    lam = (jnp.exp(jnp.sum(lam_q1.astype(jnp.float32) * lam_k1.astype(jnp.float32)))
           - jnp.exp(jnp.sum(lam_q2.astype(jnp.float32) * lam_k2.astype(jnp.float32)))
           + lambda_init)
    slopes = alibi_slopes()
    pos_f = pos.astype(jnp.float32)
    scale = DIFF_HEAD_DIM ** -0.5

    def block(args):
        qb, pq = args
        sc = jnp.einsum('bqhcd,bkhcd->bhcqk', qb, k).astype(jnp.float32) * scale
        dist = jnp.abs(pq[:, :, None] - pos_f[:, None, :])
        sc = sc - slopes[None, :, None, None, None] * dist[:, None, None]
        p = jax.nn.softmax(sc, axis=-1)
        w = (p[:, :, 0] - lam * p[:, :, 1]).astype(v.dtype)
        return jnp.einsum('bhqk,bkhv->bqhv', w, v)

    o = from_blocks(lax.map(block, (to_blocks(q), to_blocks(pos_f))))
    o = rms_norm(o, subln_g) * (1.0 - lambda_init)
    return o.reshape(b, s, DIFF_OUT)


def swiglu(h, w_gate, w_up, w_down):
    return (jax.nn.silu(h @ w_gate) * (h @ w_up)) @ w_down


def moe_swiglu(h, w_router, w_gate_e, w_up_e, w_down_e):
    b, s, d = h.shape
    t = h.reshape(b * s, d)
    logits = (t @ w_router).astype(jnp.float32)
    top_vals, top_idx = lax.top_k(logits, TOP_K)
    top_w = jax.nn.softmax(top_vals, axis=-1)
    combine = jnp.sum(jax.nn.one_hot(top_idx, N_EXPERTS, dtype=jnp.float32)
                      * top_w[..., None], axis=1).astype(t.dtype)
    out = jnp.zeros_like(t)
    for e in range(N_EXPERTS):
        y = swiglu(t, w_gate_e[e], w_up_e[e], w_down_e[e])
        out = out + combine[:, e:e + 1] * y
    return out.reshape(b, s, d)


def setup_inputs(seed: int = 0) -> dict:
    key = jax.random.key(seed)
    keys = iter(jax.random.split(key, 64))

    def w(shape, fan_in):
        return jax.random.normal(next(keys), shape, jnp.float32) * (fan_in ** -0.5)

    def gain(shape):
        return 1.0 + 0.02 * jax.random.normal(next(keys), shape, jnp.float32)

    def small(shape, std):
        return std * jax.random.normal(next(keys), shape, jnp.float32)

    L = DEPTH
    return {
        "x": jax.random.normal(next(keys), (BATCH, SEQ, D_MODEL), jnp.float32),
        "positions": jnp.broadcast_to(jnp.arange(SEQ, dtype=jnp.int32), (BATCH, SEQ)),
        "attn_norm_g": gain((L, D_MODEL)),
        "w_in": w((L, D_MODEL, D_IN), D_MODEL),
        "w_gate": w((L, D_MODEL, 2 * D_MODEL), D_MODEL),
        "mla_q_norm_g": gain((L, MLA_Q_LORA)),
        "mla_kv_norm_g": gain((L, MLA_KV_LORA)),
        "mla_w_uq": w((L, MLA_Q_LORA, MLA_HEADS * (MLA_NOPE + MLA_ROPE)), MLA_Q_LORA),
        "mla_w_ukv": w((L, MLA_KV_LORA, MLA_HEADS * (MLA_NOPE + MLA_V)), MLA_KV_LORA),
        "mla_qn_nope_g": gain((L, MLA_NOPE)),
        "mla_qn_rope_g": gain((L, MLA_ROPE)),
        "mla_kn_nope_g": gain((L, MLA_NOPE)),
        "mla_kn_rope_g": gain((L, MLA_ROPE)),
        "diff_qn_g": gain((L, DIFF_HEAD_DIM)),
        "diff_kn_g": gain((L, DIFF_HEAD_DIM)),
        "diff_lam_q1": small((L, DIFF_HEAD_DIM), 0.1),
        "diff_lam_k1": small((L, DIFF_HEAD_DIM), 0.1),
        "diff_lam_q2": small((L, DIFF_HEAD_DIM), 0.1),
        "diff_lam_k2": small((L, DIFF_HEAD_DIM), 0.1),
        "diff_subln_g": gain((L, 2 * DIFF_HEAD_DIM)),
        "w_branch_a": w((L, MLA_OUT, D_MODEL), MLA_OUT),
        "w_branch_b": w((L, DIFF_OUT, D_MODEL), DIFF_OUT),
        "w_out": w((L, D_MODEL, D_MODEL), D_MODEL),
        "ffn_norm_g": gain((L, D_MODEL)),
        "dense_w_gate": w((N_DENSE, D_MODEL, D_FF_DENSE), D_MODEL),
        "dense_w_up": w((N_DENSE, D_MODEL, D_FF_DENSE), D_MODEL),
        "dense_w_down": w((N_DENSE, D_FF_DENSE, D_MODEL), D_FF_DENSE),
        "moe_w_router": w((N_MOE, D_MODEL, N_EXPERTS), D_MODEL),
        "moe_w_gate": w((N_MOE, N_EXPERTS, D_MODEL, D_FF_EXPERT), D_MODEL),
        "moe_w_up": w((N_MOE, N_EXPERTS, D_MODEL, D_FF_EXPERT), D_MODEL),
        "moe_w_down": w((N_MOE, N_EXPERTS, D_FF_EXPERT, D_MODEL), D_FF_EXPERT),
    }


def reference(x, positions, attn_norm_g, w_in, w_gate, mla_q_norm_g, mla_kv_norm_g,
              mla_w_uq, mla_w_ukv, mla_qn_nope_g, mla_qn_rope_g, mla_kn_nope_g,
              mla_kn_rope_g, diff_qn_g, diff_kn_g, diff_lam_q1, diff_lam_k1,
              diff_lam_q2, diff_lam_k2, diff_subln_g, w_branch_a, w_branch_b, w_out,
              ffn_norm_g, dense_w_gate, dense_w_up, dense_w_down, moe_w_router,
              moe_w_gate, moe_w_up, moe_w_down):
    splits = [W_CQ, W_CQ + W_CKV, W_CQ + W_CKV + W_KR, W_CQ + W_CKV + W_KR + W_DQ,
              W_CQ + W_CKV + W_KR + W_DQ + W_DK]
    for layer in range(DEPTH):
        h = rms_norm(x, attn_norm_g[layer])
        c_q, c_kv, k_rope, dq, dk, dv = jnp.split(h @ w_in[layer], splits, axis=-1)
        o_a = mla_attention(c_q, c_kv, k_rope, positions,
                            mla_q_norm_g[layer], mla_kv_norm_g[layer],
                            mla_w_uq[layer], mla_w_ukv[layer],
                            mla_qn_nope_g[layer], mla_qn_rope_g[layer],
                            mla_kn_nope_g[layer], mla_kn_rope_g[layer])
        lambda_init = 0.8 - 0.6 * math.exp(-0.3 * layer)
        o_b = diff_attention(dq, dk, dv, positions, diff_qn_g[layer], diff_kn_g[layer],
                             diff_lam_q1[layer], diff_lam_k1[layer],
                             diff_lam_q2[layer], diff_lam_k2[layer],
                             diff_subln_g[layer], lambda_init)
        g_a, g_b = jnp.split(jax.nn.sigmoid(h @ w_gate[layer]), 2, axis=-1)
        merged = g_a * (o_a @ w_branch_a[layer]) + g_b * (o_b @ w_branch_b[layer])
        x = x + merged @ w_out[layer]
        h = rms_norm(x, ffn_norm_g[layer])
        j = layer // 2
        if layer % 2 == 0:
            x = x + swiglu(h, dense_w_gate[j], dense_w_up[j], dense_w_down[j])
        else:
            x = x + moe_swiglu(h, moe_w_router[j], moe_w_gate[j], moe_w_up[j], moe_w_down[j])
    return x
```

```python
import functools
import math

import jax
import jax.numpy as jnp
from jax import lax
from jax.experimental import pallas as pl
from jax.experimental.pallas import tpu as pltpu

D_MODEL = 1024
DEPTH = 2
MLA_HEADS = 8
MLA_Q_LORA = 512
MLA_KV_LORA = 256
MLA_NOPE = 128
MLA_ROPE = 64
MLA_V = 128
ROPE_THETA = 10000.0
DIFF_HEADS = 8
DIFF_HEAD_DIM = 64
D_FF_DENSE = 2816
N_EXPERTS = 8
TOP_K = 2
D_FF_EXPERT = 3584
EPS = 1e-6

LANES = 128
LOG2E = 1.4426950408889634
NEG_BIG = -1e30
BF16 = jnp.bfloat16
F32 = jnp.float32

P_WIDTH = 6144
P_DQ_BLK = 1
P_DK_BLK = 2
P_DV_COL0 = 3072
P_GATE_COL0 = 4096
VMEM_LIMIT = 52 * 1024 * 1024


def _cparams(sem):
    return pltpu.CompilerParams(dimension_semantics=sem, vmem_limit_bytes=VMEM_LIMIT)


def _rms(x, gain):
    ms = jnp.mean(x * x, axis=-1, keepdims=True)
    return x * lax.rsqrt(ms + EPS) * gain


def _ms_half_lanes(xb):
    lane = lax.broadcasted_iota(jnp.int32, xb.shape, 1)
    lo = lane < 64
    sq = xb * xb
    s_lo = jnp.sum(jnp.where(lo, sq, 0.0), axis=-1, keepdims=True)
    s_hi = jnp.sum(jnp.where(lo, 0.0, sq), axis=-1, keepdims=True)
    return jnp.where(lo, s_lo, s_hi) * (1.0 / 64.0)


def _norm_proj_kernel(x_ref, g_ref, w_ref, o_ref, h_ref, *, gate_tile0):
    j = pl.program_id(1)

    @pl.when(j == 0)
    def _():
        h_ref[...] = _rms(x_ref[...], g_ref[...]).astype(BF16)

    y = jnp.dot(h_ref[...], w_ref[...], preferred_element_type=F32)

    @pl.when(j < gate_tile0)
    def _():
        o_ref[...] = y.astype(BF16)

    @pl.when(j >= gate_tile0)
    def _():
        o_ref[...] = jax.nn.sigmoid(y).astype(BF16)


def _norm_proj(x, gain, w, *, tm=1024, tn=512):
    t, d = x.shape
    n = w.shape[1]
    return pl.pallas_call(
        functools.partial(_norm_proj_kernel, gate_tile0=P_GATE_COL0 // tn),
        grid=(t // tm, n // tn),
        in_specs=[
            pl.BlockSpec((tm, d), lambda i, j: (i, 0)),
            pl.BlockSpec((1, d), lambda i, j: (0, 0)),
            pl.BlockSpec((d, tn), lambda i, j: (0, j)),
        ],
        out_specs=pl.BlockSpec((tm, tn), lambda i, j: (i, j)),
        out_shape=jax.ShapeDtypeStruct((t, n), BF16),
        scratch_shapes=[pltpu.VMEM((tm, d), BF16)],
        compiler_params=_cparams(("parallel", "arbitrary")),
        name="norm_proj",
    )(x, gain, w)


def _rope_table_kernel(pos_ref, freq_ref, cos_ref, sin_ref):
    ang = pos_ref[...] * freq_ref[...]
    lane = lax.broadcasted_iota(jnp.int32, ang.shape, 1)
    first_half = (lane % 64) < 32
    cos_ref[...] = jnp.cos(ang)
    sin_ref[...] = jnp.where(first_half, -1.0, 1.0) * jnp.sin(ang)


def _rope_tables(pos_col, *, tm=2048):
    t = pos_col.shape[0]
    half = MLA_ROPE // 2
    inv_freq = ROPE_THETA ** (-jnp.arange(half, dtype=F32) / half)
    freq = jnp.tile(inv_freq, LANES // half).reshape(1, LANES)
    return pl.pallas_call(
        _rope_table_kernel,
        grid=(t // tm,),
        in_specs=[pl.BlockSpec((tm, 1), lambda i: (i, 0)),
                  pl.BlockSpec((1, LANES), lambda i: (0, 0))],
        out_specs=[pl.BlockSpec((tm, LANES), lambda i: (i, 0))] * 2,
        out_shape=[jax.ShapeDtypeStruct((t, LANES), F32)] * 2,
        compiler_params=_cparams(("parallel",)),
        name="rope_tables",
    )(pos_col, freq)


def _rope(n, cos, sin_signed):
    w = n.shape[-1]
    lane = lax.broadcasted_iota(jnp.int32, n.shape, 1)
    first_half = (lane % 64) < 32
    swapped = jnp.where(first_half, pltpu.roll(n, w - 32, 1), pltpu.roll(n, 32, 1))
    return n * cos + swapped * sin_signed


def _mla_prep_kernel(p_ref, cos_ref, sin_ref, gq_ref, gkv_ref, wuq_ref, wukv_ref,
                     gqn_ref, gqr_ref, gkn_ref, gkr_ref, q_ref, k_ref, v_ref, *, q_scale):
    cq = p_ref[:, 0:MLA_Q_LORA].astype(F32)
    ckv = p_ref[:, MLA_Q_LORA:MLA_Q_LORA + MLA_KV_LORA].astype(F32)
    kr = p_ref[:, 768:896].astype(F32)
    cos = cos_ref[...]
    sin = sin_ref[...]

    q = jnp.dot(_rms(cq, gq_ref[...]).astype(BF16), wuq_ref[...], preferred_element_type=F32)
    kv = jnp.dot(_rms(ckv, gkv_ref[...]).astype(BF16), wukv_ref[...], preferred_element_type=F32)

    n_nope = MLA_HEADS * MLA_NOPE
    kr_ms = jnp.sum(kr * kr, axis=-1, keepdims=True) * (1.0 / MLA_ROPE)
    kr_n = kr * lax.rsqrt(kr_ms + EPS) * gkr_ref[...]
    kr_r = _rope(kr_n, cos, sin)
    kr_dup = (kr_r + pltpu.roll(kr_r, 64, 1)).astype(BF16)

    lane = lax.broadcasted_iota(jnp.int32, (q.shape[0], LANES), 1)
    for pair in range(MLA_HEADS // 2):
        qr = q[:, n_nope + pair * LANES:n_nope + (pair + 1) * LANES]
        qr_n = qr * lax.rsqrt(_ms_half_lanes(qr) + EPS) * gqr_ref[...]
        qr_r = _rope(qr_n, cos, sin) * q_scale
        for sub in range(2):
            h = 2 * pair + sub
            keep = (lane < 64) if sub == 0 else (lane >= 64)
            qn = _rms(q[:, h * MLA_NOPE:(h + 1) * MLA_NOPE], gqn_ref[...]) * q_scale
            q_ref[0, h, :, 0:LANES] = qn.astype(BF16)
            q_ref[0, h, :, LANES:2 * LANES] = jnp.where(keep, qr_r, 0.0).astype(BF16)
            kn = _rms(kv[:, h * MLA_NOPE:(h + 1) * MLA_NOPE], gkn_ref[...])
            k_ref[0, h, :, 0:LANES] = kn.astype(BF16)
            k_ref[0, h, :, LANES:2 * LANES] = kr_dup
            v_ref[0, h, :, :] = kv[:, n_nope + h * MLA_V:n_nope + (h + 1) * MLA_V].astype(BF16)


def _mla_prep(p, cos, sin, gq, gkv, wuq, wukv, gqn, gqr, gkn, gkr, *, batch, seq, tm=512):
    nb = seq // tm
    full = lambda a: pl.BlockSpec(a.shape, lambda b, i: (0,) * a.ndim)
    row = lambda b, i: (b * nb + i, 0)
    q_scale = (MLA_NOPE + MLA_ROPE) ** -0.5 * LOG2E
    hshape = lambda w: jax.ShapeDtypeStruct((batch, MLA_HEADS, seq, w), BF16)
    hspec = lambda w: pl.BlockSpec((1, MLA_HEADS, tm, w), lambda b, i: (b, 0, i, 0))
    return pl.pallas_call(
        functools.partial(_mla_prep_kernel, q_scale=q_scale),
        grid=(batch, nb),
        in_specs=[pl.BlockSpec((tm, 1024), row),
                  pl.BlockSpec((tm, LANES), row), pl.BlockSpec((tm, LANES), row),
                  full(gq), full(gkv), full(wuq), full(wukv),
                  full(gqn), full(gqr), full(gkn), full(gkr)],
        out_specs=[hspec(2 * LANES), hspec(2 * LANES), hspec(MLA_V)],
        out_shape=[hshape(2 * LANES), hshape(2 * LANES), hshape(MLA_V)],
        compiler_params=_cparams(("parallel", "parallel")),
        name="mla_prep",
    )(p, cos, sin, gq, gkv, wuq, wukv, gqn, gqr, gkn, gkr)


def _nt_dot(a, b):
    return lax.dot_general(a, b, (((1,), (1,)), ((), ())), preferred_element_type=F32)


def _online_update(s, m, l):
    m_new = jnp.maximum(m, jnp.max(s, axis=-1, keepdims=True))
    alpha = jnp.exp2(m - m_new)
    p = jnp.exp2(s - m_new)
    l_new = alpha * l + jnp.sum(p, axis=-1, keepdims=True)
    return p, alpha, m_new, l_new


def _mla_attn_kernel(q_ref, k_ref, v_ref, o_ref, *, tk, nk):
    q = q_ref[0, 0]
    tq = q.shape[0]

    def body(j, carry):
        m, l, acc = carry
        off = pl.multiple_of(j * tk, tk)
        k = k_ref[0, 0, pl.ds(off, tk), :]
        v = v_ref[0, 0, pl.ds(off, tk), :]
        p, alpha, m, l = _online_update(_nt_dot(q, k), m, l)
        acc = alpha * acc + jnp.dot(p.astype(BF16), v, preferred_element_type=F32)
        return m, l, acc

    init = (jnp.full((tq, 1), NEG_BIG, F32), jnp.zeros((tq, 1), F32), jnp.zeros((tq, MLA_V), F32))
    _, l, acc = lax.fori_loop(0, nk, body, init)
    o_ref[0] = (acc / l).astype(BF16)


def _mla_attn(q, k, v, *, tq=512, tk=1024):
    b, h, s, w = q.shape
    return pl.pallas_call(
        functools.partial(_mla_attn_kernel, tk=tk, nk=s // tk),
        grid=(b, h, s // tq),
        in_specs=[pl.BlockSpec((1, 1, tq, w), lambda bi, hi, qi: (bi, hi, qi, 0)),
                  pl.BlockSpec((1, 1, s, w), lambda bi, hi, qi: (bi, hi, 0, 0)),
                  pl.BlockSpec((1, 1, s, MLA_V), lambda bi, hi, qi: (bi, hi, 0, 0))],
        out_specs=pl.BlockSpec((1, tq, MLA_V), lambda bi, hi, qi: (bi, qi, hi)),
        out_shape=jax.ShapeDtypeStruct((b, s, h * MLA_V), BF16),
        compiler_params=_cparams(("parallel", "parallel", "arbitrary")),
        name="mla_attn",
    )(q, k, v)


def _diff_prep_kernel(q_ref, k_ref, gq_ref, gk_ref, qn_ref, kn_ref, *, q_scale):
    for c in range(DIFF_HEADS):
        sl = slice(c * LANES, (c + 1) * LANES)
        q = q_ref[:, sl].astype(F32)
        k = k_ref[:, sl].astype(F32)
        qn_ref[:, sl] = (q * lax.rsqrt(_ms_half_lanes(q) + EPS) * (gq_ref[...] * q_scale)).astype(BF16)
        kn_ref[:, sl] = (k * lax.rsqrt(_ms_half_lanes(k) + EPS) * gk_ref[...]).astype(BF16)


def _diff_prep(p, gq, gk, *, tm=1024):
    t = p.shape[0]
    w = DIFF_HEADS * 2 * DIFF_HEAD_DIM
    q_scale = DIFF_HEAD_DIM ** -0.5 * LOG2E
    return pl.pallas_call(
        functools.partial(_diff_prep_kernel, q_scale=q_scale),
        grid=(t // tm,),
        in_specs=[pl.BlockSpec((tm, w), lambda i: (i, P_DQ_BLK)),
                  pl.BlockSpec((tm, w), lambda i: (i, P_DK_BLK)),
                  pl.BlockSpec((1, LANES), lambda i: (0, 0)),
                  pl.BlockSpec((1, LANES), lambda i: (0, 0))],
        out_specs=[pl.BlockSpec((tm, w), lambda i: (i, 0))] * 2,
        out_shape=[jax.ShapeDtypeStruct((t, w), BF16)] * 2,
        compiler_params=_cparams(("parallel",)),
        name="diff_prep",
    )(p, p, gq, gk)


def _diff_attn_kernel(slope_ref, q_ref, k_ref, v_ref, pq_ref, pk_ref, lam_ref, g_ref, o_ref,
                      *, tk, nk, lambda_init):
    q = q_ref[0]
    tq = q.shape[0]
    lane = lax.broadcasted_iota(jnp.int32, q.shape, 1)
    zero = jnp.zeros_like(q)
    q0 = jnp.where(lane < DIFF_HEAD_DIM, q, zero)
    q1 = jnp.where(lane < DIFF_HEAD_DIM, zero, q)
    pq = pq_ref[0]
    slope = slope_ref[pl.program_id(1)] * LOG2E

    def body(j, carry):
        m0, l0, a0, m1, l1, a1 = carry
        off = pl.multiple_of(j * tk, tk)
        k = k_ref[0, pl.ds(off, tk), :]
        v = v_ref[0, pl.ds(off, tk), :]
        bias = slope * jnp.abs(pq - pk_ref[0, :, pl.ds(off, tk)])
        p0, al0, m0, l0 = _online_update(_nt_dot(q0, k) - bias, m0, l0)
        a0 = al0 * a0 + jnp.dot(p0.astype(BF16), v, preferred_element_type=F32)
        p1, al1, m1, l1 = _online_update(_nt_dot(q1, k) - bias, m1, l1)
        a1 = al1 * a1 + jnp.dot(p1.astype(BF16), v, preferred_element_type=F32)
        return m0, l0, a0, m1, l1, a1

    neg = jnp.full((tq, 1), NEG_BIG, F32)
    z1 = jnp.zeros((tq, 1), F32)
    za = jnp.zeros((tq, LANES), F32)
    _, l0, a0, _, l1, a1 = lax.fori_loop(0, nk, body, (neg, z1, za, neg, z1, za))

    lam = (jnp.exp(jnp.sum(lam_ref[0:1, :] * lam_ref[1:2, :], axis=-1, keepdims=True))
           - jnp.exp(jnp.sum(lam_ref[2:3, :] * lam_ref[3:4, :], axis=-1, keepdims=True))
           + lambda_init)
    o = a0 / l0 - lam * (a1 / l1)
    o_ref[0] = (_rms(o, g_ref[...]) * (1.0 - lambda_init)).astype(BF16)


def _diff_attn(slopes, qn, kn, p, pos_q, pos_k, lam4, subln_g, *, lambda_init, tq=512, tk=1024):
    b, s, _ = qn.shape
    dv_blk0 = P_DV_COL0 // LANES
    return pl.pallas_call(
        functools.partial(_diff_attn_kernel, tk=tk, nk=s // tk, lambda_init=lambda_init),
        grid=(b, DIFF_HEADS, s // tq),
        in_specs=[pl.BlockSpec(memory_space=pltpu.SMEM),
                  pl.BlockSpec((1, tq, LANES), lambda bi, hi, qi: (bi, qi, hi)),
                  pl.BlockSpec((1, s, LANES), lambda bi, hi, qi: (bi, 0, hi)),
                  pl.BlockSpec((1, s, LANES), lambda bi, hi, qi: (bi, 0, dv_blk0 + hi)),
                  pl.BlockSpec((1, tq, 1), lambda bi, hi, qi: (bi, qi, 0)),
                  pl.BlockSpec((1, 1, s), lambda bi, hi, qi: (bi, 0, 0)),
                  pl.BlockSpec((4, DIFF_HEAD_DIM), lambda bi, hi, qi: (0, 0)),
                  pl.BlockSpec((1, LANES), lambda bi, hi, qi: (0, 0))],
        out_specs=pl.BlockSpec((1, tq, LANES), lambda bi, hi, qi: (bi, qi, hi)),
        out_shape=jax.ShapeDtypeStruct((b, s, DIFF_HEADS * LANES), BF16),
        compiler_params=_cparams(("parallel", "parallel", "arbitrary")),
        name="diff_attn",
    )(slopes, qn, kn, p, pos_q, pos_k, lam4, subln_g)


def _split_dot_f32(a, w_hi, w_lo):
    a_hi = a.astype(BF16)
    a_lo = (a - a_hi.astype(F32)).astype(BF16)
    return (jnp.dot(a_hi, w_hi, preferred_element_type=F32)
            + jnp.dot(a_lo, w_hi, preferred_element_type=F32)
            + jnp.dot(a_hi, w_lo, preferred_element_type=F32))


def _merge_kernel(x_ref, oa_ref, ob_ref, ga_ref, gb_ref, wa_ref, wb_ref, wo_ref, gf_ref, *rest,
                  with_router):
    if with_router:
        wr_hi_ref, wr_lo_ref, xo_ref, h_ref, logit_ref = rest
    else:
        xo_ref, h_ref = rest
    ya = jnp.dot(oa_ref[...], wa_ref[...], preferred_element_type=F32)
    yb = jnp.dot(ob_ref[...], wb_ref[...], preferred_element_type=F32)
    merged = ga_ref[...].astype(F32) * ya + gb_ref[...].astype(F32) * yb
    x = x_ref[...] + jnp.dot(merged.astype(BF16), wo_ref[...], preferred_element_type=F32)
    xo_ref[...] = x
    h = _rms(x, gf_ref[...])
    h_ref[...] = h.astype(BF16)
    if with_router:
        logit_ref[...] = _split_dot_f32(h, wr_hi_ref[...], wr_lo_ref[...])


def _merge(x, oa, ob, p, wa, wb, wo, gf, router=None, *, tm=512):
    t, d = x.shape
    ga_blk = P_GATE_COL0 // d
    row = lambda i: (i, 0)
    full = lambda a: pl.BlockSpec(a.shape, lambda i: (0,) * a.ndim)
    in_specs = [pl.BlockSpec((tm, d), row), pl.BlockSpec((tm, d), row), pl.BlockSpec((tm, d), row),
                pl.BlockSpec((tm, d), lambda i: (i, ga_blk)),
                pl.BlockSpec((tm, d), lambda i: (i, ga_blk + 1)),
                full(wa), full(wb), full(wo), full(gf)]
    args = [x, oa, ob, p, p, wa, wb, wo, gf]
    out_specs = [pl.BlockSpec((tm, d), row), pl.BlockSpec((tm, d), row)]
    out_shape = [jax.ShapeDtypeStruct((t, d), F32), jax.ShapeDtypeStruct((t, d), BF16)]
    if router is not None:
        in_specs += [full(router[0]), full(router[1])]
        args += list(router)
        out_specs.append(pl.BlockSpec((tm, LANES), row))
        out_shape.append(jax.ShapeDtypeStruct((t, LANES), F32))
    return pl.pallas_call(
        functools.partial(_merge_kernel, with_router=router is not None),
        grid=(t // tm,),
        in_specs=in_specs,
        out_specs=out_specs,
        out_shape=out_shape,
        compiler_params=_cparams(("parallel",)),
        name="merge_router" if router is not None else "merge",
    )(*args)


def _swiglu_act(h, wg, wu):
    g = jnp.dot(h, wg, preferred_element_type=F32)
    u = jnp.dot(h, wu, preferred_element_type=F32)
    return (g * jax.nn.sigmoid(g) * u).astype(BF16)


def _ffn_kernel(x_ref, h_ref, wg_ref, wu_ref, wd_ref, o_ref, acc_ref):
    f = pl.program_id(1)
    y = jnp.dot(_swiglu_act(h_ref[...], wg_ref[...], wu_ref[...]), wd_ref[...],
                preferred_element_type=F32)

    @pl.when(f == 0)
    def _():
        acc_ref[...] = x_ref[...] + y

    @pl.when(f > 0)
    def _():
        acc_ref[...] += y

    @pl.when(f == pl.num_programs(1) - 1)
    def _():
        o_ref[...] = acc_ref[...]


def _ffn(x, h, wg, wu, wd, *, tm=1024, tf=1408):
    t, d = x.shape
    ff = wg.shape[1]
    return pl.pallas_call(
        _ffn_kernel,
        grid=(t // tm, ff // tf),
        in_specs=[pl.BlockSpec((tm, d), lambda i, f: (i, 0)),
                  pl.BlockSpec((tm, d), lambda i, f: (i, 0)),
                  pl.BlockSpec((d, tf), lambda i, f: (0, f)),
                  pl.BlockSpec((d, tf), lambda i, f: (0, f)),
                  pl.BlockSpec((tf, d), lambda i, f: (f, 0))],
        out_specs=pl.BlockSpec((tm, d), lambda i, f: (i, 0)),
        out_shape=jax.ShapeDtypeStruct((t, d), F32),
        scratch_shapes=[pltpu.VMEM((tm, d), F32)],
        compiler_params=_cparams(("parallel", "arbitrary")),
        name="ffn_dense",
    )(x, h, wg, wu, wd)


def _top2_combine(logits):
    lane = lax.broadcasted_iota(jnp.int32, logits.shape, 1).astype(F32)
    neg = jnp.float32(-jnp.inf)
    z = jnp.where(lane < N_EXPERTS, logits, neg)
    m1 = jnp.max(z, axis=-1, keepdims=True)
    i1 = jnp.min(jnp.where(z == m1, lane, float(LANES)), axis=-1, keepdims=True)
    z2 = jnp.where(lane == i1, neg, z)
    m2 = jnp.max(z2, axis=-1, keepdims=True)
    i2 = jnp.min(jnp.where(z2 == m2, lane, float(LANES)), axis=-1, keepdims=True)
    e2 = jnp.exp(m2 - m1)
    w1 = 1.0 / (1.0 + e2)
    w2 = e2 / (1.0 + e2)
    return jnp.where(lane == i1, w1, jnp.where(lane == i2, w2, 0.0))


def _moe_kernel(x_ref, h_ref, logit_ref, wg_ref, wu_ref, wd_ref, o_ref, comb_ref, acc_e_ref, acc_ref):
    e = pl.program_id(1)
    f = pl.program_id(2)
    nf = pl.num_programs(2)

    @pl.when((e == 0) & (f == 0))
    def _():
        comb_ref[...] = _top2_combine(logit_ref[...])
        acc_ref[...] = x_ref[...]

    y = jnp.dot(_swiglu_act(h_ref[...], wg_ref[0], wu_ref[0]), wd_ref[0], preferred_element_type=F32)

    @pl.when(f == 0)
    def _():
        acc_e_ref[...] = y

    @pl.when(f > 0)
    def _():
        acc_e_ref[...] += y

    @pl.when(f == nf - 1)
    def _():
        lane = lax.broadcasted_iota(jnp.int32, comb_ref.shape, 1)
        c = jnp.sum(jnp.where(lane == e, comb_ref[...], 0.0), axis=-1, keepdims=True)
        acc_ref[...] += c * acc_e_ref[...]

    @pl.when((e == pl.num_programs(1) - 1) & (f == nf - 1))
    def _():
        o_ref[...] = acc_ref[...]


def _moe(x, h, logits, wg, wu, wd, *, tm=1024, tf=896):
    t, d = x.shape
    ne, _, ff = wg.shape
    return pl.pallas_call(
        _moe_kernel,
        grid=(t // tm, ne, ff // tf),
        in_specs=[pl.BlockSpec((tm, d), lambda i, e, f: (i, 0)),
                  pl.BlockSpec((tm, d), lambda i, e, f: (i, 0)),
                  pl.BlockSpec((tm, LANES), lambda i, e, f: (i, 0)),
                  pl.BlockSpec((1, d, tf), lambda i, e, f: (e, 0, f)),
                  pl.BlockSpec((1, d, tf), lambda i, e, f: (e, 0, f)),
                  pl.BlockSpec((1, tf, d), lambda i, e, f: (e, f, 0))],
        out_specs=pl.BlockSpec((tm, d), lambda i, e, f: (i, 0)),
        out_shape=jax.ShapeDtypeStruct((t, d), F32),
        scratch_shapes=[pltpu.VMEM((tm, LANES), F32), pltpu.VMEM((tm, d), F32), pltpu.VMEM((tm, d), F32)],
        compiler_params=_cparams(("parallel", "arbitrary", "arbitrary")),
        name="moe",
    )(x, h, logits, wg, wu, wd)


def _row(v, reps=1):
    return jnp.tile(v.astype(F32), reps).reshape(1, -1)


def kernel(x, positions, attn_norm_g, w_in, w_gate, mla_q_norm_g, mla_kv_norm_g, mla_w_uq, mla_w_ukv, mla_qn_nope_g, mla_qn_rope_g, mla_kn_nope_g, mla_kn_rope_g, diff_qn_g, diff_kn_g, diff_lam_q1, diff_lam_k1, diff_lam_q2, diff_lam_k2, diff_subln_g, w_branch_a, w_branch_b, w_out, ffn_norm_g, dense_w_gate, dense_w_up, dense_w_down, moe_w_router, moe_w_gate, moe_w_up, moe_w_down):
    batch, seq, d = x.shape
    t = batch * seq
    xf = x.reshape(t, d)
    pos_f = positions.astype(F32)
    cos, sin = _rope_tables(pos_f.reshape(t, 1))
    pos_q = pos_f.reshape(batch, seq, 1)
    pos_k = pos_f.reshape(batch, 1, seq)
    slopes = 2.0 ** (-8.0 * jnp.arange(1, DIFF_HEADS + 1, dtype=F32) / DIFF_HEADS)
    n_in_head = MLA_Q_LORA + MLA_KV_LORA + MLA_ROPE

    for layer in range(DEPTH):
        w1 = jnp.concatenate(
            [w_in[layer][:, :n_in_head], jnp.zeros((d, 1024 - n_in_head), F32),
             w_in[layer][:, n_in_head:], w_gate[layer]], axis=1).astype(BF16)
        wuq = mla_w_uq[layer].reshape(MLA_Q_LORA, MLA_HEADS, MLA_NOPE + MLA_ROPE)
        wuq = jnp.concatenate([wuq[:, :, :MLA_NOPE].reshape(MLA_Q_LORA, -1),
                               wuq[:, :, MLA_NOPE:].reshape(MLA_Q_LORA, -1)], axis=1).astype(BF16)
        wukv = mla_w_ukv[layer].reshape(MLA_KV_LORA, MLA_HEADS, MLA_NOPE + MLA_V)
        wukv = jnp.concatenate([wukv[:, :, :MLA_NOPE].reshape(MLA_KV_LORA, -1),
                                wukv[:, :, MLA_NOPE:].reshape(MLA_KV_LORA, -1)], axis=1).astype(BF16)
        gkr = jnp.concatenate([mla_kn_rope_g[layer].astype(F32), jnp.zeros((LANES - MLA_ROPE,), F32)]).reshape(1, -1)

        p = _norm_proj(xf, _row(attn_norm_g[layer]), w1)
        q_m, k_m, v_m = _mla_prep(
            p, cos, sin, _row(mla_q_norm_g[layer]), _row(mla_kv_norm_g[layer]), wuq, wukv,
            _row(mla_qn_nope_g[layer]), _row(mla_qn_rope_g[layer], 2), _row(mla_kn_nope_g[layer]), gkr,
            batch=batch, seq=seq)
        o_a = _mla_attn(q_m, k_m, v_m).reshape(t, -1)

        lambda_init = 0.8 - 0.6 * math.exp(-0.3 * layer)
        qn, kn = _diff_prep(p, _row(diff_qn_g[layer], 2), _row(diff_kn_g[layer], 2))
        lam4 = jnp.stack([diff_lam_q1[layer], diff_lam_k1[layer],
                          diff_lam_q2[layer], diff_lam_k2[layer]]).astype(F32)
        o_b = _diff_attn(slopes, qn.reshape(batch, seq, -1), kn.reshape(batch, seq, -1),
                         p.reshape(batch, seq, -1), pos_q, pos_k, lam4, _row(diff_subln_g[layer]),
                         lambda_init=lambda_init).reshape(t, -1)

        j = layer // 2
        wa, wb, wo = (w.astype(BF16) for w in (w_branch_a[layer], w_branch_b[layer], w_out[layer]))
        if layer % 2 == 0:
            xf, h = _merge(xf, o_a, o_b, p, wa, wb, wo, _row(ffn_norm_g[layer]))
            xf = _ffn(xf, h, dense_w_gate[j].astype(BF16), dense_w_up[j].astype(BF16),
                      dense_w_down[j].astype(BF16))
        else:
            wr = jnp.pad(moe_w_router[j].astype(F32), ((0, 0), (0, LANES - N_EXPERTS)))
            wr_hi = wr.astype(BF16)
            wr_lo = (wr - wr_hi.astype(F32)).astype(BF16)
            xf, h, logits = _merge(xf, o_a, o_b, p, wa, wb, wo, _row(ffn_norm_g[layer]),
                                   router=(wr_hi, wr_lo))
            xf = _moe(xf, h, logits, moe_w_gate[j].astype(BF16), moe_w_up[j].astype(BF16),
                      moe_w_down[j].astype(BF16))
    return xf.reshape(batch, seq, d)
```

```python
import functools
import math

import jax
import jax.numpy as jnp
from jax import lax
from jax.experimental import pallas as pl
from jax.experimental.pallas import tpu as pltpu

D_MODEL = 1024
DEPTH = 2
MLA_HEADS = 8
MLA_Q_LORA = 512
MLA_KV_LORA = 256
MLA_NOPE = 128
MLA_ROPE = 64
MLA_V = 128
ROPE_THETA = 10000.0
DIFF_HEADS = 8
DIFF_HEAD_DIM = 64
D_FF_DENSE = 2816
N_EXPERTS = 8
TOP_K = 2
D_FF_EXPERT = 3584
EPS = 1e-6

LANES = 128
LOG2E = 1.4426950408889634
MLA_Q_SCALE = (MLA_NOPE + MLA_ROPE) ** -0.5 * LOG2E
DIFF_Q_SCALE = DIFF_HEAD_DIM ** -0.5 * LOG2E
NEG_BIG = -1e30
BF16 = jnp.bfloat16
F32 = jnp.float32

P_WIDTH = 6144
P_DQ_BLK = 1
P_DK_BLK = 2
P_DV_COL0 = 3072
P_GATE_COL0 = 4096
VMEM_LIMIT = 52 * 1024 * 1024


def _cparams(sem):
    return pltpu.CompilerParams(dimension_semantics=sem, vmem_limit_bytes=VMEM_LIMIT)


def _rms(x, gain):
    ms = jnp.mean(x * x, axis=-1, keepdims=True)
    return x * lax.rsqrt(ms + EPS) * gain


def _ms_half_lanes(xb):
    lane = lax.broadcasted_iota(jnp.int32, xb.shape, 1)
    lo = lane < 64
    sq = xb * xb
    s_lo = jnp.sum(jnp.where(lo, sq, 0.0), axis=-1, keepdims=True)
    s_hi = jnp.sum(jnp.where(lo, 0.0, sq), axis=-1, keepdims=True)
    return jnp.where(lo, s_lo, s_hi) * (1.0 / 64.0)


def _norm_proj_kernel(x_ref, g_ref, w_ref, o_ref, h_ref, *, gate_tile0):
    j = pl.program_id(1)

    @pl.when(j == 0)
    def _():
        h_ref[...] = _rms(x_ref[...], g_ref[...]).astype(BF16)

    y = jnp.dot(h_ref[...], w_ref[...], preferred_element_type=F32)

    @pl.when(j < gate_tile0)
    def _():
        o_ref[...] = y.astype(BF16)

    @pl.when(j >= gate_tile0)
    def _():
        o_ref[...] = jax.nn.sigmoid(y).astype(BF16)


def _norm_proj(x, gain, w, *, tm=1024, tn=512):
    t, d = x.shape
    n = w.shape[1]
    return pl.pallas_call(
        functools.partial(_norm_proj_kernel, gate_tile0=P_GATE_COL0 // tn),
        grid=(t // tm, n // tn),
        in_specs=[
            pl.BlockSpec((tm, d), lambda i, j: (i, 0)),
            pl.BlockSpec((1, d), lambda i, j: (0, 0)),
            pl.BlockSpec((d, tn), lambda i, j: (0, j)),
        ],
        out_specs=pl.BlockSpec((tm, tn), lambda i, j: (i, j)),
        out_shape=jax.ShapeDtypeStruct((t, n), BF16),
        scratch_shapes=[pltpu.VMEM((tm, d), BF16)],
        compiler_params=_cparams(("parallel", "arbitrary")),
        name="norm_proj",
    )(x, gain, w)


def _rope_table_kernel(pos_ref, freq_ref, cos_ref, sin_ref):
    ang = pos_ref[...] * freq_ref[...]
    lane = lax.broadcasted_iota(jnp.int32, ang.shape, 1)
    first_half = (lane % 64) < 32
    cos_ref[...] = jnp.cos(ang)
    sin_ref[...] = jnp.where(first_half, -1.0, 1.0) * jnp.sin(ang)


def _rope_tables(pos_col, *, tm=2048):
    t = pos_col.shape[0]
    half = MLA_ROPE // 2
    inv_freq = ROPE_THETA ** (-jnp.arange(half, dtype=F32) / half)
    freq = jnp.tile(inv_freq, LANES // half).reshape(1, LANES)
    return pl.pallas_call(
        _rope_table_kernel,
        grid=(t // tm,),
        in_specs=[pl.BlockSpec((tm, 1), lambda i: (i, 0)),
                  pl.BlockSpec((1, LANES), lambda i: (0, 0))],
        out_specs=[pl.BlockSpec((tm, LANES), lambda i: (i, 0))] * 2,
        out_shape=[jax.ShapeDtypeStruct((t, LANES), F32)] * 2,
        compiler_params=_cparams(("parallel",)),
        name="rope_tables",
    )(pos_col, freq)


def _rope(n, cos, sin_signed):
    w = n.shape[-1]
    lane = lax.broadcasted_iota(jnp.int32, n.shape, 1)
    first_half = (lane % 64) < 32
    swapped = jnp.where(first_half, pltpu.roll(n, w - 32, 1), pltpu.roll(n, 32, 1))
    return n * cos + swapped * sin_signed


def _mla_prep_kernel(p_ref, cos_ref, sin_ref, gq_ref, gkv_ref, wuq_ref, wukv_ref,
                     gqn_ref, gqr_ref, gkn_ref, gkr_ref, shift_ref, q_ref, k_ref, v_ref, *, q_scale):
    cq = p_ref[:, 0:MLA_Q_LORA].astype(F32)
    ckv = p_ref[:, MLA_Q_LORA:MLA_Q_LORA + MLA_KV_LORA].astype(F32)
    kr = p_ref[:, 768:896].astype(F32)
    cos = cos_ref[...]
    sin = sin_ref[...]

    q = jnp.dot(_rms(cq, gq_ref[...]).astype(BF16), wuq_ref[...], preferred_element_type=F32)
    kv = jnp.dot(_rms(ckv, gkv_ref[...]).astype(BF16), wukv_ref[...], preferred_element_type=F32)

    n_nope = MLA_HEADS * MLA_NOPE
    kr_ms = jnp.sum(kr * kr, axis=-1, keepdims=True) * (1.0 / MLA_ROPE)
    kr_n = kr * lax.rsqrt(kr_ms + EPS) * gkr_ref[...]
    kr_r = _rope(kr_n, cos, sin)
    kr_dup = kr_r + pltpu.roll(kr_r, 64, 1)

    lane = lax.broadcasted_iota(jnp.int32, (q.shape[0], LANES), 1)
    one_hot = jnp.where(lane % 64 == 0, 1.0, 0.0)
    for pair in range(MLA_HEADS // 2):
        qr = q[:, n_nope + pair * LANES:n_nope + (pair + 1) * LANES]
        qr_n = qr * lax.rsqrt(_ms_half_lanes(qr) + EPS) * gqr_ref[...]
        qr_r = _rope(qr_n, cos, sin) * q_scale
        for sub in range(2):
            h = 2 * pair + sub
            keep = (lane < 64) if sub == 0 else (lane >= 64)
            qn = _rms(q[:, h * MLA_NOPE:(h + 1) * MLA_NOPE], gqn_ref[...]) * q_scale
            q_ref[0, h, :, 0:LANES] = qn.astype(BF16)
            q_ref[0, h, :, LANES:2 * LANES] = jnp.where(keep, qr_r, shift_ref[...]).astype(BF16)
            kn = _rms(kv[:, h * MLA_NOPE:(h + 1) * MLA_NOPE], gkn_ref[...])
            k_ref[0, h, :, 0:LANES] = kn.astype(BF16)
            k_ref[0, h, :, LANES:2 * LANES] = jnp.where(keep, kr_dup, one_hot).astype(BF16)
            v_ref[0, h, :, :] = kv[:, n_nope + h * MLA_V:n_nope + (h + 1) * MLA_V].astype(BF16)


def _mla_prep(p, cos, sin, gq, gkv, wuq, wukv, gqn, gqr, gkn, gkr, shift_row, *, batch, seq, tm=512):
    nb = seq // tm
    full = lambda a: pl.BlockSpec(a.shape, lambda b, i: (0,) * a.ndim)
    row = lambda b, i: (b * nb + i, 0)
    hshape = lambda w: jax.ShapeDtypeStruct((batch, MLA_HEADS, seq, w), BF16)
    hspec = lambda w: pl.BlockSpec((1, MLA_HEADS, tm, w), lambda b, i: (b, 0, i, 0))
    return pl.pallas_call(
        functools.partial(_mla_prep_kernel, q_scale=MLA_Q_SCALE),
        grid=(batch, nb),
        in_specs=[pl.BlockSpec((tm, 1024), row),
                  pl.BlockSpec((tm, LANES), row), pl.BlockSpec((tm, LANES), row),
                  full(gq), full(gkv), full(wuq), full(wukv),
                  full(gqn), full(gqr), full(gkn), full(gkr), full(shift_row)],
        out_specs=[hspec(2 * LANES), hspec(2 * LANES), hspec(MLA_V)],
        out_shape=[hshape(2 * LANES), hshape(2 * LANES), hshape(MLA_V)],
        compiler_params=_cparams(("parallel", "parallel")),
        name="mla_prep",
    )(p, cos, sin, gq, gkv, wuq, wukv, gqn, gqr, gkn, gkr, shift_row)


BOUND_MARGIN = 1.02
MAX_SHIFT_SPAN = 100.0


def _norm_bound(n, gain):
    return math.sqrt(n) * jnp.max(jnp.abs(gain.astype(F32)))


def _score_shift(q_bound, k_bound):
    shift = (BOUND_MARGIN * q_bound * k_bound).astype(BF16).astype(F32)
    return shift, (2.0 * shift <= MAX_SHIFT_SPAN)


def _nt_dot(a, b):
    return lax.dot_general(a, b, (((1,), (1,)), ((), ())), preferred_element_type=F32)


def _lane_partial_sum(p):
    acc = p[:, 0:LANES]
    for c in range(1, p.shape[1] // LANES):
        acc = acc + p[:, c * LANES:(c + 1) * LANES]
    return acc


def _online_update(s, m, l):
    m_new = jnp.maximum(m, jnp.max(s, axis=-1, keepdims=True))
    alpha = jnp.exp2(m - m_new)
    p = jnp.exp2(s - m_new)
    l_new = alpha * l + jnp.sum(p, axis=-1, keepdims=True)
    return p, alpha, m_new, l_new


def _mla_attn_kernel(flag_ref, q_ref, k_ref, v_ref, o_ref, *, tk, nk, unroll):
    q = q_ref[0, 0]
    tq = q.shape[0]

    def chunk(j):
        off = pl.multiple_of(j * tk, tk)
        return k_ref[0, 0, pl.ds(off, tk), :], v_ref[0, 0, pl.ds(off, tk), :]

    def bounded():
        def body(j, carry):
            l, acc = carry
            k, v = chunk(j)
            p = jnp.exp2(_nt_dot(q, k))
            return (l + _lane_partial_sum(p),
                    acc + jnp.dot(p.astype(BF16), v, preferred_element_type=F32))

        zero = jnp.zeros((tq, LANES), F32)
        l, acc = lax.fori_loop(0, nk, body, (zero, zero), unroll=unroll)
        return acc / jnp.sum(l, axis=-1, keepdims=True)

    def online():
        def body(j, carry):
            m, l, acc = carry
            k, v = chunk(j)
            p, alpha, m, l = _online_update(_nt_dot(q, k), m, l)
            return m, l, alpha * acc + jnp.dot(p.astype(BF16), v, preferred_element_type=F32)

        init = (jnp.full((tq, 1), NEG_BIG, F32), jnp.zeros((tq, 1), F32), jnp.zeros((tq, MLA_V), F32))
        _, l, acc = lax.fori_loop(0, nk, body, init)
        return acc / l

    o_ref[0] = lax.cond(flag_ref[0] != 0, bounded, online).astype(BF16)


def _mla_attn(flag, q, k, v, *, tq=1024, tk=512, unroll=4):
    b, h, s, w = q.shape
    return pl.pallas_call(
        functools.partial(_mla_attn_kernel, tk=tk, nk=s // tk, unroll=unroll),
        grid=(b, h, s // tq),
        in_specs=[pl.BlockSpec(memory_space=pltpu.SMEM),
                  pl.BlockSpec((1, 1, tq, w), lambda bi, hi, qi: (bi, hi, qi, 0)),
                  pl.BlockSpec((1, 1, s, w), lambda bi, hi, qi: (bi, hi, 0, 0)),
                  pl.BlockSpec((1, 1, s, MLA_V), lambda bi, hi, qi: (bi, hi, 0, 0))],
        out_specs=pl.BlockSpec((1, tq, MLA_V), lambda bi, hi, qi: (bi, qi, hi)),
        out_shape=jax.ShapeDtypeStruct((b, s, h * MLA_V), BF16),
        compiler_params=_cparams(("parallel", "parallel", "arbitrary")),
        name="mla_attn",
    )(flag, q, k, v)


def _diff_prep_kernel(q_ref, k_ref, gq_ref, gk_ref, qn_ref, kn_ref, *, q_scale):
    for c in range(DIFF_HEADS):
        sl = slice(c * LANES, (c + 1) * LANES)
        q = q_ref[:, sl].astype(F32)
        k = k_ref[:, sl].astype(F32)
        qn_ref[:, sl] = (q * lax.rsqrt(_ms_half_lanes(q) + EPS) * (gq_ref[...] * q_scale)).astype(BF16)
        kn_ref[:, sl] = (k * lax.rsqrt(_ms_half_lanes(k) + EPS) * gk_ref[...]).astype(BF16)


def _diff_prep(p, gq, gk, *, tm=1024):
    t = p.shape[0]
    w = DIFF_HEADS * 2 * DIFF_HEAD_DIM
    return pl.pallas_call(
        functools.partial(_diff_prep_kernel, q_scale=DIFF_Q_SCALE),
        grid=(t // tm,),
        in_specs=[pl.BlockSpec((tm, w), lambda i: (i, P_DQ_BLK)),
                  pl.BlockSpec((tm, w), lambda i: (i, P_DK_BLK)),
                  pl.BlockSpec((1, LANES), lambda i: (0, 0)),
                  pl.BlockSpec((1, LANES), lambda i: (0, 0))],
        out_specs=[pl.BlockSpec((tm, w), lambda i: (i, 0))] * 2,
        out_shape=[jax.ShapeDtypeStruct((t, w), BF16)] * 2,
        compiler_params=_cparams(("parallel",)),
        name="diff_prep",
    )(p, p, gq, gk)


def _diff_attn_kernel(scal_ref, q_ref, k_ref, v_ref, pq_ref, pk_ref, lam_ref, g_ref, o_ref,
                      *, tk, nk, unroll, lambda_init):
    q = q_ref[0]
    tq = q.shape[0]
    lane = lax.broadcasted_iota(jnp.int32, q.shape, 1)
    zero = jnp.zeros_like(q)
    q0 = jnp.where(lane < DIFF_HEAD_DIM, q, zero)
    q1 = jnp.where(lane < DIFF_HEAD_DIM, zero, q)
    pq = pq_ref[0]
    slope = scal_ref[pl.program_id(1)] * LOG2E
    shift = scal_ref[DIFF_HEADS]
    use_bounded = scal_ref[DIFF_HEADS + 1]

    def chunk(j):
        off = pl.multiple_of(j * tk, tk)
        k = k_ref[0, pl.ds(off, tk), :]
        v = v_ref[0, pl.ds(off, tk), :]
        bias = slope * jnp.abs(pq - pk_ref[0, :, pl.ds(off, tk)]) + shift
        return k, v, bias

    def bounded():
        def body(j, carry):
            l0, a0, l1, a1 = carry
            k, v, bias = chunk(j)
            p0 = jnp.exp2(_nt_dot(q0, k) - bias)
            p1 = jnp.exp2(_nt_dot(q1, k) - bias)
            return (l0 + _lane_partial_sum(p0),
                    a0 + jnp.dot(p0.astype(BF16), v, preferred_element_type=F32),
                    l1 + _lane_partial_sum(p1),
                    a1 + jnp.dot(p1.astype(BF16), v, preferred_element_type=F32))

        z = jnp.zeros((tq, LANES), F32)
        l0, a0, l1, a1 = lax.fori_loop(0, nk, body, (z, z, z, z), unroll=unroll)
        return (a0 / jnp.sum(l0, axis=-1, keepdims=True), a1 / jnp.sum(l1, axis=-1, keepdims=True))

    def online():
        def body(j, carry):
            m0, l0, a0, m1, l1, a1 = carry
            k, v, bias = chunk(j)
            p0, al0, m0, l0 = _online_update(_nt_dot(q0, k) - bias, m0, l0)
            a0 = al0 * a0 + jnp.dot(p0.astype(BF16), v, preferred_element_type=F32)
            p1, al1, m1, l1 = _online_update(_nt_dot(q1, k) - bias, m1, l1)
            a1 = al1 * a1 + jnp.dot(p1.astype(BF16), v, preferred_element_type=F32)
            return m0, l0, a0, m1, l1, a1

        neg = jnp.full((tq, 1), NEG_BIG, F32)
        z1 = jnp.zeros((tq, 1), F32)
        za = jnp.zeros((tq, LANES), F32)
        _, l0, a0, _, l1, a1 = lax.fori_loop(0, nk, body, (neg, z1, za, neg, z1, za))
        return a0 / l0, a1 / l1

    o0, o1 = lax.cond(use_bounded != 0.0, bounded, online)
    lam = (jnp.exp(jnp.sum(lam_ref[0:1, :] * lam_ref[1:2, :], axis=-1, keepdims=True))
           - jnp.exp(jnp.sum(lam_ref[2:3, :] * lam_ref[3:4, :], axis=-1, keepdims=True))
           + lambda_init)
    o = o0 - lam * o1
    o_ref[0] = (_rms(o, g_ref[...]) * (1.0 - lambda_init)).astype(BF16)


def _diff_attn(scal, qn, kn, p, pos_q, pos_k, lam4, subln_g, *, lambda_init, tq=1024, tk=512, unroll=4):
    b, s, _ = qn.shape
    dv_blk0 = P_DV_COL0 // LANES
    return pl.pallas_call(
        functools.partial(_diff_attn_kernel, tk=tk, nk=s // tk, unroll=unroll, lambda_init=lambda_init),
        grid=(b, DIFF_HEADS, s // tq),
        in_specs=[pl.BlockSpec(memory_space=pltpu.SMEM),
                  pl.BlockSpec((1, tq, LANES), lambda bi, hi, qi: (bi, qi, hi)),
                  pl.BlockSpec((1, s, LANES), lambda bi, hi, qi: (bi, 0, hi)),
                  pl.BlockSpec((1, s, LANES), lambda bi, hi, qi: (bi, 0, dv_blk0 + hi)),
                  pl.BlockSpec((1, tq, 1), lambda bi, hi, qi: (bi, qi, 0)),
                  pl.BlockSpec((1, 1, s), lambda bi, hi, qi: (bi, 0, 0)),
                  pl.BlockSpec((4, DIFF_HEAD_DIM), lambda bi, hi, qi: (0, 0)),
                  pl.BlockSpec((1, LANES), lambda bi, hi, qi: (0, 0))],
        out_specs=pl.BlockSpec((1, tq, LANES), lambda bi, hi, qi: (bi, qi, hi)),
        out_shape=jax.ShapeDtypeStruct((b, s, DIFF_HEADS * LANES), BF16),
        compiler_params=_cparams(("parallel", "parallel", "arbitrary")),
        name="diff_attn",
    )(scal, qn, kn, p, pos_q, pos_k, lam4, subln_g)


def _split_dot_f32(a, w_hi, w_lo):
    a_hi = a.astype(BF16)
    a_lo = (a - a_hi.astype(F32)).astype(BF16)
    return (jnp.dot(a_hi, w_hi, preferred_element_type=F32)
            + jnp.dot(a_lo, w_hi, preferred_element_type=F32)
            + jnp.dot(a_hi, w_lo, preferred_element_type=F32))


def _merge_kernel(x_ref, oa_ref, ob_ref, ga_ref, gb_ref, wa_ref, wb_ref, wo_ref, gf_ref, *rest,
                  with_router):
    if with_router:
        wr_hi_ref, wr_lo_ref, xo_ref, h_ref, logit_ref = rest
    else:
        xo_ref, h_ref = rest
    ya = jnp.dot(oa_ref[...], wa_ref[...], preferred_element_type=F32)
    yb = jnp.dot(ob_ref[...], wb_ref[...], preferred_element_type=F32)
    merged = ga_ref[...].astype(F32) * ya + gb_ref[...].astype(F32) * yb
    x = x_ref[...] + jnp.dot(merged.astype(BF16), wo_ref[...], preferred_element_type=F32)
    xo_ref[...] = x
    h = _rms(x, gf_ref[...])
    h_ref[...] = h.astype(BF16)
    if with_router:
        logit_ref[...] = _split_dot_f32(h, wr_hi_ref[...], wr_lo_ref[...])


def _merge(x, oa, ob, p, wa, wb, wo, gf, router=None, *, tm=512):
    t, d = x.shape
    ga_blk = P_GATE_COL0 // d
    row = lambda i: (i, 0)
    full = lambda a: pl.BlockSpec(a.shape, lambda i: (0,) * a.ndim)
    in_specs = [pl.BlockSpec((tm, d), row), pl.BlockSpec((tm, d), row), pl.BlockSpec((tm, d), row),
                pl.BlockSpec((tm, d), lambda i: (i, ga_blk)),
                pl.BlockSpec((tm, d), lambda i: (i, ga_blk + 1)),
                full(wa), full(wb), full(wo), full(gf)]
    args = [x, oa, ob, p, p, wa, wb, wo, gf]
    out_specs = [pl.BlockSpec((tm, d), row), pl.BlockSpec((tm, d), row)]
    out_shape = [jax.ShapeDtypeStruct((t, d), F32), jax.ShapeDtypeStruct((t, d), BF16)]
    if router is not None:
        in_specs += [full(router[0]), full(router[1])]
        args += list(router)
        out_specs.append(pl.BlockSpec((tm, LANES), row))
        out_shape.append(jax.ShapeDtypeStruct((t, LANES), F32))
    return pl.pallas_call(
        functools.partial(_merge_kernel, with_router=router is not None),
        grid=(t // tm,),
        in_specs=in_specs,
        out_specs=out_specs,
        out_shape=out_shape,
        compiler_params=_cparams(("parallel",)),
        name="merge_router" if router is not None else "merge",
    )(*args)


def _swiglu_act(h, wg, wu):
    g = jnp.dot(h, wg, preferred_element_type=F32)
    u = jnp.dot(h, wu, preferred_element_type=F32)
    return (g * jax.nn.sigmoid(g) * u).astype(BF16)


def _ffn_kernel(x_ref, h_ref, wg_ref, wu_ref, wd_ref, o_ref, acc_ref):
    f = pl.program_id(1)
    y = jnp.dot(_swiglu_act(h_ref[...], wg_ref[...], wu_ref[...]), wd_ref[...],
                preferred_element_type=F32)

    @pl.when(f == 0)
    def _():
        acc_ref[...] = x_ref[...] + y

    @pl.when(f > 0)
    def _():
        acc_ref[...] += y

    @pl.when(f == pl.num_programs(1) - 1)
    def _():
        o_ref[...] = acc_ref[...]


def _ffn(x, h, wg, wu, wd, *, tm=1024, tf=1408):
    t, d = x.shape
    ff = wg.shape[1]
    return pl.pallas_call(
        _ffn_kernel,
        grid=(t // tm, ff // tf),
        in_specs=[pl.BlockSpec((tm, d), lambda i, f: (i, 0)),
                  pl.BlockSpec((tm, d), lambda i, f: (i, 0)),
                  pl.BlockSpec((d, tf), lambda i, f: (0, f)),
                  pl.BlockSpec((d, tf), lambda i, f: (0, f)),
                  pl.BlockSpec((tf, d), lambda i, f: (f, 0))],
        out_specs=pl.BlockSpec((tm, d), lambda i, f: (i, 0)),
        out_shape=jax.ShapeDtypeStruct((t, d), F32),
        scratch_shapes=[pltpu.VMEM((tm, d), F32)],
        compiler_params=_cparams(("parallel", "arbitrary")),
        name="ffn_dense",
    )(x, h, wg, wu, wd)


def _top2_combine(logits):
    lane = lax.broadcasted_iota(jnp.int32, logits.shape, 1).astype(F32)
    neg = jnp.float32(-jnp.inf)
    z = jnp.where(lane < N_EXPERTS, logits, neg)
    m1 = jnp.max(z, axis=-1, keepdims=True)
    i1 = jnp.min(jnp.where(z == m1, lane, float(LANES)), axis=-1, keepdims=True)
    z2 = jnp.where(lane == i1, neg, z)
    m2 = jnp.max(z2, axis=-1, keepdims=True)
    i2 = jnp.min(jnp.where(z2 == m2, lane, float(LANES)), axis=-1, keepdims=True)
    e2 = jnp.exp(m2 - m1)
    w1 = 1.0 / (1.0 + e2)
    w2 = e2 / (1.0 + e2)
    return jnp.where(lane == i1, w1, jnp.where(lane == i2, w2, 0.0))


def _moe_kernel(x_ref, h_ref, logit_ref, wg_ref, wu_ref, wd_ref, o_ref, comb_ref, acc_e_ref, acc_ref):
    e = pl.program_id(1)
    f = pl.program_id(2)
    nf = pl.num_programs(2)

    @pl.when((e == 0) & (f == 0))
    def _():
        comb_ref[...] = _top2_combine(logit_ref[...])
        acc_ref[...] = x_ref[...]

    y = jnp.dot(_swiglu_act(h_ref[...], wg_ref[0], wu_ref[0]), wd_ref[0], preferred_element_type=F32)

    @pl.when(f == 0)
    def _():
        acc_e_ref[...] = y

    @pl.when(f > 0)
    def _():
        acc_e_ref[...] += y

    @pl.when(f == nf - 1)
    def _():
        lane = lax.broadcasted_iota(jnp.int32, comb_ref.shape, 1)
        c = jnp.sum(jnp.where(lane == e, comb_ref[...], 0.0), axis=-1, keepdims=True)
        acc_ref[...] += c * acc_e_ref[...]

    @pl.when((e == pl.num_programs(1) - 1) & (f == nf - 1))
    def _():
        o_ref[...] = acc_ref[...]


def _moe(x, h, logits, wg, wu, wd, *, tm=1024, tf=896):
    t, d = x.shape
    ne, _, ff = wg.shape
    return pl.pallas_call(
        _moe_kernel,
        grid=(t // tm, ne, ff // tf),
        in_specs=[pl.BlockSpec((tm, d), lambda i, e, f: (i, 0)),
                  pl.BlockSpec((tm, d), lambda i, e, f: (i, 0)),
                  pl.BlockSpec((tm, LANES), lambda i, e, f: (i, 0)),
                  pl.BlockSpec((1, d, tf), lambda i, e, f: (e, 0, f)),
                  pl.BlockSpec((1, d, tf), lambda i, e, f: (e, 0, f)),
                  pl.BlockSpec((1, tf, d), lambda i, e, f: (e, f, 0))],
        out_specs=pl.BlockSpec((tm, d), lambda i, e, f: (i, 0)),
        out_shape=jax.ShapeDtypeStruct((t, d), F32),
        scratch_shapes=[pltpu.VMEM((tm, LANES), F32), pltpu.VMEM((tm, d), F32), pltpu.VMEM((tm, d), F32)],
        compiler_params=_cparams(("parallel", "arbitrary", "arbitrary")),
        name="moe",
    )(x, h, logits, wg, wu, wd)


def _row(v, reps=1):
    return jnp.tile(v.astype(F32), reps).reshape(1, -1)


def kernel(x, positions, attn_norm_g, w_in, w_gate, mla_q_norm_g, mla_kv_norm_g, mla_w_uq, mla_w_ukv, mla_qn_nope_g, mla_qn_rope_g, mla_kn_nope_g, mla_kn_rope_g, diff_qn_g, diff_kn_g, diff_lam_q1, diff_lam_k1, diff_lam_q2, diff_lam_k2, diff_subln_g, w_branch_a, w_branch_b, w_out, ffn_norm_g, dense_w_gate, dense_w_up, dense_w_down, moe_w_router, moe_w_gate, moe_w_up, moe_w_down):
    batch, seq, d = x.shape
    t = batch * seq
    xf = x.reshape(t, d)
    pos_f = positions.astype(F32)
    cos, sin = _rope_tables(pos_f.reshape(t, 1))
    pos_q = pos_f.reshape(batch, seq, 1)
    pos_k = pos_f.reshape(batch, 1, seq)
    slopes = 2.0 ** (-8.0 * jnp.arange(1, DIFF_HEADS + 1, dtype=F32) / DIFF_HEADS)
    n_in_head = MLA_Q_LORA + MLA_KV_LORA + MLA_ROPE

    for layer in range(DEPTH):
        w1 = jnp.concatenate(
            [w_in[layer][:, :n_in_head], jnp.zeros((d, 1024 - n_in_head), F32),
             w_in[layer][:, n_in_head:], w_gate[layer]], axis=1).astype(BF16)
        wuq = mla_w_uq[layer].reshape(MLA_Q_LORA, MLA_HEADS, MLA_NOPE + MLA_ROPE)
        wuq = jnp.concatenate([wuq[:, :, :MLA_NOPE].reshape(MLA_Q_LORA, -1),
                               wuq[:, :, MLA_NOPE:].reshape(MLA_Q_LORA, -1)], axis=1).astype(BF16)
        wukv = mla_w_ukv[layer].reshape(MLA_KV_LORA, MLA_HEADS, MLA_NOPE + MLA_V)
        wukv = jnp.concatenate([wukv[:, :, :MLA_NOPE].reshape(MLA_KV_LORA, -1),
                                wukv[:, :, MLA_NOPE:].reshape(MLA_KV_LORA, -1)], axis=1).astype(BF16)
        gkr = jnp.concatenate([mla_kn_rope_g[layer].astype(F32), jnp.zeros((LANES - MLA_ROPE,), F32)]).reshape(1, -1)

        p = _norm_proj(xf, _row(attn_norm_g[layer]), w1)
        q_bound = MLA_Q_SCALE * jnp.sqrt(_norm_bound(MLA_NOPE, mla_qn_nope_g[layer]) ** 2
                                         + _norm_bound(MLA_ROPE, mla_qn_rope_g[layer]) ** 2)
        k_bound = jnp.sqrt(_norm_bound(MLA_NOPE, mla_kn_nope_g[layer]) ** 2
                           + _norm_bound(MLA_ROPE, mla_kn_rope_g[layer]) ** 2)
        shift, use_bounded = _score_shift(q_bound, k_bound)
        shift_row = jnp.where(jnp.arange(LANES) % 64 == 0, -shift, 0.0).astype(F32).reshape(1, LANES)
        q_m, k_m, v_m = _mla_prep(
            p, cos, sin, _row(mla_q_norm_g[layer]), _row(mla_kv_norm_g[layer]), wuq, wukv,
            _row(mla_qn_nope_g[layer]), _row(mla_qn_rope_g[layer], 2), _row(mla_kn_nope_g[layer]), gkr,
            shift_row, batch=batch, seq=seq)
        o_a = _mla_attn(use_bounded.astype(jnp.int32).reshape(1), q_m, k_m, v_m).reshape(t, -1)

        lambda_init = 0.8 - 0.6 * math.exp(-0.3 * layer)
        qn, kn = _diff_prep(p, _row(diff_qn_g[layer], 2), _row(diff_kn_g[layer], 2))
        lam4 = jnp.stack([diff_lam_q1[layer], diff_lam_k1[layer],
                          diff_lam_q2[layer], diff_lam_k2[layer]]).astype(F32)
        d_shift, d_bounded = _score_shift(DIFF_Q_SCALE * _norm_bound(DIFF_HEAD_DIM, diff_qn_g[layer]),
                                          _norm_bound(DIFF_HEAD_DIM, diff_kn_g[layer]))
        scal = jnp.concatenate([slopes, d_shift.reshape(1), d_bounded.astype(F32).reshape(1)])
        o_b = _diff_attn(scal, qn.reshape(batch, seq, -1), kn.reshape(batch, seq, -1),
                         p.reshape(batch, seq, -1), pos_q, pos_k, lam4, _row(diff_subln_g[layer]),
                         lambda_init=lambda_init).reshape(t, -1)

        j = layer // 2
        wa, wb, wo = (w.astype(BF16) for w in (w_branch_a[layer], w_branch_b[layer], w_out[layer]))
        if layer % 2 == 0:
            xf, h = _merge(xf, o_a, o_b, p, wa, wb, wo, _row(ffn_norm_g[layer]))
            xf = _ffn(xf, h, dense_w_gate[j].astype(BF16), dense_w_up[j].astype(BF16),
                      dense_w_down[j].astype(BF16))
        else:
            wr = jnp.pad(moe_w_router[j].astype(F32), ((0, 0), (0, LANES - N_EXPERTS)))
            wr_hi = wr.astype(BF16)
            wr_lo = (wr - wr_hi.astype(F32)).astype(BF16)
            xf, h, logits = _merge(xf, o_a, o_b, p, wa, wb, wo, _row(ffn_norm_g[layer]),
                                   router=(wr_hi, wr_lo))
            xf = _moe(xf, h, logits, moe_w_gate[j].astype(BF16), moe_w_up[j].astype(BF16),
                      moe_w_down[j].astype(BF16))
    return xf.reshape(batch, seq, d)
```

```python
import functools
import math

import jax
import jax.numpy as jnp
from jax import lax
from jax.experimental import pallas as pl
from jax.experimental.pallas import tpu as pltpu

D_MODEL = 1024
DEPTH = 2
MLA_HEADS = 8
MLA_Q_LORA = 512
MLA_KV_LORA = 256
MLA_NOPE = 128
MLA_ROPE = 64
MLA_V = 128
ROPE_THETA = 10000.0
DIFF_HEADS = 8
DIFF_HEAD_DIM = 64
D_FF_DENSE = 2816
N_EXPERTS = 8
TOP_K = 2
D_FF_EXPERT = 3584
EPS = 1e-6

LANES = 128
LOG2E = 1.4426950408889634
MLA_Q_SCALE = (MLA_NOPE + MLA_ROPE) ** -0.5 * LOG2E
DIFF_Q_SCALE = DIFF_HEAD_DIM ** -0.5 * LOG2E
NEG_BIG = -1e30
BF16 = jnp.bfloat16
F32 = jnp.float32

P_WIDTH = 6144
P_DQ_BLK = 1
P_DK_BLK = 2
P_DV_COL0 = 3072
P_GATE_COL0 = 4096
MOE_TILE_ROWS = 1024
VMEM_LIMIT = 52 * 1024 * 1024


def _cparams(sem):
    return pltpu.CompilerParams(dimension_semantics=sem, vmem_limit_bytes=VMEM_LIMIT)


def _rms(x, gain):
    ms = jnp.mean(x * x, axis=-1, keepdims=True)
    return x * lax.rsqrt(ms + EPS) * gain


def _ms_half_lanes(xb):
    lane = lax.broadcasted_iota(jnp.int32, xb.shape, 1)
    lo = lane < 64
    sq = xb * xb
    s_lo = jnp.sum(jnp.where(lo, sq, 0.0), axis=-1, keepdims=True)
    s_hi = jnp.sum(jnp.where(lo, 0.0, sq), axis=-1, keepdims=True)
    return jnp.where(lo, s_lo, s_hi) * (1.0 / 64.0)


def _norm_proj_kernel(x_ref, g_ref, w_ref, o_ref, h_ref, *, gate_tile0):
    j = pl.program_id(1)

    @pl.when(j == 0)
    def _():
        h_ref[...] = _rms(x_ref[...], g_ref[...]).astype(BF16)

    y = jnp.dot(h_ref[...], w_ref[...], preferred_element_type=F32)

    @pl.when(j < gate_tile0)
    def _():
        o_ref[...] = y.astype(BF16)

    @pl.when(j >= gate_tile0)
    def _():
        o_ref[...] = jax.nn.sigmoid(y).astype(BF16)


def _norm_proj(x, gain, w, *, tm=1024, tn=512):
    t, d = x.shape
    n = w.shape[1]
    return pl.pallas_call(
        functools.partial(_norm_proj_kernel, gate_tile0=P_GATE_COL0 // tn),
        grid=(t // tm, n // tn),
        in_specs=[
            pl.BlockSpec((tm, d), lambda i, j: (i, 0)),
            pl.BlockSpec((1, d), lambda i, j: (0, 0)),
            pl.BlockSpec((d, tn), lambda i, j: (0, j)),
        ],
        out_specs=pl.BlockSpec((tm, tn), lambda i, j: (i, j)),
        out_shape=jax.ShapeDtypeStruct((t, n), BF16),
        scratch_shapes=[pltpu.VMEM((tm, d), BF16)],
        compiler_params=_cparams(("parallel", "arbitrary")),
        name="norm_proj",
    )(x, gain, w)


def _rope_table_kernel(pos_ref, freq_ref, cos_ref, sin_ref):
    ang = pos_ref[...] * freq_ref[...]
    lane = lax.broadcasted_iota(jnp.int32, ang.shape, 1)
    first_half = (lane % 64) < 32
    cos_ref[...] = jnp.cos(ang)
    sin_ref[...] = jnp.where(first_half, -1.0, 1.0) * jnp.sin(ang)


def _rope_tables(pos_col, *, tm=2048):
    t = pos_col.shape[0]
    half = MLA_ROPE // 2
    inv_freq = ROPE_THETA ** (-jnp.arange(half, dtype=F32) / half)
    freq = jnp.tile(inv_freq, LANES // half).reshape(1, LANES)
    return pl.pallas_call(
        _rope_table_kernel,
        grid=(t // tm,),
        in_specs=[pl.BlockSpec((tm, 1), lambda i: (i, 0)),
                  pl.BlockSpec((1, LANES), lambda i: (0, 0))],
        out_specs=[pl.BlockSpec((tm, LANES), lambda i: (i, 0))] * 2,
        out_shape=[jax.ShapeDtypeStruct((t, LANES), F32)] * 2,
        compiler_params=_cparams(("parallel",)),
        name="rope_tables",
    )(pos_col, freq)


def _rope(n, cos, sin_signed):
    w = n.shape[-1]
    lane = lax.broadcasted_iota(jnp.int32, n.shape, 1)
    first_half = (lane % 64) < 32
    swapped = jnp.where(first_half, pltpu.roll(n, w - 32, 1), pltpu.roll(n, 32, 1))
    return n * cos + swapped * sin_signed


def _mla_prep_kernel(p_ref, cos_ref, sin_ref, gq_ref, gkv_ref, wuq_ref, wukv_ref,
                     gqn_ref, gqr_ref, gkn_ref, gkr_ref, shift_ref, q_ref, k_ref, v_ref, *, q_scale):
    cq = p_ref[:, 0:MLA_Q_LORA].astype(F32)
    ckv = p_ref[:, MLA_Q_LORA:MLA_Q_LORA + MLA_KV_LORA].astype(F32)
    kr = p_ref[:, 768:896].astype(F32)
    cos = cos_ref[...]
    sin = sin_ref[...]

    q = jnp.dot(_rms(cq, gq_ref[...]).astype(BF16), wuq_ref[...], preferred_element_type=F32)
    kv = jnp.dot(_rms(ckv, gkv_ref[...]).astype(BF16), wukv_ref[...], preferred_element_type=F32)

    n_nope = MLA_HEADS * MLA_NOPE
    kr_ms = jnp.sum(kr * kr, axis=-1, keepdims=True) * (1.0 / MLA_ROPE)
    kr_n = kr * lax.rsqrt(kr_ms + EPS) * gkr_ref[...]
    kr_r = _rope(kr_n, cos, sin)
    kr_dup = kr_r + pltpu.roll(kr_r, 64, 1)

    lane = lax.broadcasted_iota(jnp.int32, (q.shape[0], LANES), 1)
    one_hot = jnp.where(lane % 64 == 0, 1.0, 0.0)
    for pair in range(MLA_HEADS // 2):
        qr = q[:, n_nope + pair * LANES:n_nope + (pair + 1) * LANES]
        qr_n = qr * lax.rsqrt(_ms_half_lanes(qr) + EPS) * gqr_ref[...]
        qr_r = _rope(qr_n, cos, sin) * q_scale
        for sub in range(2):
            h = 2 * pair + sub
            keep = (lane < 64) if sub == 0 else (lane >= 64)
            qn = _rms(q[:, h * MLA_NOPE:(h + 1) * MLA_NOPE], gqn_ref[...]) * q_scale
            q_ref[0, h, :, 0:LANES] = qn.astype(BF16)
            q_ref[0, h, :, LANES:2 * LANES] = jnp.where(keep, qr_r, shift_ref[...]).astype(BF16)
            kn = _rms(kv[:, h * MLA_NOPE:(h + 1) * MLA_NOPE], gkn_ref[...])
            k_ref[0, h, :, 0:LANES] = kn.astype(BF16)
            k_ref[0, h, :, LANES:2 * LANES] = jnp.where(keep, kr_dup, one_hot).astype(BF16)
            v_ref[0, h, :, :] = kv[:, n_nope + h * MLA_V:n_nope + (h + 1) * MLA_V].astype(BF16)


def _mla_prep(p, cos, sin, gq, gkv, wuq, wukv, gqn, gqr, gkn, gkr, shift_row, *, batch, seq, tm=512):
    nb = seq // tm
    full = lambda a: pl.BlockSpec(a.shape, lambda b, i: (0,) * a.ndim)
    row = lambda b, i: (b * nb + i, 0)
    hshape = lambda w: jax.ShapeDtypeStruct((batch, MLA_HEADS, seq, w), BF16)
    hspec = lambda w: pl.BlockSpec((1, MLA_HEADS, tm, w), lambda b, i: (b, 0, i, 0))
    return pl.pallas_call(
        functools.partial(_mla_prep_kernel, q_scale=MLA_Q_SCALE),
        grid=(batch, nb),
        in_specs=[pl.BlockSpec((tm, 1024), row),
                  pl.BlockSpec((tm, LANES), row), pl.BlockSpec((tm, LANES), row),
                  full(gq), full(gkv), full(wuq), full(wukv),
                  full(gqn), full(gqr), full(gkn), full(gkr), full(shift_row)],
        out_specs=[hspec(2 * LANES), hspec(2 * LANES), hspec(MLA_V)],
        out_shape=[hshape(2 * LANES), hshape(2 * LANES), hshape(MLA_V)],
        compiler_params=_cparams(("parallel", "parallel")),
        name="mla_prep",
    )(p, cos, sin, gq, gkv, wuq, wukv, gqn, gqr, gkn, gkr, shift_row)


BOUND_MARGIN = 1.02
MAX_SHIFT_SPAN = 100.0


def _norm_bound(n, gain):
    return math.sqrt(n) * jnp.max(jnp.abs(gain.astype(F32)))


def _score_shift(q_bound, k_bound):
    shift = (BOUND_MARGIN * q_bound * k_bound).astype(BF16).astype(F32)
    return shift, (2.0 * shift <= MAX_SHIFT_SPAN)


def _nt_dot(a, b):
    return lax.dot_general(a, b, (((1,), (1,)), ((), ())), preferred_element_type=F32)


def _lane_partial_sum(p):
    acc = p[:, 0:LANES]
    for c in range(1, p.shape[1] // LANES):
        acc = acc + p[:, c * LANES:(c + 1) * LANES]
    return acc


def _online_update(s, m, l):
    m_new = jnp.maximum(m, jnp.max(s, axis=-1, keepdims=True))
    alpha = jnp.exp2(m - m_new)
    p = jnp.exp2(s - m_new)
    l_new = alpha * l + jnp.sum(p, axis=-1, keepdims=True)
    return p, alpha, m_new, l_new


def _mla_attn_kernel(flag_ref, q_ref, k_ref, v_ref, o_ref, *, tk, nk, unroll):
    q = q_ref[0, 0]
    tq = q.shape[0]

    def chunk(j):
        off = pl.multiple_of(j * tk, tk)
        return k_ref[0, 0, pl.ds(off, tk), :], v_ref[0, 0, pl.ds(off, tk), :]

    def bounded():
        def body(j, carry):
            l, acc = carry
            k, v = chunk(j)
            p = jnp.exp2(_nt_dot(q, k))
            return (l + _lane_partial_sum(p),
                    acc + jnp.dot(p.astype(BF16), v, preferred_element_type=F32))

        zero = jnp.zeros((tq, LANES), F32)
        l, acc = lax.fori_loop(0, nk, body, (zero, zero), unroll=unroll)
        return acc / jnp.sum(l, axis=-1, keepdims=True)

    def online():
        def body(j, carry):
            m, l, acc = carry
            k, v = chunk(j)
            p, alpha, m, l = _online_update(_nt_dot(q, k), m, l)
            return m, l, alpha * acc + jnp.dot(p.astype(BF16), v, preferred_element_type=F32)

        init = (jnp.full((tq, 1), NEG_BIG, F32), jnp.zeros((tq, 1), F32), jnp.zeros((tq, MLA_V), F32))
        _, l, acc = lax.fori_loop(0, nk, body, init)
        return acc / l

    o_ref[0] = lax.cond(flag_ref[0] != 0, bounded, online).astype(BF16)


def _mla_attn(flag, q, k, v, *, tq=1024, tk=512, unroll=4):
    b, h, s, w = q.shape
    return pl.pallas_call(
        functools.partial(_mla_attn_kernel, tk=tk, nk=s // tk, unroll=unroll),
        grid=(b, h, s // tq),
        in_specs=[pl.BlockSpec(memory_space=pltpu.SMEM),
                  pl.BlockSpec((1, 1, tq, w), lambda bi, hi, qi: (bi, hi, qi, 0)),
                  pl.BlockSpec((1, 1, s, w), lambda bi, hi, qi: (bi, hi, 0, 0)),
                  pl.BlockSpec((1, 1, s, MLA_V), lambda bi, hi, qi: (bi, hi, 0, 0))],
        out_specs=pl.BlockSpec((1, tq, MLA_V), lambda bi, hi, qi: (bi, qi, hi)),
        out_shape=jax.ShapeDtypeStruct((b, s, h * MLA_V), BF16),
        compiler_params=_cparams(("parallel", "parallel", "arbitrary")),
        name="mla_attn",
    )(flag, q, k, v)


def _diff_prep_kernel(q_ref, k_ref, gq_ref, gk_ref, qn_ref, kn_ref, *, q_scale):
    for c in range(DIFF_HEADS):
        sl = slice(c * LANES, (c + 1) * LANES)
        q = q_ref[:, sl].astype(F32)
        k = k_ref[:, sl].astype(F32)
        qn_ref[:, sl] = (q * lax.rsqrt(_ms_half_lanes(q) + EPS) * (gq_ref[...] * q_scale)).astype(BF16)
        kn_ref[:, sl] = (k * lax.rsqrt(_ms_half_lanes(k) + EPS) * gk_ref[...]).astype(BF16)


def _diff_prep(p, gq, gk, *, tm=1024):
    t = p.shape[0]
    w = DIFF_HEADS * 2 * DIFF_HEAD_DIM
    return pl.pallas_call(
        functools.partial(_diff_prep_kernel, q_scale=DIFF_Q_SCALE),
        grid=(t // tm,),
        in_specs=[pl.BlockSpec((tm, w), lambda i: (i, P_DQ_BLK)),
                  pl.BlockSpec((tm, w), lambda i: (i, P_DK_BLK)),
                  pl.BlockSpec((1, LANES), lambda i: (0, 0)),
                  pl.BlockSpec((1, LANES), lambda i: (0, 0))],
        out_specs=[pl.BlockSpec((tm, w), lambda i: (i, 0))] * 2,
        out_shape=[jax.ShapeDtypeStruct((t, w), BF16)] * 2,
        compiler_params=_cparams(("parallel",)),
        name="diff_prep",
    )(p, p, gq, gk)


def _diff_attn_kernel(scal_ref, q_ref, k_ref, v_ref, pq_ref, pk_ref, lam_ref, g_ref, o_ref,
                      *, tk, nk, unroll, lambda_init):
    q = q_ref[0]
    tq = q.shape[0]
    lane = lax.broadcasted_iota(jnp.int32, q.shape, 1)
    zero = jnp.zeros_like(q)
    q0 = jnp.where(lane < DIFF_HEAD_DIM, q, zero)
    q1 = jnp.where(lane < DIFF_HEAD_DIM, zero, q)
    pq = pq_ref[0]
    slope = scal_ref[pl.program_id(1)] * LOG2E
    shift = scal_ref[DIFF_HEADS]
    use_bounded = scal_ref[DIFF_HEADS + 1]

    def chunk(j):
        off = pl.multiple_of(j * tk, tk)
        k = k_ref[0, pl.ds(off, tk), :]
        v = v_ref[0, pl.ds(off, tk), :]
        bias = slope * jnp.abs(pq - pk_ref[0, :, pl.ds(off, tk)]) + shift
        return k, v, bias

    def bounded():
        def body(j, carry):
            l0, a0, l1, a1 = carry
            k, v, bias = chunk(j)
            p0 = jnp.exp2(_nt_dot(q0, k) - bias)
            p1 = jnp.exp2(_nt_dot(q1, k) - bias)
            return (l0 + _lane_partial_sum(p0),
                    a0 + jnp.dot(p0.astype(BF16), v, preferred_element_type=F32),
                    l1 + _lane_partial_sum(p1),
                    a1 + jnp.dot(p1.astype(BF16), v, preferred_element_type=F32))

        z = jnp.zeros((tq, LANES), F32)
        l0, a0, l1, a1 = lax.fori_loop(0, nk, body, (z, z, z, z), unroll=unroll)
        return (a0 / jnp.sum(l0, axis=-1, keepdims=True), a1 / jnp.sum(l1, axis=-1, keepdims=True))

    def online():
        def body(j, carry):
            m0, l0, a0, m1, l1, a1 = carry
            k, v, bias = chunk(j)
            p0, al0, m0, l0 = _online_update(_nt_dot(q0, k) - bias, m0, l0)
            a0 = al0 * a0 + jnp.dot(p0.astype(BF16), v, preferred_element_type=F32)
            p1, al1, m1, l1 = _online_update(_nt_dot(q1, k) - bias, m1, l1)
            a1 = al1 * a1 + jnp.dot(p1.astype(BF16), v, preferred_element_type=F32)
            return m0, l0, a0, m1, l1, a1

        neg = jnp.full((tq, 1), NEG_BIG, F32)
        z1 = jnp.zeros((tq, 1), F32)
        za = jnp.zeros((tq, LANES), F32)
        _, l0, a0, _, l1, a1 = lax.fori_loop(0, nk, body, (neg, z1, za, neg, z1, za))
        return a0 / l0, a1 / l1

    o0, o1 = lax.cond(use_bounded != 0.0, bounded, online)
    lam = (jnp.exp(jnp.sum(lam_ref[0:1, :] * lam_ref[1:2, :], axis=-1, keepdims=True))
           - jnp.exp(jnp.sum(lam_ref[2:3, :] * lam_ref[3:4, :], axis=-1, keepdims=True))
           + lambda_init)
    o = o0 - lam * o1
    o_ref[0] = (_rms(o, g_ref[...]) * (1.0 - lambda_init)).astype(BF16)


def _diff_attn(scal, qn, kn, p, pos_q, pos_k, lam4, subln_g, *, lambda_init, tq=1024, tk=512, unroll=4):
    b, s, _ = qn.shape
    dv_blk0 = P_DV_COL0 // LANES
    return pl.pallas_call(
        functools.partial(_diff_attn_kernel, tk=tk, nk=s // tk, unroll=unroll, lambda_init=lambda_init),
        grid=(b, DIFF_HEADS, s // tq),
        in_specs=[pl.BlockSpec(memory_space=pltpu.SMEM),
                  pl.BlockSpec((1, tq, LANES), lambda bi, hi, qi: (bi, qi, hi)),
                  pl.BlockSpec((1, s, LANES), lambda bi, hi, qi: (bi, 0, hi)),
                  pl.BlockSpec((1, s, LANES), lambda bi, hi, qi: (bi, 0, dv_blk0 + hi)),
                  pl.BlockSpec((1, tq, 1), lambda bi, hi, qi: (bi, qi, 0)),
                  pl.BlockSpec((1, 1, s), lambda bi, hi, qi: (bi, 0, 0)),
                  pl.BlockSpec((4, DIFF_HEAD_DIM), lambda bi, hi, qi: (0, 0)),
                  pl.BlockSpec((1, LANES), lambda bi, hi, qi: (0, 0))],
        out_specs=pl.BlockSpec((1, tq, LANES), lambda bi, hi, qi: (bi, qi, hi)),
        out_shape=jax.ShapeDtypeStruct((b, s, DIFF_HEADS * LANES), BF16),
        compiler_params=_cparams(("parallel", "parallel", "arbitrary")),
        name="diff_attn",
    )(scal, qn, kn, p, pos_q, pos_k, lam4, subln_g)


def _split_dot_f32(a, w_hi, w_lo):
    a_hi = a.astype(BF16)
    a_lo = (a - a_hi.astype(F32)).astype(BF16)
    return (jnp.dot(a_hi, w_hi, preferred_element_type=F32)
            + jnp.dot(a_lo, w_hi, preferred_element_type=F32)
            + jnp.dot(a_hi, w_lo, preferred_element_type=F32))


def _merge_kernel(x_ref, oa_ref, ob_ref, ga_ref, gb_ref, wa_ref, wb_ref, wo_ref, gf_ref, *rest,
                  with_router):
    if with_router:
        wr_hi_ref, wr_lo_ref, xo_ref, h_ref, route_ref = rest
    else:
        xo_ref, h_ref = rest
    ya = jnp.dot(oa_ref[...], wa_ref[...], preferred_element_type=F32)
    yb = jnp.dot(ob_ref[...], wb_ref[...], preferred_element_type=F32)
    merged = ga_ref[...].astype(F32) * ya + gb_ref[...].astype(F32) * yb
    x = x_ref[...] + jnp.dot(merged.astype(BF16), wo_ref[...], preferred_element_type=F32)
    xo_ref[...] = x
    h = _rms(x, gf_ref[...])
    h_ref[...] = h.astype(h_ref.dtype)
    if with_router:
        route_ref[...] = _top2_route(_split_dot_f32(h, wr_hi_ref[...], wr_lo_ref[...]))


def _merge(x, oa, ob, p, wa, wb, wo, gf, router=None, *, tm=512):
    t, d = x.shape
    ga_blk = P_GATE_COL0 // d
    row = lambda i: (i, 0)
    full = lambda a: pl.BlockSpec(a.shape, lambda i: (0,) * a.ndim)
    in_specs = [pl.BlockSpec((tm, d), row), pl.BlockSpec((tm, d), row), pl.BlockSpec((tm, d), row),
                pl.BlockSpec((tm, d), lambda i: (i, ga_blk)),
                pl.BlockSpec((tm, d), lambda i: (i, ga_blk + 1)),
                full(wa), full(wb), full(wo), full(gf)]
    args = [x, oa, ob, p, p, wa, wb, wo, gf]
    out_specs = [pl.BlockSpec((tm, d), row), pl.BlockSpec((tm, d), row)]
    h_dtype = BF16 if router is None else F32
    out_shape = [jax.ShapeDtypeStruct((t, d), F32), jax.ShapeDtypeStruct((t, d), h_dtype)]
    if router is not None:
        in_specs += [full(router[0]), full(router[1])]
        args += list(router)
        out_specs.append(pl.BlockSpec((tm, LANES), row))
        out_shape.append(jax.ShapeDtypeStruct((t, LANES), F32))
    return pl.pallas_call(
        functools.partial(_merge_kernel, with_router=router is not None),
        grid=(t // tm,),
        in_specs=in_specs,
        out_specs=out_specs,
        out_shape=out_shape,
        compiler_params=_cparams(("parallel",)),
        name="merge_router" if router is not None else "merge",
    )(*args)


def _swiglu_act(h, wg, wu):
    g = jnp.dot(h, wg, preferred_element_type=F32)
    u = jnp.dot(h, wu, preferred_element_type=F32)
    return (g * jax.nn.sigmoid(g) * u).astype(BF16)


def _ffn_kernel(x_ref, h_ref, wg_ref, wu_ref, wd_ref, o_ref, acc_ref):
    f = pl.program_id(1)
    y = jnp.dot(_swiglu_act(h_ref[...], wg_ref[...], wu_ref[...]), wd_ref[...],
                preferred_element_type=F32)

    @pl.when(f == 0)
    def _():
        acc_ref[...] = x_ref[...] + y

    @pl.when(f > 0)
    def _():
        acc_ref[...] += y

    @pl.when(f == pl.num_programs(1) - 1)
    def _():
        o_ref[...] = acc_ref[...]


def _ffn(x, h, wg, wu, wd, *, tm=1024, tf=1408):
    t, d = x.shape
    ff = wg.shape[1]
    return pl.pallas_call(
        _ffn_kernel,
        grid=(t // tm, ff // tf),
        in_specs=[pl.BlockSpec((tm, d), lambda i, f: (i, 0)),
                  pl.BlockSpec((tm, d), lambda i, f: (i, 0)),
                  pl.BlockSpec((d, tf), lambda i, f: (0, f)),
                  pl.BlockSpec((d, tf), lambda i, f: (0, f)),
                  pl.BlockSpec((tf, d), lambda i, f: (f, 0))],
        out_specs=pl.BlockSpec((tm, d), lambda i, f: (i, 0)),
        out_shape=jax.ShapeDtypeStruct((t, d), F32),
        scratch_shapes=[pltpu.VMEM((tm, d), F32)],
        compiler_params=_cparams(("parallel", "arbitrary")),
        name="ffn_dense",
    )(x, h, wg, wu, wd)


def _top2_route(logits):
    lane_i = lax.broadcasted_iota(jnp.int32, logits.shape, 1)
    lane = lane_i.astype(F32)
    neg = jnp.float32(-jnp.inf)
    z = jnp.where(lane_i < N_EXPERTS, logits, neg)
    m1 = jnp.max(z, axis=-1, keepdims=True)
    i1 = jnp.min(jnp.where(z == m1, lane, float(LANES)), axis=-1, keepdims=True)
    z2 = jnp.where(lane == i1, neg, z)
    m2 = jnp.max(z2, axis=-1, keepdims=True)
    i2 = jnp.min(jnp.where(z2 == m2, lane, float(LANES)), axis=-1, keepdims=True)
    e2 = jnp.exp(m2 - m1)
    w1 = 1.0 / (1.0 + e2)
    w2 = e2 / (1.0 + e2)
    return jnp.where(lane_i == 0, i1, jnp.where(lane_i == 1, i2,
                     jnp.where(lane_i == 2, w1, jnp.where(lane_i == 3, w2, 0.0))))


def _route_plan(route, tm):
    t = route.shape[0]
    e = route[:, 0:TOP_K].astype(jnp.int32)
    w = route[:, TOP_K:2 * TOP_K]
    sel = (e[:, :, None] == jnp.arange(N_EXPERTS, dtype=jnp.int32)).astype(jnp.int32).sum(axis=1)
    csum = jnp.cumsum(sel, axis=0)
    padded = (csum[-1] + tm - 1) // tm * tm
    end = jnp.cumsum(padded)
    pos = (end - padded)[e] + jnp.take_along_axis(csum - sel, e, axis=1)
    n_rows = TOP_K * t + N_EXPERTS * tm
    flat = pos.reshape(-1)
    tok = jnp.zeros((n_rows,), jnp.int32).at[flat].set(
        jnp.repeat(jnp.arange(t, dtype=jnp.int32), TOP_K), unique_indices=True)
    w_row = jnp.zeros((n_rows,), F32).at[flat].set(w.reshape(-1), unique_indices=True)
    n_tiles = n_rows // tm
    row0 = jnp.arange(n_tiles, dtype=jnp.int32) * tm
    valid = (row0 < end[-1]).astype(jnp.int32)
    tile_e = jnp.searchsorted(end, jnp.minimum(row0, end[-1] - 1), side="right").astype(jnp.int32)
    return tok.reshape(n_tiles, 1, tm), w_row.reshape(n_rows, 1), tile_e, valid, pos


def _moe_gmm_kernel(tile_e_ref, valid_ref, tok_ref, tok_next_ref, w_ref, h_hbm, wg_ref, wu_ref, wd_ref,
                    o_ref, xg_ref, xb_ref, acc_ref, sem):
    del tile_e_ref
    i = pl.program_id(0)
    f = pl.program_id(1)
    nf = pl.num_programs(1)
    tm = xb_ref.shape[0]
    slot = i % 2

    def gather_rows(tok, s):
        def body(r, c):
            pltpu.make_async_copy(h_hbm.at[pl.ds(tok[0, 0, r], 1), :],
                                  xg_ref.at[s, pl.ds(r, 1), :], sem.at[s]).start()
            return c
        lax.fori_loop(0, tm, body, 0, unroll=8)

    @pl.when((i == 0) & (f == 0))
    def _():
        gather_rows(tok_ref, 0)

    @pl.when(f == 0)
    def _():
        pltpu.make_async_copy(h_hbm.at[pl.ds(0, tm), :], xg_ref.at[slot], sem.at[slot]).wait()
        xb_ref[...] = xg_ref[slot].astype(BF16)

    @pl.when((f == 1) & (i + 1 < pl.num_programs(0)))
    def _():
        gather_rows(tok_next_ref, 1 - slot)

    valid = valid_ref[i] != 0

    @pl.when(valid)
    def _():
        y = jnp.dot(_swiglu_act(xb_ref[...], wg_ref[0], wu_ref[0]), wd_ref[0],
                    preferred_element_type=F32)

        @pl.when(f == 0)
        def _():
            acc_ref[...] = y

        @pl.when(f > 0)
        def _():
            acc_ref[...] += y

        @pl.when(f == nf - 1)
        def _():
            o_ref[...] = acc_ref[...] * w_ref[...]

    @pl.when(jnp.logical_not(valid) & (f == nf - 1))
    def _():
        o_ref[...] = jnp.zeros_like(o_ref)


def _moe_gmm(tok, w_row, tile_e, valid, h, wg, wu, wd, *, tf=896):
    n_tiles, _, tm = tok.shape
    d = h.shape[1]
    ff = wg.shape[2]
    assert ff // tf >= 2
    tok_spec = lambda imap: pl.BlockSpec((1, 1, tm), imap, memory_space=pltpu.SMEM)
    grid_spec = pltpu.PrefetchScalarGridSpec(
        num_scalar_prefetch=2,
        grid=(n_tiles, ff // tf),
        in_specs=[tok_spec(lambda i, f, te, va: (i, 0, 0)),
                  tok_spec(lambda i, f, te, va: (jnp.minimum(i + 1, n_tiles - 1), 0, 0)),
                  pl.BlockSpec((tm, 1), lambda i, f, te, va: (i, 0)),
                  pl.BlockSpec(memory_space=pl.ANY),
                  pl.BlockSpec((1, d, tf), lambda i, f, te, va: (te[i], 0, f)),
                  pl.BlockSpec((1, d, tf), lambda i, f, te, va: (te[i], 0, f)),
                  pl.BlockSpec((1, tf, d), lambda i, f, te, va: (te[i], f, 0))],
        out_specs=pl.BlockSpec((tm, d), lambda i, f, te, va: (i, 0)),
        scratch_shapes=[pltpu.VMEM((2, tm, d), F32), pltpu.VMEM((tm, d), BF16), pltpu.VMEM((tm, d), F32),
                        pltpu.SemaphoreType.DMA((2,))],
    )
    return pl.pallas_call(
        _moe_gmm_kernel,
        grid_spec=grid_spec,
        out_shape=jax.ShapeDtypeStruct((n_tiles * tm, d), F32),
        compiler_params=_cparams(("arbitrary", "arbitrary")),
        name="moe_gmm",
    )(tile_e, valid, tok, tok, w_row, h, wg, wu, wd)


def _moe_combine_kernel(pos_ref, pos_next_ref, x_ref, y_hbm, o_ref, buf_ref, sem):
    i = pl.program_id(0)
    tc = x_ref.shape[0]
    slot = i % 2

    def gather_rows(pos, s):
        def body(r, c):
            for k in range(TOP_K):
                pltpu.make_async_copy(y_hbm.at[pl.ds(pos[0, 0, k * tc + r], 1), :],
                                      buf_ref.at[s, pl.ds(k * tc + r, 1), :], sem.at[s]).start()
            return c
        lax.fori_loop(0, tc, body, 0, unroll=4)

    @pl.when(i == 0)
    def _():
        gather_rows(pos_ref, 0)

    @pl.when(i + 1 < pl.num_programs(0))
    def _():
        gather_rows(pos_next_ref, 1 - slot)

    pltpu.make_async_copy(y_hbm.at[pl.ds(0, TOP_K * tc), :], buf_ref.at[slot], sem.at[slot]).wait()
    y = buf_ref[slot]
    o_ref[...] = x_ref[...] + y[0:tc] + y[tc:2 * tc]


def _moe_combine(x, y_rows, pos, *, tc=512):
    t, d = x.shape
    n = t // tc
    pos_t = pos.reshape(n, tc, TOP_K).transpose(0, 2, 1).reshape(n, 1, TOP_K * tc)
    pos_spec = lambda imap: pl.BlockSpec((1, 1, TOP_K * tc), imap, memory_space=pltpu.SMEM)
    return pl.pallas_call(
        _moe_combine_kernel,
        grid=(n,),
        in_specs=[pos_spec(lambda i: (i, 0, 0)),
                  pos_spec(lambda i: (jnp.minimum(i + 1, n - 1), 0, 0)),
                  pl.BlockSpec((tc, d), lambda i: (i, 0)),
                  pl.BlockSpec(memory_space=pl.ANY)],
        out_specs=pl.BlockSpec((tc, d), lambda i: (i, 0)),
        out_shape=jax.ShapeDtypeStruct((t, d), F32),
        scratch_shapes=[pltpu.VMEM((2, TOP_K * tc, d), F32), pltpu.SemaphoreType.DMA((2,))],
        compiler_params=_cparams(("arbitrary",)),
        name="moe_combine",
    )(pos_t, pos_t, x, y_rows)


def _row(v, reps=1):
    return jnp.tile(v.astype(F32), reps).reshape(1, -1)


def kernel(x, positions, attn_norm_g, w_in, w_gate, mla_q_norm_g, mla_kv_norm_g, mla_w_uq, mla_w_ukv, mla_qn_nope_g, mla_qn_rope_g, mla_kn_nope_g, mla_kn_rope_g, diff_qn_g, diff_kn_g, diff_lam_q1, diff_lam_k1, diff_lam_q2, diff_lam_k2, diff_subln_g, w_branch_a, w_branch_b, w_out, ffn_norm_g, dense_w_gate, dense_w_up, dense_w_down, moe_w_router, moe_w_gate, moe_w_up, moe_w_down):
    batch, seq, d = x.shape
    t = batch * seq
    xf = x.reshape(t, d)
    pos_f = positions.astype(F32)
    cos, sin = _rope_tables(pos_f.reshape(t, 1))
    pos_q = pos_f.reshape(batch, seq, 1)
    pos_k = pos_f.reshape(batch, 1, seq)
    slopes = 2.0 ** (-8.0 * jnp.arange(1, DIFF_HEADS + 1, dtype=F32) / DIFF_HEADS)
    n_in_head = MLA_Q_LORA + MLA_KV_LORA + MLA_ROPE

    for layer in range(DEPTH):
        w1 = jnp.concatenate(
            [w_in[layer][:, :n_in_head], jnp.zeros((d, 1024 - n_in_head), F32),
             w_in[layer][:, n_in_head:], w_gate[layer]], axis=1).astype(BF16)
        wuq = mla_w_uq[layer].reshape(MLA_Q_LORA, MLA_HEADS, MLA_NOPE + MLA_ROPE)
        wuq = jnp.concatenate([wuq[:, :, :MLA_NOPE].reshape(MLA_Q_LORA, -1),
                               wuq[:, :, MLA_NOPE:].reshape(MLA_Q_LORA, -1)], axis=1).astype(BF16)
        wukv = mla_w_ukv[layer].reshape(MLA_KV_LORA, MLA_HEADS, MLA_NOPE + MLA_V)
        wukv = jnp.concatenate([wukv[:, :, :MLA_NOPE].reshape(MLA_KV_LORA, -1),
                                wukv[:, :, MLA_NOPE:].reshape(MLA_KV_LORA, -1)], axis=1).astype(BF16)
        gkr = jnp.concatenate([mla_kn_rope_g[layer].astype(F32), jnp.zeros((LANES - MLA_ROPE,), F32)]).reshape(1, -1)

        p = _norm_proj(xf, _row(attn_norm_g[layer]), w1)
        q_bound = MLA_Q_SCALE * jnp.sqrt(_norm_bound(MLA_NOPE, mla_qn_nope_g[layer]) ** 2
                                         + _norm_bound(MLA_ROPE, mla_qn_rope_g[layer]) ** 2)
        k_bound = jnp.sqrt(_norm_bound(MLA_NOPE, mla_kn_nope_g[layer]) ** 2
                           + _norm_bound(MLA_ROPE, mla_kn_rope_g[layer]) ** 2)
        shift, use_bounded = _score_shift(q_bound, k_bound)
        shift_row = jnp.where(jnp.arange(LANES) % 64 == 0, -shift, 0.0).astype(F32).reshape(1, LANES)
        q_m, k_m, v_m = _mla_prep(
            p, cos, sin, _row(mla_q_norm_g[layer]), _row(mla_kv_norm_g[layer]), wuq, wukv,
            _row(mla_qn_nope_g[layer]), _row(mla_qn_rope_g[layer], 2), _row(mla_kn_nope_g[layer]), gkr,
            shift_row, batch=batch, seq=seq)
        o_a = _mla_attn(use_bounded.astype(jnp.int32).reshape(1), q_m, k_m, v_m).reshape(t, -1)

        lambda_init = 0.8 - 0.6 * math.exp(-0.3 * layer)
        qn, kn = _diff_prep(p, _row(diff_qn_g[layer], 2), _row(diff_kn_g[layer], 2))
        lam4 = jnp.stack([diff_lam_q1[layer], diff_lam_k1[layer],
                          diff_lam_q2[layer], diff_lam_k2[layer]]).astype(F32)
        d_shift, d_bounded = _score_shift(DIFF_Q_SCALE * _norm_bound(DIFF_HEAD_DIM, diff_qn_g[layer]),
                                          _norm_bound(DIFF_HEAD_DIM, diff_kn_g[layer]))
        scal = jnp.concatenate([slopes, d_shift.reshape(1), d_bounded.astype(F32).reshape(1)])
        o_b = _diff_attn(scal, qn.reshape(batch, seq, -1), kn.reshape(batch, seq, -1),
                         p.reshape(batch, seq, -1), pos_q, pos_k, lam4, _row(diff_subln_g[layer]),
                         lambda_init=lambda_init).reshape(t, -1)

        j = layer // 2
        wa, wb, wo = (w.astype(BF16) for w in (w_branch_a[layer], w_branch_b[layer], w_out[layer]))
        if layer % 2 == 0:
            xf, h = _merge(xf, o_a, o_b, p, wa, wb, wo, _row(ffn_norm_g[layer]))
            xf = _ffn(xf, h, dense_w_gate[j].astype(BF16), dense_w_up[j].astype(BF16),
                      dense_w_down[j].astype(BF16))
        else:
            wr = jnp.pad(moe_w_router[j].astype(F32), ((0, 0), (0, LANES - N_EXPERTS)))
            wr_hi = wr.astype(BF16)
            wr_lo = (wr - wr_hi.astype(F32)).astype(BF16)
            xf, h, route = _merge(xf, o_a, o_b, p, wa, wb, wo, _row(ffn_norm_g[layer]),
                                  router=(wr_hi, wr_lo))
            tok, w_row, tile_e, valid, pos = _route_plan(route, MOE_TILE_ROWS)
            y_rows = _moe_gmm(tok, w_row, tile_e, valid, h, moe_w_gate[j].astype(BF16),
                              moe_w_up[j].astype(BF16), moe_w_down[j].astype(BF16))
            xf = _moe_combine(xf, y_rows, pos)
    return xf.reshape(batch, seq, d)
```

```python
import functools
import math

import jax
import jax.numpy as jnp
from jax import lax
from jax.experimental import pallas as pl
from jax.experimental.pallas import tpu as pltpu

D_MODEL = 1024
DEPTH = 2
MLA_HEADS = 8
MLA_Q_LORA = 512
MLA_KV_LORA = 256
MLA_NOPE = 128
MLA_ROPE = 64
MLA_V = 128
ROPE_THETA = 10000.0
DIFF_HEADS = 8
DIFF_HEAD_DIM = 64
D_FF_DENSE = 2816
N_EXPERTS = 8
TOP_K = 2
D_FF_EXPERT = 3584
EPS = 1e-6

LANES = 128
LOG2E = 1.4426950408889634
MLA_Q_SCALE = (MLA_NOPE + MLA_ROPE) ** -0.5 * LOG2E
DIFF_Q_SCALE = DIFF_HEAD_DIM ** -0.5 * LOG2E
NEG_BIG = -1e30
BF16 = jnp.bfloat16
F32 = jnp.float32

P_WIDTH = 6144
P_DQ_BLK = 1
P_DK_BLK = 2
P_DV_COL0 = 3072
P_GATE_COL0 = 4096
MOE_TILE_ROWS = 1024
VMEM_LIMIT = 52 * 1024 * 1024


def _cparams(sem):
    return pltpu.CompilerParams(dimension_semantics=sem, vmem_limit_bytes=VMEM_LIMIT)


def _rms(x, gain):
    ms = jnp.mean(x * x, axis=-1, keepdims=True)
    return x * lax.rsqrt(ms + EPS) * gain


def _ms_half_lanes(xb):
    lane = lax.broadcasted_iota(jnp.int32, xb.shape, 1)
    lo = lane < 64
    sq = xb * xb
    s_lo = jnp.sum(jnp.where(lo, sq, 0.0), axis=-1, keepdims=True)
    s_hi = jnp.sum(jnp.where(lo, 0.0, sq), axis=-1, keepdims=True)
    return jnp.where(lo, s_lo, s_hi) * (1.0 / 64.0)


def _norm_proj_kernel(x_ref, g_ref, w_ref, o_ref, h_ref, *, gate_tile0):
    j = pl.program_id(1)

    @pl.when(j == 0)
    def _():
        h_ref[...] = _rms(x_ref[...], g_ref[...]).astype(BF16)

    y = jnp.dot(h_ref[...], w_ref[...], preferred_element_type=F32)

    @pl.when(j < gate_tile0)
    def _():
        o_ref[...] = y.astype(BF16)

    @pl.when(j >= gate_tile0)
    def _():
        o_ref[...] = jax.nn.sigmoid(y).astype(BF16)


def _norm_proj(x, gain, w, *, tm=1024, tn=2048):
    t, d = x.shape
    n = w.shape[1]
    return pl.pallas_call(
        functools.partial(_norm_proj_kernel, gate_tile0=P_GATE_COL0 // tn),
        grid=(t // tm, n // tn),
        in_specs=[
            pl.BlockSpec((tm, d), lambda i, j: (i, 0)),
            pl.BlockSpec((1, d), lambda i, j: (0, 0)),
            pl.BlockSpec((d, tn), lambda i, j: (0, j)),
        ],
        out_specs=pl.BlockSpec((tm, tn), lambda i, j: (i, j)),
        out_shape=jax.ShapeDtypeStruct((t, n), BF16),
        scratch_shapes=[pltpu.VMEM((tm, d), BF16)],
        compiler_params=_cparams(("parallel", "arbitrary")),
        name="norm_proj",
    )(x, gain, w)


def _rope_table_kernel(pos_ref, freq_ref, cos_ref, sin_ref):
    ang = pos_ref[...] * freq_ref[...]
    lane = lax.broadcasted_iota(jnp.int32, ang.shape, 1)
    first_half = (lane % 64) < 32
    cos_ref[...] = jnp.cos(ang)
    sin_ref[...] = jnp.where(first_half, -1.0, 1.0) * jnp.sin(ang)


def _rope_tables(pos_col, *, tm=2048):
    t = pos_col.shape[0]
    half = MLA_ROPE // 2
    inv_freq = ROPE_THETA ** (-jnp.arange(half, dtype=F32) / half)
    freq = jnp.tile(inv_freq, LANES // half).reshape(1, LANES)
    return pl.pallas_call(
        _rope_table_kernel,
        grid=(t // tm,),
        in_specs=[pl.BlockSpec((tm, 1), lambda i: (i, 0)),
                  pl.BlockSpec((1, LANES), lambda i: (0, 0))],
        out_specs=[pl.BlockSpec((tm, LANES), lambda i: (i, 0))] * 2,
        out_shape=[jax.ShapeDtypeStruct((t, LANES), F32)] * 2,
        compiler_params=_cparams(("parallel",)),
        name="rope_tables",
    )(pos_col, freq)


def _rope(n, cos, sin_signed):
    w = n.shape[-1]
    lane = lax.broadcasted_iota(jnp.int32, n.shape, 1)
    first_half = (lane % 64) < 32
    swapped = jnp.where(first_half, pltpu.roll(n, w - 32, 1), pltpu.roll(n, 32, 1))
    return n * cos + swapped * sin_signed


def _mla_prep_kernel(p_ref, cos_ref, sin_ref, gq_ref, gkv_ref, wuq_ref, wukv_ref,
                     gqn_ref, gqr_ref, gkn_ref, gkr_ref, shift_ref, q_ref, k_ref, v_ref, *, q_scale):
    cq = p_ref[:, 0:MLA_Q_LORA].astype(F32)
    ckv = p_ref[:, MLA_Q_LORA:MLA_Q_LORA + MLA_KV_LORA].astype(F32)
    kr = p_ref[:, 768:896].astype(F32)
    cos = cos_ref[...]
    sin = sin_ref[...]

    q = jnp.dot(_rms(cq, gq_ref[...]).astype(BF16), wuq_ref[...], preferred_element_type=F32)
    kv = jnp.dot(_rms(ckv, gkv_ref[...]).astype(BF16), wukv_ref[...], preferred_element_type=F32)

    n_nope = MLA_HEADS * MLA_NOPE
    kr_ms = jnp.sum(kr * kr, axis=-1, keepdims=True) * (1.0 / MLA_ROPE)
    kr_n = kr * lax.rsqrt(kr_ms + EPS) * gkr_ref[...]
    kr_r = _rope(kr_n, cos, sin)
    kr_dup = kr_r + pltpu.roll(kr_r, 64, 1)

    lane = lax.broadcasted_iota(jnp.int32, (q.shape[0], LANES), 1)
    one_hot = jnp.where(lane % 64 == 0, 1.0, 0.0)
    for pair in range(MLA_HEADS // 2):
        qr = q[:, n_nope + pair * LANES:n_nope + (pair + 1) * LANES]
        qr_n = qr * lax.rsqrt(_ms_half_lanes(qr) + EPS) * gqr_ref[...]
        qr_r = _rope(qr_n, cos, sin) * q_scale
        for sub in range(2):
            h = 2 * pair + sub
            keep = (lane < 64) if sub == 0 else (lane >= 64)
            qn = _rms(q[:, h * MLA_NOPE:(h + 1) * MLA_NOPE], gqn_ref[...]) * q_scale
            q_ref[0, h, :, 0:LANES] = qn.astype(BF16)
            q_ref[0, h, :, LANES:2 * LANES] = jnp.where(keep, qr_r, shift_ref[...]).astype(BF16)
            kn = _rms(kv[:, h * MLA_NOPE:(h + 1) * MLA_NOPE], gkn_ref[...])
            k_ref[0, h, :, 0:LANES] = kn.astype(BF16)
            k_ref[0, h, :, LANES:2 * LANES] = jnp.where(keep, kr_dup, one_hot).astype(BF16)
            v_ref[0, h, :, :] = kv[:, n_nope + h * MLA_V:n_nope + (h + 1) * MLA_V].astype(BF16)


def _mla_prep(p, cos, sin, gq, gkv, wuq, wukv, gqn, gqr, gkn, gkr, shift_row, *, batch, seq, tm=512):
    nb = seq // tm
    full = lambda a: pl.BlockSpec(a.shape, lambda b, i: (0,) * a.ndim)
    row = lambda b, i: (b * nb + i, 0)
    hshape = lambda w: jax.ShapeDtypeStruct((batch, MLA_HEADS, seq, w), BF16)
    hspec = lambda w: pl.BlockSpec((1, MLA_HEADS, tm, w), lambda b, i: (b, 0, i, 0))
    return pl.pallas_call(
        functools.partial(_mla_prep_kernel, q_scale=MLA_Q_SCALE),
        grid=(batch, nb),
        in_specs=[pl.BlockSpec((tm, 1024), row),
                  pl.BlockSpec((tm, LANES), row), pl.BlockSpec((tm, LANES), row),
                  full(gq), full(gkv), full(wuq), full(wukv),
                  full(gqn), full(gqr), full(gkn), full(gkr), full(shift_row)],
        out_specs=[hspec(2 * LANES), hspec(2 * LANES), hspec(MLA_V)],
        out_shape=[hshape(2 * LANES), hshape(2 * LANES), hshape(MLA_V)],
        compiler_params=_cparams(("parallel", "parallel")),
        name="mla_prep",
    )(p, cos, sin, gq, gkv, wuq, wukv, gqn, gqr, gkn, gkr, shift_row)


BOUND_MARGIN = 1.02
MAX_SHIFT_SPAN = 100.0


def _norm_bound(n, gain):
    return math.sqrt(n) * jnp.max(jnp.abs(gain.astype(F32)))


def _score_shift(q_bound, k_bound):
    shift = (BOUND_MARGIN * q_bound * k_bound).astype(BF16).astype(F32)
    return shift, (2.0 * shift <= MAX_SHIFT_SPAN)


def _nt_dot(a, b):
    return lax.dot_general(a, b, (((1,), (1,)), ((), ())), preferred_element_type=F32)


def _lane_partial_sum(p):
    acc = p[:, 0:LANES]
    for c in range(1, p.shape[1] // LANES):
        acc = acc + p[:, c * LANES:(c + 1) * LANES]
    return acc


def _online_update(s, m, l):
    m_new = jnp.maximum(m, jnp.max(s, axis=-1, keepdims=True))
    alpha = jnp.exp2(m - m_new)
    p = jnp.exp2(s - m_new)
    l_new = alpha * l + jnp.sum(p, axis=-1, keepdims=True)
    return p, alpha, m_new, l_new


def _mla_attn_kernel(flag_ref, q_ref, k_ref, v_ref, o_ref, *, tk, nk, unroll):
    q = q_ref[0, 0]
    tq = q.shape[0]

    def chunk(j):
        off = pl.multiple_of(j * tk, tk)
        return k_ref[0, 0, pl.ds(off, tk), :], v_ref[0, 0, pl.ds(off, tk), :]

    def bounded():
        def body(j, carry):
            l, acc = carry
            k, v = chunk(j)
            p = jnp.exp2(_nt_dot(q, k))
            return (l + _lane_partial_sum(p),
                    acc + jnp.dot(p.astype(BF16), v, preferred_element_type=F32))

        zero = jnp.zeros((tq, LANES), F32)
        l, acc = lax.fori_loop(0, nk, body, (zero, zero), unroll=unroll)
        return acc / jnp.sum(l, axis=-1, keepdims=True)

    def online():
        def body(j, carry):
            m, l, acc = carry
            k, v = chunk(j)
            p, alpha, m, l = _online_update(_nt_dot(q, k), m, l)
            return m, l, alpha * acc + jnp.dot(p.astype(BF16), v, preferred_element_type=F32)

        init = (jnp.full((tq, 1), NEG_BIG, F32), jnp.zeros((tq, 1), F32), jnp.zeros((tq, MLA_V), F32))
        _, l, acc = lax.fori_loop(0, nk, body, init)
        return acc / l

    o_ref[0] = lax.cond(flag_ref[0] != 0, bounded, online).astype(BF16)


def _mla_attn(flag, q, k, v, *, tq=1024, tk=512, unroll=4):
    b, h, s, w = q.shape
    return pl.pallas_call(
        functools.partial(_mla_attn_kernel, tk=tk, nk=s // tk, unroll=unroll),
        grid=(b, h, s // tq),
        in_specs=[pl.BlockSpec(memory_space=pltpu.SMEM),
                  pl.BlockSpec((1, 1, tq, w), lambda bi, hi, qi: (bi, hi, qi, 0)),
                  pl.BlockSpec((1, 1, s, w), lambda bi, hi, qi: (bi, hi, 0, 0)),
                  pl.BlockSpec((1, 1, s, MLA_V), lambda bi, hi, qi: (bi, hi, 0, 0))],
        out_specs=pl.BlockSpec((1, tq, MLA_V), lambda bi, hi, qi: (bi, qi, hi)),
        out_shape=jax.ShapeDtypeStruct((b, s, h * MLA_V), BF16),
        compiler_params=_cparams(("parallel", "parallel", "arbitrary")),
        name="mla_attn",
    )(flag, q, k, v)


def _diff_prep_kernel(q_ref, k_ref, gq_ref, gk_ref, qn_ref, kn_ref, *, q_scale):
    for c in range(DIFF_HEADS):
        sl = slice(c * LANES, (c + 1) * LANES)
        q = q_ref[:, sl].astype(F32)
        k = k_ref[:, sl].astype(F32)
        qn_ref[:, sl] = (q * lax.rsqrt(_ms_half_lanes(q) + EPS) * (gq_ref[...] * q_scale)).astype(BF16)
        kn_ref[:, sl] = (k * lax.rsqrt(_ms_half_lanes(k) + EPS) * gk_ref[...]).astype(BF16)


def _diff_prep(p, gq, gk, *, tm=1024):
    t = p.shape[0]
    w = DIFF_HEADS * 2 * DIFF_HEAD_DIM
    return pl.pallas_call(
        functools.partial(_diff_prep_kernel, q_scale=DIFF_Q_SCALE),
        grid=(t // tm,),
        in_specs=[pl.BlockSpec((tm, w), lambda i: (i, P_DQ_BLK)),
                  pl.BlockSpec((tm, w), lambda i: (i, P_DK_BLK)),
                  pl.BlockSpec((1, LANES), lambda i: (0, 0)),
                  pl.BlockSpec((1, LANES), lambda i: (0, 0))],
        out_specs=[pl.BlockSpec((tm, w), lambda i: (i, 0))] * 2,
        out_shape=[jax.ShapeDtypeStruct((t, w), BF16)] * 2,
        compiler_params=_cparams(("parallel",)),
        name="diff_prep",
    )(p, p, gq, gk)


def _diff_attn_kernel(scal_ref, q_ref, k_ref, v_ref, pq_ref, pk_ref, lam_ref, g_ref, o_ref,
                      *, tk, nk, unroll, lambda_init):
    q = q_ref[0]
    tq = q.shape[0]
    lane = lax.broadcasted_iota(jnp.int32, q.shape, 1)
    zero = jnp.zeros_like(q)
    q0 = jnp.where(lane < DIFF_HEAD_DIM, q, zero)
    q1 = jnp.where(lane < DIFF_HEAD_DIM, zero, q)
    pq = pq_ref[0]
    slope = scal_ref[pl.program_id(1)] * LOG2E
    shift = scal_ref[DIFF_HEADS]
    use_bounded = scal_ref[DIFF_HEADS + 1]

    def chunk(j):
        off = pl.multiple_of(j * tk, tk)
        k = k_ref[0, pl.ds(off, tk), :]
        v = v_ref[0, pl.ds(off, tk), :]
        bias = slope * jnp.abs(pq - pk_ref[0, :, pl.ds(off, tk)]) + shift
        return k, v, bias

    def bounded():
        def body(j, carry):
            l0, a0, l1, a1 = carry
            k, v, bias = chunk(j)
            p0 = jnp.exp2(_nt_dot(q0, k) - bias)
            p1 = jnp.exp2(_nt_dot(q1, k) - bias)
            return (l0 + _lane_partial_sum(p0),
                    a0 + jnp.dot(p0.astype(BF16), v, preferred_element_type=F32),
                    l1 + _lane_partial_sum(p1),
                    a1 + jnp.dot(p1.astype(BF16), v, preferred_element_type=F32))

        z = jnp.zeros((tq, LANES), F32)
        l0, a0, l1, a1 = lax.fori_loop(0, nk, body, (z, z, z, z), unroll=unroll)
        return (a0 / jnp.sum(l0, axis=-1, keepdims=True), a1 / jnp.sum(l1, axis=-1, keepdims=True))

    def online():
        def body(j, carry):
            m0, l0, a0, m1, l1, a1 = carry
            k, v, bias = chunk(j)
            p0, al0, m0, l0 = _online_update(_nt_dot(q0, k) - bias, m0, l0)
            a0 = al0 * a0 + jnp.dot(p0.astype(BF16), v, preferred_element_type=F32)
            p1, al1, m1, l1 = _online_update(_nt_dot(q1, k) - bias, m1, l1)
            a1 = al1 * a1 + jnp.dot(p1.astype(BF16), v, preferred_element_type=F32)
            return m0, l0, a0, m1, l1, a1

        neg = jnp.full((tq, 1), NEG_BIG, F32)
        z1 = jnp.zeros((tq, 1), F32)
        za = jnp.zeros((tq, LANES), F32)
        _, l0, a0, _, l1, a1 = lax.fori_loop(0, nk, body, (neg, z1, za, neg, z1, za))
        return a0 / l0, a1 / l1

    o0, o1 = lax.cond(use_bounded != 0.0, bounded, online)
    lam = (jnp.exp(jnp.sum(lam_ref[0:1, :] * lam_ref[1:2, :], axis=-1, keepdims=True))
           - jnp.exp(jnp.sum(lam_ref[2:3, :] * lam_ref[3:4, :], axis=-1, keepdims=True))
           + lambda_init)
    o = o0 - lam * o1
    o_ref[0] = (_rms(o, g_ref[...]) * (1.0 - lambda_init)).astype(BF16)


def _diff_attn(scal, qn, kn, p, pos_q, pos_k, lam4, subln_g, *, lambda_init, tq=1024, tk=512, unroll=4):
    b, s, _ = qn.shape
    dv_blk0 = P_DV_COL0 // LANES
    return pl.pallas_call(
        functools.partial(_diff_attn_kernel, tk=tk, nk=s // tk, unroll=unroll, lambda_init=lambda_init),
        grid=(b, DIFF_HEADS, s // tq),
        in_specs=[pl.BlockSpec(memory_space=pltpu.SMEM),
                  pl.BlockSpec((1, tq, LANES), lambda bi, hi, qi: (bi, qi, hi)),
                  pl.BlockSpec((1, s, LANES), lambda bi, hi, qi: (bi, 0, hi)),
                  pl.BlockSpec((1, s, LANES), lambda bi, hi, qi: (bi, 0, dv_blk0 + hi)),
                  pl.BlockSpec((1, tq, 1), lambda bi, hi, qi: (bi, qi, 0)),
                  pl.BlockSpec((1, 1, s), lambda bi, hi, qi: (bi, 0, 0)),
                  pl.BlockSpec((4, DIFF_HEAD_DIM), lambda bi, hi, qi: (0, 0)),
                  pl.BlockSpec((1, LANES), lambda bi, hi, qi: (0, 0))],
        out_specs=pl.BlockSpec((1, tq, LANES), lambda bi, hi, qi: (bi, qi, hi)),
        out_shape=jax.ShapeDtypeStruct((b, s, DIFF_HEADS * LANES), BF16),
        compiler_params=_cparams(("parallel", "parallel", "arbitrary")),
        name="diff_attn",
    )(scal, qn, kn, p, pos_q, pos_k, lam4, subln_g)


def _split_dot_f32(a, w_hi, w_lo):
    a_hi = a.astype(BF16)
    a_lo = (a - a_hi.astype(F32)).astype(BF16)
    return (jnp.dot(a_hi, w_hi, preferred_element_type=F32)
            + jnp.dot(a_lo, w_hi, preferred_element_type=F32)
            + jnp.dot(a_hi, w_lo, preferred_element_type=F32))


def _merge_kernel(x_ref, oa_ref, ob_ref, ga_ref, gb_ref, wa_ref, wb_ref, wo_ref, gf_ref, *rest,
                  with_router):
    if with_router:
        wr_hi_ref, wr_lo_ref, xo_ref, h_ref, route_ref = rest
    else:
        xo_ref, h_ref = rest
    ya = jnp.dot(oa_ref[...], wa_ref[...], preferred_element_type=F32)
    yb = jnp.dot(ob_ref[...], wb_ref[...], preferred_element_type=F32)
    merged = ga_ref[...].astype(F32) * ya + gb_ref[...].astype(F32) * yb
    x = x_ref[...] + jnp.dot(merged.astype(BF16), wo_ref[...], preferred_element_type=F32)
    xo_ref[...] = x
    h = _rms(x, gf_ref[...])
    h_ref[...] = h.astype(h_ref.dtype)
    if with_router:
        route_ref[...] = _top2_route(_split_dot_f32(h, wr_hi_ref[...], wr_lo_ref[...]))


def _merge(x, oa, ob, p, wa, wb, wo, gf, router=None, *, tm=512):
    t, d = x.shape
    ga_blk = P_GATE_COL0 // d
    row = lambda i: (i, 0)
    full = lambda a: pl.BlockSpec(a.shape, lambda i: (0,) * a.ndim)
    in_specs = [pl.BlockSpec((tm, d), row), pl.BlockSpec((tm, d), row), pl.BlockSpec((tm, d), row),
                pl.BlockSpec((tm, d), lambda i: (i, ga_blk)),
                pl.BlockSpec((tm, d), lambda i: (i, ga_blk + 1)),
                full(wa), full(wb), full(wo), full(gf)]
    args = [x, oa, ob, p, p, wa, wb, wo, gf]
    out_specs = [pl.BlockSpec((tm, d), row), pl.BlockSpec((tm, d), row)]
    h_dtype = BF16 if router is None else F32
    out_shape = [jax.ShapeDtypeStruct((t, d), F32), jax.ShapeDtypeStruct((t, d), h_dtype)]
    if router is not None:
        in_specs += [full(router[0]), full(router[1])]
        args += list(router)
        out_specs.append(pl.BlockSpec((tm, LANES), row))
        out_shape.append(jax.ShapeDtypeStruct((t, LANES), F32))
    return pl.pallas_call(
        functools.partial(_merge_kernel, with_router=router is not None),
        grid=(t // tm,),
        in_specs=in_specs,
        out_specs=out_specs,
        out_shape=out_shape,
        compiler_params=_cparams(("parallel",)),
        name="merge_router" if router is not None else "merge",
    )(*args)


def _swiglu_act(h, wg, wu):
    g = jnp.dot(h, wg, preferred_element_type=F32)
    u = jnp.dot(h, wu, preferred_element_type=F32)
    return (g * jax.nn.sigmoid(g) * u).astype(BF16)


def _ffn_kernel(x_ref, h_ref, wg_ref, wu_ref, wd_ref, o_ref, acc_ref):
    f = pl.program_id(1)
    y = jnp.dot(_swiglu_act(h_ref[...], wg_ref[...], wu_ref[...]), wd_ref[...],
                preferred_element_type=F32)

    @pl.when(f == 0)
    def _():
        acc_ref[...] = x_ref[...] + y

    @pl.when(f > 0)
    def _():
        acc_ref[...] += y

    @pl.when(f == pl.num_programs(1) - 1)
    def _():
        o_ref[...] = acc_ref[...]


def _ffn(x, h, wg, wu, wd, *, tm=1024, tf=1408):
    t, d = x.shape
    ff = wg.shape[1]
    return pl.pallas_call(
        _ffn_kernel,
        grid=(t // tm, ff // tf),
        in_specs=[pl.BlockSpec((tm, d), lambda i, f: (i, 0)),
                  pl.BlockSpec((tm, d), lambda i, f: (i, 0)),
                  pl.BlockSpec((d, tf), lambda i, f: (0, f)),
                  pl.BlockSpec((d, tf), lambda i, f: (0, f)),
                  pl.BlockSpec((tf, d), lambda i, f: (f, 0))],
        out_specs=pl.BlockSpec((tm, d), lambda i, f: (i, 0)),
        out_shape=jax.ShapeDtypeStruct((t, d), F32),
        scratch_shapes=[pltpu.VMEM((tm, d), F32)],
        compiler_params=_cparams(("parallel", "arbitrary")),
        name="ffn_dense",
    )(x, h, wg, wu, wd)


def _top2_route(logits):
    lane_i = lax.broadcasted_iota(jnp.int32, logits.shape, 1)
    lane = lane_i.astype(F32)
    neg = jnp.float32(-jnp.inf)
    z = jnp.where(lane_i < N_EXPERTS, logits, neg)
    m1 = jnp.max(z, axis=-1, keepdims=True)
    i1 = jnp.min(jnp.where(z == m1, lane, float(LANES)), axis=-1, keepdims=True)
    z2 = jnp.where(lane == i1, neg, z)
    m2 = jnp.max(z2, axis=-1, keepdims=True)
    i2 = jnp.min(jnp.where(z2 == m2, lane, float(LANES)), axis=-1, keepdims=True)
    e2 = jnp.exp(m2 - m1)
    w1 = 1.0 / (1.0 + e2)
    w2 = e2 / (1.0 + e2)
    return jnp.where(lane_i == 0, i1, jnp.where(lane_i == 1, i2,
                     jnp.where(lane_i == 2, w1, jnp.where(lane_i == 3, w2, 0.0))))


def _route_plan(route, tm):
    t = route.shape[0]
    e = route[:, 0:TOP_K].astype(jnp.int32)
    sel = (e[:, :, None] == jnp.arange(N_EXPERTS, dtype=jnp.int32)).astype(jnp.int32).sum(axis=1)
    csum = jnp.cumsum(sel, axis=0)
    padded = (csum[-1] + tm - 1) // tm * tm
    end = jnp.cumsum(padded)
    pos = (end - padded)[e] + jnp.take_along_axis(csum - sel, e, axis=1)
    n_rows = TOP_K * t + N_EXPERTS * tm
    flat = pos.reshape(-1)
    tok = jnp.zeros((n_rows,), jnp.int32).at[flat].set(
        jnp.repeat(jnp.arange(t, dtype=jnp.int32), TOP_K), unique_indices=True)
    n_tiles = n_rows // tm
    row0 = jnp.arange(n_tiles, dtype=jnp.int32) * tm
    valid = (row0 < end[-1]).astype(jnp.int32)
    tile_e = jnp.searchsorted(end, jnp.minimum(row0, end[-1] - 1), side="right").astype(jnp.int32)
    return tok.reshape(n_tiles, 1, tm), tile_e, valid, pos


def _moe_gmm_kernel(tile_e_ref, valid_ref, tok_ref, tok_next_ref, h_hbm, wg_ref, wu_ref, wd_ref,
                    o_ref, xg_ref, xb_ref, acc_ref, sem, *, nf, gather_steps):
    del tile_e_ref
    i = pl.program_id(0)
    f = pl.program_id(1)
    tm = xb_ref.shape[0]
    slot = i % 2

    def row_copy(tok, r, s):
        return pltpu.make_async_copy(h_hbm.at[pl.ds(tok[0, 0, r], 1), :],
                                     xg_ref.at[s, pl.ds(r, 1), :], sem.at[s])

    @pl.when((i == 0) & (f == 0))
    def _():
        def body(r, c):
            row_copy(tok_ref, r, 0).start()
            return c
        lax.fori_loop(0, tm, body, 0, unroll=8)

    valid = valid_ref[i] != 0
    prev_valid = valid_ref[jnp.maximum(i - 1, 0)] != 0

    @pl.when((f == 0) & ((i == 0) | prev_valid))
    def _():
        pltpu.make_async_copy(h_hbm.at[pl.ds(0, tm), :], xg_ref.at[slot], sem.at[slot]).wait()
        xb_ref[...] = xg_ref[slot].astype(BF16)

    def step(request_rows):
        if request_rows:
            rows = tm // gather_steps
            for r in range(rows):
                row_copy(tok_next_ref, f * rows + r, 1 - slot).start()
        y = jnp.dot(_swiglu_act(xb_ref[...], wg_ref[0], wu_ref[0]), wd_ref[0],
                    preferred_element_type=F32)

        @pl.when(f == 0)
        def _():
            acc_ref[...] = y

        @pl.when((f > 0) & (f < nf - 1))
        def _():
            acc_ref[...] += y

        @pl.when(f == nf - 1)
        def _():
            o_ref[...] = acc_ref[...] + y

    pl.when(valid & (f < gather_steps))(functools.partial(step, True))
    pl.when(valid & (f >= gather_steps))(functools.partial(step, False))

    @pl.when(jnp.logical_not(valid) & (f == nf - 1))
    def _():
        o_ref[...] = jnp.zeros_like(o_ref)


def _moe_gmm(tok, tile_e, valid, h, wg, wu, wd, *, tf=896, gather_steps=4):
    n_tiles, _, tm = tok.shape
    d = h.shape[1]
    ff = wg.shape[2]
    nf = ff // tf
    assert 2 <= gather_steps <= nf and tm % gather_steps == 0
    tok_spec = lambda imap: pl.BlockSpec((1, 1, tm), imap, memory_space=pltpu.SMEM)
    grid_spec = pltpu.PrefetchScalarGridSpec(
        num_scalar_prefetch=2,
        grid=(n_tiles, nf),
        in_specs=[tok_spec(lambda i, f, te, va: (i, 0, 0)),
                  tok_spec(lambda i, f, te, va: (jnp.minimum(i + 1, n_tiles - 1), 0, 0)),
                  pl.BlockSpec(memory_space=pl.ANY),
                  pl.BlockSpec((1, d, tf), lambda i, f, te, va: (te[i], 0, f)),
                  pl.BlockSpec((1, d, tf), lambda i, f, te, va: (te[i], 0, f)),
                  pl.BlockSpec((1, tf, d), lambda i, f, te, va: (te[i], f, 0))],
        out_specs=pl.BlockSpec((tm, d), lambda i, f, te, va: (i, 0)),
        scratch_shapes=[pltpu.VMEM((2, tm, d), F32), pltpu.VMEM((tm, d), BF16), pltpu.VMEM((tm, d), F32),
                        pltpu.SemaphoreType.DMA((2,))],
    )
    return pl.pallas_call(
        functools.partial(_moe_gmm_kernel, nf=nf, gather_steps=gather_steps),
        grid_spec=grid_spec,
        out_shape=jax.ShapeDtypeStruct((n_tiles * tm, d), F32),
        compiler_params=_cparams(("arbitrary", "arbitrary")),
        name="moe_gmm",
    )(tile_e, valid, tok, tok, h, wg, wu, wd)


def _moe_combine_kernel(pos_ref, pos_next_ref, x_ref, route_ref, y_hbm, o_ref, buf_ref, sem):
    i = pl.program_id(0)
    tc = x_ref.shape[0]
    slot = i % 2

    def gather_rows(pos, s):
        def body(r, c):
            for k in range(TOP_K):
                pltpu.make_async_copy(y_hbm.at[pl.ds(pos[0, 0, k * tc + r], 1), :],
                                      buf_ref.at[s, pl.ds(k * tc + r, 1), :], sem.at[s]).start()
            return c
        lax.fori_loop(0, tc, body, 0, unroll=4)

    @pl.when(i == 0)
    def _():
        gather_rows(pos_ref, 0)

    @pl.when(i + 1 < pl.num_programs(0))
    def _():
        gather_rows(pos_next_ref, 1 - slot)

    pltpu.make_async_copy(y_hbm.at[pl.ds(0, TOP_K * tc), :], buf_ref.at[slot], sem.at[slot]).wait()
    y = buf_ref[slot]
    w = route_ref[...]
    o_ref[...] = (x_ref[...] + w[:, TOP_K:TOP_K + 1] * y[0:tc]
                  + w[:, TOP_K + 1:TOP_K + 2] * y[tc:2 * tc])


def _moe_combine(x, route, y_rows, pos, *, tc=512):
    t, d = x.shape
    n = t // tc
    pos_t = pos.reshape(n, tc, TOP_K).transpose(0, 2, 1).reshape(n, 1, TOP_K * tc)
    pos_spec = lambda imap: pl.BlockSpec((1, 1, TOP_K * tc), imap, memory_space=pltpu.SMEM)
    return pl.pallas_call(
        _moe_combine_kernel,
        grid=(n,),
        in_specs=[pos_spec(lambda i: (i, 0, 0)),
                  pos_spec(lambda i: (jnp.minimum(i + 1, n - 1), 0, 0)),
                  pl.BlockSpec((tc, d), lambda i: (i, 0)),
                  pl.BlockSpec((tc, LANES), lambda i: (i, 0)),
                  pl.BlockSpec(memory_space=pl.ANY)],
        out_specs=pl.BlockSpec((tc, d), lambda i: (i, 0)),
        out_shape=jax.ShapeDtypeStruct((t, d), F32),
        scratch_shapes=[pltpu.VMEM((2, TOP_K * tc, d), F32), pltpu.SemaphoreType.DMA((2,))],
        compiler_params=_cparams(("arbitrary",)),
        name="moe_combine",
    )(pos_t, pos_t, x, route, y_rows)


def _row(v, reps=1):
    return jnp.tile(v.astype(F32), reps).reshape(1, -1)


def kernel(x, positions, attn_norm_g, w_in, w_gate, mla_q_norm_g, mla_kv_norm_g, mla_w_uq, mla_w_ukv, mla_qn_nope_g, mla_qn_rope_g, mla_kn_nope_g, mla_kn_rope_g, diff_qn_g, diff_kn_g, diff_lam_q1, diff_lam_k1, diff_lam_q2, diff_lam_k2, diff_subln_g, w_branch_a, w_branch_b, w_out, ffn_norm_g, dense_w_gate, dense_w_up, dense_w_down, moe_w_router, moe_w_gate, moe_w_up, moe_w_down):
    batch, seq, d = x.shape
    t = batch * seq
    xf = x.reshape(t, d)
    pos_f = positions.astype(F32)
    cos, sin = _rope_tables(pos_f.reshape(t, 1))
    pos_q = pos_f.reshape(batch, seq, 1)
    pos_k = pos_f.reshape(batch, 1, seq)
    slopes = 2.0 ** (-8.0 * jnp.arange(1, DIFF_HEADS + 1, dtype=F32) / DIFF_HEADS)
    n_in_head = MLA_Q_LORA + MLA_KV_LORA + MLA_ROPE

    for layer in range(DEPTH):
        w1 = jnp.concatenate(
            [w_in[layer][:, :n_in_head], jnp.zeros((d, 1024 - n_in_head), F32),
             w_in[layer][:, n_in_head:], w_gate[layer]], axis=1).astype(BF16)
        wuq = mla_w_uq[layer].reshape(MLA_Q_LORA, MLA_HEADS, MLA_NOPE + MLA_ROPE)
        wuq = jnp.concatenate([wuq[:, :, :MLA_NOPE].reshape(MLA_Q_LORA, -1),
                               wuq[:, :, MLA_NOPE:].reshape(MLA_Q_LORA, -1)], axis=1).astype(BF16)
        wukv = mla_w_ukv[layer].reshape(MLA_KV_LORA, MLA_HEADS, MLA_NOPE + MLA_V)
        wukv = jnp.concatenate([wukv[:, :, :MLA_NOPE].reshape(MLA_KV_LORA, -1),
                                wukv[:, :, MLA_NOPE:].reshape(MLA_KV_LORA, -1)], axis=1).astype(BF16)
        gkr = jnp.concatenate([mla_kn_rope_g[layer].astype(F32), jnp.zeros((LANES - MLA_ROPE,), F32)]).reshape(1, -1)

        p = _norm_proj(xf, _row(attn_norm_g[layer]), w1)
        q_bound = MLA_Q_SCALE * jnp.sqrt(_norm_bound(MLA_NOPE, mla_qn_nope_g[layer]) ** 2
                                         + _norm_bound(MLA_ROPE, mla_qn_rope_g[layer]) ** 2)
        k_bound = jnp.sqrt(_norm_bound(MLA_NOPE, mla_kn_nope_g[layer]) ** 2
                           + _norm_bound(MLA_ROPE, mla_kn_rope_g[layer]) ** 2)
        shift, use_bounded = _score_shift(q_bound, k_bound)
        shift_row = jnp.where(jnp.arange(LANES) % 64 == 0, -shift, 0.0).astype(F32).reshape(1, LANES)
        q_m, k_m, v_m = _mla_prep(
            p, cos, sin, _row(mla_q_norm_g[layer]), _row(mla_kv_norm_g[layer]), wuq, wukv,
            _row(mla_qn_nope_g[layer]), _row(mla_qn_rope_g[layer], 2), _row(mla_kn_nope_g[layer]), gkr,
            shift_row, batch=batch, seq=seq)
        o_a = _mla_attn(use_bounded.astype(jnp.int32).reshape(1), q_m, k_m, v_m).reshape(t, -1)

        lambda_init = 0.8 - 0.6 * math.exp(-0.3 * layer)
        qn, kn = _diff_prep(p, _row(diff_qn_g[layer], 2), _row(diff_kn_g[layer], 2))
        lam4 = jnp.stack([diff_lam_q1[layer], diff_lam_k1[layer],
                          diff_lam_q2[layer], diff_lam_k2[layer]]).astype(F32)
        d_shift, d_bounded = _score_shift(DIFF_Q_SCALE * _norm_bound(DIFF_HEAD_DIM, diff_qn_g[layer]),
                                          _norm_bound(DIFF_HEAD_DIM, diff_kn_g[layer]))
        scal = jnp.concatenate([slopes, d_shift.reshape(1), d_bounded.astype(F32).reshape(1)])
        o_b = _diff_attn(scal, qn.reshape(batch, seq, -1), kn.reshape(batch, seq, -1),
                         p.reshape(batch, seq, -1), pos_q, pos_k, lam4, _row(diff_subln_g[layer]),
                         lambda_init=lambda_init).reshape(t, -1)

        j = layer // 2
        wa, wb, wo = (w.astype(BF16) for w in (w_branch_a[layer], w_branch_b[layer], w_out[layer]))
        if layer % 2 == 0:
            xf, h = _merge(xf, o_a, o_b, p, wa, wb, wo, _row(ffn_norm_g[layer]))
            xf = _ffn(xf, h, dense_w_gate[j].astype(BF16), dense_w_up[j].astype(BF16),
                      dense_w_down[j].astype(BF16))
        else:
            wr = jnp.pad(moe_w_router[j].astype(F32), ((0, 0), (0, LANES - N_EXPERTS)))
            wr_hi = wr.astype(BF16)
            wr_lo = (wr - wr_hi.astype(F32)).astype(BF16)
            xf, h, route = _merge(xf, o_a, o_b, p, wa, wb, wo, _row(ffn_norm_g[layer]),
                                  router=(wr_hi, wr_lo))
            tok, tile_e, valid, pos = _route_plan(route, MOE_TILE_ROWS)
            y_rows = _moe_gmm(tok, tile_e, valid, h, moe_w_gate[j].astype(BF16),
                              moe_w_up[j].astype(BF16), moe_w_down[j].astype(BF16))
            xf = _moe_combine(xf, route, y_rows, pos)
    return xf.reshape(batch, seq, d)
```

```python
import functools
import math

import jax
import jax.numpy as jnp
from jax import lax
from jax.experimental import pallas as pl
from jax.experimental.pallas import tpu as pltpu

D_MODEL = 1024
DEPTH = 2
MLA_HEADS = 8
MLA_Q_LORA = 512
MLA_KV_LORA = 256
MLA_NOPE = 128
MLA_ROPE = 64
MLA_V = 128
ROPE_THETA = 10000.0
DIFF_HEADS = 8
DIFF_HEAD_DIM = 64
D_FF_DENSE = 2816
N_EXPERTS = 8
TOP_K = 2
D_FF_EXPERT = 3584
EPS = 1e-6

LANES = 128
LOG2E = 1.4426950408889634
MLA_Q_SCALE = (MLA_NOPE + MLA_ROPE) ** -0.5 * LOG2E
DIFF_Q_SCALE = DIFF_HEAD_DIM ** -0.5 * LOG2E
NEG_BIG = -1e30
BF16 = jnp.bfloat16
F32 = jnp.float32

P_WIDTH = 6144
P_DQ_BLK = 1
P_DK_BLK = 2
P_DV_BLK = 3
P_GATE_COL0 = 4096
MOE_TILE_ROWS = 1024
VMEM_LIMIT = 52 * 1024 * 1024


def _cparams(sem):
    return pltpu.CompilerParams(dimension_semantics=sem, vmem_limit_bytes=VMEM_LIMIT)


def _rms(x, gain):
    ms = jnp.mean(x * x, axis=-1, keepdims=True)
    return x * lax.rsqrt(ms + EPS) * gain


def _ms_half_lanes(xb):
    lane = lax.broadcasted_iota(jnp.int32, xb.shape, 1)
    lo = lane < 64
    sq = xb * xb
    s_lo = jnp.sum(jnp.where(lo, sq, 0.0), axis=-1, keepdims=True)
    s_hi = jnp.sum(jnp.where(lo, 0.0, sq), axis=-1, keepdims=True)
    return jnp.where(lo, s_lo, s_hi) * (1.0 / 64.0)


def _norm_proj_kernel(x_ref, g_ref, w_ref, o_ref, h_ref, *, gate_tile0):
    j = pl.program_id(1)

    @pl.when(j == 0)
    def _():
        h_ref[...] = _rms(x_ref[...], g_ref[...]).astype(BF16)

    y = jnp.dot(h_ref[...], w_ref[...], preferred_element_type=F32)

    @pl.when(j < gate_tile0)
    def _():
        o_ref[...] = y.astype(BF16)

    @pl.when(j >= gate_tile0)
    def _():
        o_ref[...] = jax.nn.sigmoid(y).astype(BF16)


def _norm_proj(x, gain, w, *, tm=1024, tn=2048):
    t, d = x.shape
    n = w.shape[1]
    return pl.pallas_call(
        functools.partial(_norm_proj_kernel, gate_tile0=P_GATE_COL0 // tn),
        grid=(t // tm, n // tn),
        in_specs=[
            pl.BlockSpec((tm, d), lambda i, j: (i, 0)),
            pl.BlockSpec((1, d), lambda i, j: (0, 0)),
            pl.BlockSpec((d, tn), lambda i, j: (0, j)),
        ],
        out_specs=pl.BlockSpec((tm, tn), lambda i, j: (i, j)),
        out_shape=jax.ShapeDtypeStruct((t, n), BF16),
        scratch_shapes=[pltpu.VMEM((tm, d), BF16)],
        compiler_params=_cparams(("parallel", "arbitrary")),
        name="norm_proj",
    )(x, gain, w)


def _rope_table_kernel(pos_ref, freq_ref, cos_ref, sin_ref):
    ang = pos_ref[...] * freq_ref[...]
    lane = lax.broadcasted_iota(jnp.int32, ang.shape, 1)
    first_half = (lane % 64) < 32
    cos_ref[...] = jnp.cos(ang)
    sin_ref[...] = jnp.where(first_half, -1.0, 1.0) * jnp.sin(ang)


def _rope_tables(pos_col, *, tm=2048):
    t = pos_col.shape[0]
    half = MLA_ROPE // 2
    inv_freq = ROPE_THETA ** (-jnp.arange(half, dtype=F32) / half)
    freq = jnp.tile(inv_freq, LANES // half).reshape(1, LANES)
    return pl.pallas_call(
        _rope_table_kernel,
        grid=(t // tm,),
        in_specs=[pl.BlockSpec((tm, 1), lambda i: (i, 0)),
                  pl.BlockSpec((1, LANES), lambda i: (0, 0))],
        out_specs=[pl.BlockSpec((tm, LANES), lambda i: (i, 0))] * 2,
        out_shape=[jax.ShapeDtypeStruct((t, LANES), F32)] * 2,
        compiler_params=_cparams(("parallel",)),
        name="rope_tables",
    )(pos_col, freq)


def _rope(n, cos, sin_signed):
    w = n.shape[-1]
    lane = lax.broadcasted_iota(jnp.int32, n.shape, 1)
    first_half = (lane % 64) < 32
    swapped = jnp.where(first_half, pltpu.roll(n, w - 32, 1), pltpu.roll(n, 32, 1))
    return n * cos + swapped * sin_signed


def _mla_prep_kernel(p_ref, cos_ref, sin_ref, gq_ref, gkv_ref, wuq_ref, wukv_ref,
                     gqn_ref, gqr_ref, gkn_ref, gkr_ref, shift_ref, q_ref, k_ref, v_ref, *, q_scale):
    cq = p_ref[:, 0:MLA_Q_LORA].astype(F32)
    ckv = p_ref[:, MLA_Q_LORA:MLA_Q_LORA + MLA_KV_LORA].astype(F32)
    kr = p_ref[:, 768:896].astype(F32)
    cos = cos_ref[...]
    sin = sin_ref[...]

    q = jnp.dot(_rms(cq, gq_ref[...]).astype(BF16), wuq_ref[...], preferred_element_type=F32)
    kv = jnp.dot(_rms(ckv, gkv_ref[...]).astype(BF16), wukv_ref[...], preferred_element_type=F32)

    n_nope = MLA_HEADS * MLA_NOPE
    kr_ms = jnp.sum(kr * kr, axis=-1, keepdims=True) * (1.0 / MLA_ROPE)
    kr_n = kr * lax.rsqrt(kr_ms + EPS) * gkr_ref[...]
    kr_r = _rope(kr_n, cos, sin)
    kr_dup = kr_r + pltpu.roll(kr_r, 64, 1)

    lane = lax.broadcasted_iota(jnp.int32, (q.shape[0], LANES), 1)
    one_hot = jnp.where(lane % 64 == 0, 1.0, 0.0)
    for pair in range(MLA_HEADS // 2):
        qr = q[:, n_nope + pair * LANES:n_nope + (pair + 1) * LANES]
        qr_n = qr * lax.rsqrt(_ms_half_lanes(qr) + EPS) * gqr_ref[...]
        qr_r = _rope(qr_n, cos, sin) * q_scale
        for sub in range(2):
            h = 2 * pair + sub
            keep = (lane < 64) if sub == 0 else (lane >= 64)
            qn = _rms(q[:, h * MLA_NOPE:(h + 1) * MLA_NOPE], gqn_ref[...]) * q_scale
            q_ref[0, h, :, 0:LANES] = qn.astype(BF16)
            q_ref[0, h, :, LANES:2 * LANES] = jnp.where(keep, qr_r, shift_ref[...]).astype(BF16)
            kn = _rms(kv[:, h * MLA_NOPE:(h + 1) * MLA_NOPE], gkn_ref[...])
            k_ref[0, h, :, 0:LANES] = kn.astype(BF16)
            k_ref[0, h, :, LANES:2 * LANES] = jnp.where(keep, kr_dup, one_hot).astype(BF16)
            v_ref[0, h, :, :] = kv[:, n_nope + h * MLA_V:n_nope + (h + 1) * MLA_V].T.astype(BF16)


def _mla_prep(p, cos, sin, gq, gkv, wuq, wukv, gqn, gqr, gkn, gkr, shift_row, *, batch, seq, tm=512):
    nb = seq // tm
    full = lambda a: pl.BlockSpec(a.shape, lambda b, i: (0,) * a.ndim)
    row = lambda b, i: (b * nb + i, 0)
    hshape = lambda w: jax.ShapeDtypeStruct((batch, MLA_HEADS, seq, w), BF16)
    hspec = lambda w: pl.BlockSpec((1, MLA_HEADS, tm, w), lambda b, i: (b, 0, i, 0))
    return pl.pallas_call(
        functools.partial(_mla_prep_kernel, q_scale=MLA_Q_SCALE),
        grid=(batch, nb),
        in_specs=[pl.BlockSpec((tm, 1024), row),
                  pl.BlockSpec((tm, LANES), row), pl.BlockSpec((tm, LANES), row),
                  full(gq), full(gkv), full(wuq), full(wukv),
                  full(gqn), full(gqr), full(gkn), full(gkr), full(shift_row)],
        out_specs=[hspec(2 * LANES), hspec(2 * LANES),
                   pl.BlockSpec((1, MLA_HEADS, MLA_V, tm), lambda b, i: (b, 0, 0, i))],
        out_shape=[hshape(2 * LANES), hshape(2 * LANES),
                   jax.ShapeDtypeStruct((batch, MLA_HEADS, MLA_V, seq), BF16)],
        compiler_params=_cparams(("parallel", "parallel")),
        name="mla_prep",
    )(p, cos, sin, gq, gkv, wuq, wukv, gqn, gqr, gkn, gkr, shift_row)


BOUND_MARGIN = 1.02
MAX_SHIFT_SPAN = 100.0


def _norm_bound(n, gain):
    return math.sqrt(n) * jnp.max(jnp.abs(gain.astype(F32)))


def _score_shift(q_bound, k_bound):
    shift = (BOUND_MARGIN * q_bound * k_bound).astype(BF16).astype(F32)
    return shift, (2.0 * shift <= MAX_SHIFT_SPAN)


def _nt_dot(a, b):
    return lax.dot_general(a, b, (((1,), (1,)), ((), ())), preferred_element_type=F32)


def _lane_partial_sum(p):
    acc = p[:, 0:LANES]
    for c in range(1, p.shape[1] // LANES):
        acc = acc + p[:, c * LANES:(c + 1) * LANES]
    return acc


def _online_update(s, m, l):
    m_new = jnp.maximum(m, jnp.max(s, axis=-1, keepdims=True))
    alpha = jnp.exp2(m - m_new)
    p = jnp.exp2(s - m_new)
    l_new = alpha * l + jnp.sum(p, axis=-1, keepdims=True)
    return p, alpha, m_new, l_new


def _sublane_partial_sum(p):
    return jnp.sum(p.reshape(p.shape[0] // 8, 8, p.shape[1]), axis=0)


def _online_update_t(s, m, l):
    m_new = jnp.maximum(m, jnp.max(s, axis=0, keepdims=True))
    alpha = jnp.exp2(m - m_new)
    p = jnp.exp2(s - m_new)
    return p, alpha, m_new, alpha * l + jnp.sum(p, axis=0, keepdims=True)


def _mla_attn_kernel(flag_ref, q_ref, k_ref, vt_ref, o_ref, *, tk, nk, unroll):
    q = q_ref[0, 0]
    tq = q.shape[0]

    def chunk(j):
        off = pl.multiple_of(j * tk, tk)
        return k_ref[0, 0, pl.ds(off, tk), :], vt_ref[0, 0, :, pl.ds(off, tk)]

    def bounded():
        def body(j, carry):
            l, acc = carry
            k, vt = chunk(j)
            p = jnp.exp2(_nt_dot(k, q))
            return (l + _sublane_partial_sum(p),
                    acc + jnp.dot(vt, p.astype(BF16), preferred_element_type=F32))

        init = (jnp.zeros((8, tq), F32), jnp.zeros((MLA_V, tq), F32))
        l, acc = lax.fori_loop(0, nk, body, init, unroll=unroll)
        return acc / jnp.sum(l, axis=0, keepdims=True)

    def online():
        def body(j, carry):
            m, l, acc = carry
            k, vt = chunk(j)
            p, alpha, m, l = _online_update_t(_nt_dot(k, q), m, l)
            return m, l, alpha * acc + jnp.dot(vt, p.astype(BF16), preferred_element_type=F32)

        init = (jnp.full((1, tq), NEG_BIG, F32), jnp.zeros((1, tq), F32), jnp.zeros((MLA_V, tq), F32))
        _, l, acc = lax.fori_loop(0, nk, body, init)
        return acc / l

    o_ref[0] = lax.cond(flag_ref[0] != 0, bounded, online).T.astype(BF16)


def _mla_attn(flag, q, k, v, *, tq=1024, tk=512, unroll=4):
    b, h, s, w = q.shape
    return pl.pallas_call(
        functools.partial(_mla_attn_kernel, tk=tk, nk=s // tk, unroll=unroll),
        grid=(b, h, s // tq),
        in_specs=[pl.BlockSpec(memory_space=pltpu.SMEM),
                  pl.BlockSpec((1, 1, tq, w), lambda bi, hi, qi: (bi, hi, qi, 0)),
                  pl.BlockSpec((1, 1, s, w), lambda bi, hi, qi: (bi, hi, 0, 0)),
                  pl.BlockSpec((1, 1, MLA_V, s), lambda bi, hi, qi: (bi, hi, 0, 0))],
        out_specs=pl.BlockSpec((1, tq, MLA_V), lambda bi, hi, qi: (bi, qi, hi)),
        out_shape=jax.ShapeDtypeStruct((b, s, h * MLA_V), BF16),
        compiler_params=_cparams(("parallel", "parallel", "arbitrary")),
        name="mla_attn",
    )(flag, q, k, v)


def _diff_prep_kernel(q_ref, k_ref, v_ref, gq_ref, gk_ref, qn_ref, kn_ref, vt_ref, *, q_scale):
    for c in range(DIFF_HEADS):
        sl = slice(c * LANES, (c + 1) * LANES)
        q = q_ref[:, sl].astype(F32)
        k = k_ref[:, sl].astype(F32)
        qn_ref[:, sl] = (q * lax.rsqrt(_ms_half_lanes(q) + EPS) * (gq_ref[...] * q_scale)).astype(BF16)
        kn_ref[:, sl] = (k * lax.rsqrt(_ms_half_lanes(k) + EPS) * gk_ref[...]).astype(BF16)
        vt_ref[0, c] = v_ref[:, sl].astype(F32).T.astype(BF16)


def _diff_prep(p, gq, gk, *, batch, seq, tm=1024):
    t = p.shape[0]
    nb = seq // tm
    w = DIFF_HEADS * 2 * DIFF_HEAD_DIM
    row = lambda blk: (lambda b, i: (b * nb + i, blk))
    gain = pl.BlockSpec((1, LANES), lambda b, i: (0, 0))
    return pl.pallas_call(
        functools.partial(_diff_prep_kernel, q_scale=DIFF_Q_SCALE),
        grid=(batch, nb),
        in_specs=[pl.BlockSpec((tm, w), row(P_DQ_BLK)), pl.BlockSpec((tm, w), row(P_DK_BLK)),
                  pl.BlockSpec((tm, w), row(P_DV_BLK)), gain, gain],
        out_specs=[pl.BlockSpec((tm, w), row(0)), pl.BlockSpec((tm, w), row(0)),
                   pl.BlockSpec((1, DIFF_HEADS, LANES, tm), lambda b, i: (b, 0, 0, i))],
        out_shape=[jax.ShapeDtypeStruct((t, w), BF16), jax.ShapeDtypeStruct((t, w), BF16),
                   jax.ShapeDtypeStruct((batch, DIFF_HEADS, LANES, seq), BF16)],
        compiler_params=_cparams(("parallel", "parallel")),
        name="diff_prep",
    )(p, p, p, gq, gk)


def _diff_attn_kernel(scal_ref, q_ref, k_ref, vt_ref, pq_ref, pk_ref, lam_ref, g_ref, o_ref,
                      *, tk, nk, unroll, lambda_init):
    q = q_ref[0]
    tq = q.shape[0]
    lane = lax.broadcasted_iota(jnp.int32, q.shape, 1)
    zero = jnp.zeros_like(q)
    q0 = jnp.where(lane < DIFF_HEAD_DIM, q, zero)
    q1 = jnp.where(lane < DIFF_HEAD_DIM, zero, q)
    pq = pq_ref[0]
    slope = scal_ref[pl.program_id(1)] * LOG2E
    shift = scal_ref[DIFF_HEADS]
    use_bounded = scal_ref[DIFF_HEADS + 1]

    def chunk(j):
        off = pl.multiple_of(j * tk, tk)
        k = k_ref[0, pl.ds(off, tk), :]
        vt = vt_ref[0, 0, :, pl.ds(off, tk)]
        bias = slope * jnp.abs(pk_ref[0, pl.ds(off, tk), :] - pq) + shift
        return k, vt, bias

    def pv(vt, p):
        return jnp.dot(vt, p.astype(BF16), preferred_element_type=F32)

    def bounded():
        def body(j, carry):
            l0, a0, l1, a1 = carry
            k, vt, bias = chunk(j)
            p0 = jnp.exp2(_nt_dot(k, q0) - bias)
            p1 = jnp.exp2(_nt_dot(k, q1) - bias)
            return (l0 + _sublane_partial_sum(p0), a0 + pv(vt, p0),
                    l1 + _sublane_partial_sum(p1), a1 + pv(vt, p1))

        zl = jnp.zeros((8, tq), F32)
        za = jnp.zeros((LANES, tq), F32)
        l0, a0, l1, a1 = lax.fori_loop(0, nk, body, (zl, za, zl, za), unroll=unroll)
        return (a0 / jnp.sum(l0, axis=0, keepdims=True), a1 / jnp.sum(l1, axis=0, keepdims=True))

    def online():
        def body(j, carry):
            m0, l0, a0, m1, l1, a1 = carry
            k, vt, bias = chunk(j)
            p0, al0, m0, l0 = _online_update_t(_nt_dot(k, q0) - bias, m0, l0)
            a0 = al0 * a0 + pv(vt, p0)
            p1, al1, m1, l1 = _online_update_t(_nt_dot(k, q1) - bias, m1, l1)
            a1 = al1 * a1 + pv(vt, p1)
            return m0, l0, a0, m1, l1, a1

        neg = jnp.full((1, tq), NEG_BIG, F32)
        z1 = jnp.zeros((1, tq), F32)
        za = jnp.zeros((LANES, tq), F32)
        _, l0, a0, _, l1, a1 = lax.fori_loop(0, nk, body, (neg, z1, za, neg, z1, za))
        return a0 / l0, a1 / l1

    o0, o1 = lax.cond(use_bounded != 0.0, bounded, online)
    lam = (jnp.exp(jnp.sum(lam_ref[0:1, :] * lam_ref[1:2, :], axis=-1, keepdims=True))
           - jnp.exp(jnp.sum(lam_ref[2:3, :] * lam_ref[3:4, :], axis=-1, keepdims=True))
           + lambda_init)
    o = (o0 - lam * o1).T
    o_ref[0] = (_rms(o, g_ref[...]) * (1.0 - lambda_init)).astype(BF16)


def _diff_attn(scal, qn, kn, vt, pos_row, pos_col, lam4, subln_g, *, lambda_init, tq=1024, tk=512, unroll=4):
    b, s, _ = qn.shape
    return pl.pallas_call(
        functools.partial(_diff_attn_kernel, tk=tk, nk=s // tk, unroll=unroll, lambda_init=lambda_init),
        grid=(b, DIFF_HEADS, s // tq),
        in_specs=[pl.BlockSpec(memory_space=pltpu.SMEM),
                  pl.BlockSpec((1, tq, LANES), lambda bi, hi, qi: (bi, qi, hi)),
                  pl.BlockSpec((1, s, LANES), lambda bi, hi, qi: (bi, 0, hi)),
                  pl.BlockSpec((1, 1, LANES, s), lambda bi, hi, qi: (bi, hi, 0, 0)),
                  pl.BlockSpec((1, 1, tq), lambda bi, hi, qi: (bi, 0, qi)),
                  pl.BlockSpec((1, s, 1), lambda bi, hi, qi: (bi, 0, 0)),
                  pl.BlockSpec((4, DIFF_HEAD_DIM), lambda bi, hi, qi: (0, 0)),
                  pl.BlockSpec((1, LANES), lambda bi, hi, qi: (0, 0))],
        out_specs=pl.BlockSpec((1, tq, LANES), lambda bi, hi, qi: (bi, qi, hi)),
        out_shape=jax.ShapeDtypeStruct((b, s, DIFF_HEADS * LANES), BF16),
        compiler_params=_cparams(("parallel", "parallel", "arbitrary")),
        name="diff_attn",
    )(scal, qn, kn, vt, pos_row, pos_col, lam4, subln_g)


def _split_dot_f32(a, w_hi, w_lo):
    a_hi = a.astype(BF16)
    a_lo = (a - a_hi.astype(F32)).astype(BF16)
    return (jnp.dot(a_hi, w_hi, preferred_element_type=F32)
            + jnp.dot(a_lo, w_hi, preferred_element_type=F32)
            + jnp.dot(a_hi, w_lo, preferred_element_type=F32))


def _merge_kernel(x_ref, oa_ref, ob_ref, ga_ref, gb_ref, wa_ref, wb_ref, wo_ref, gf_ref, *rest,
                  with_router):
    if with_router:
        wr_hi_ref, wr_lo_ref, xo_ref, h_ref, route_ref = rest
    else:
        xo_ref, h_ref = rest
    ya = jnp.dot(oa_ref[...], wa_ref[...], preferred_element_type=F32)
    yb = jnp.dot(ob_ref[...], wb_ref[...], preferred_element_type=F32)
    merged = ga_ref[...].astype(F32) * ya + gb_ref[...].astype(F32) * yb
    x = x_ref[...] + jnp.dot(merged.astype(BF16), wo_ref[...], preferred_element_type=F32)
    xo_ref[...] = x
    h = _rms(x, gf_ref[...])
    h_ref[...] = h.astype(h_ref.dtype)
    if with_router:
        route_ref[...] = _top2_route(_split_dot_f32(h, wr_hi_ref[...], wr_lo_ref[...]))


def _merge(x, oa, ob, p, wa, wb, wo, gf, router=None, *, tm=512):
    t, d = x.shape
    ga_blk = P_GATE_COL0 // d
    row = lambda i: (i, 0)
    full = lambda a: pl.BlockSpec(a.shape, lambda i: (0,) * a.ndim)
    in_specs = [pl.BlockSpec((tm, d), row), pl.BlockSpec((tm, d), row), pl.BlockSpec((tm, d), row),
                pl.BlockSpec((tm, d), lambda i: (i, ga_blk)),
                pl.BlockSpec((tm, d), lambda i: (i, ga_blk + 1)),
                full(wa), full(wb), full(wo), full(gf)]
    args = [x, oa, ob, p, p, wa, wb, wo, gf]
    out_specs = [pl.BlockSpec((tm, d), row), pl.BlockSpec((tm, d), row)]
    h_dtype = BF16 if router is None else F32
    out_shape = [jax.ShapeDtypeStruct((t, d), F32), jax.ShapeDtypeStruct((t, d), h_dtype)]
    if router is not None:
        in_specs += [full(router[0]), full(router[1])]
        args += list(router)
        out_specs.append(pl.BlockSpec((tm, LANES), row))
        out_shape.append(jax.ShapeDtypeStruct((t, LANES), F32))
    return pl.pallas_call(
        functools.partial(_merge_kernel, with_router=router is not None),
        grid=(t // tm,),
        in_specs=in_specs,
        out_specs=out_specs,
        out_shape=out_shape,
        compiler_params=_cparams(("parallel",)),
        name="merge_router" if router is not None else "merge",
    )(*args)


def _swiglu_act(h, wg, wu):
    g = jnp.dot(h, wg, preferred_element_type=F32)
    u = jnp.dot(h, wu, preferred_element_type=F32)
    return (g * jax.nn.sigmoid(g) * u).astype(BF16)


def _ffn_kernel(x_ref, h_ref, wg_ref, wu_ref, wd_ref, o_ref, acc_ref):
    f = pl.program_id(1)
    y = jnp.dot(_swiglu_act(h_ref[...], wg_ref[...], wu_ref[...]), wd_ref[...],
                preferred_element_type=F32)

    @pl.when(f == 0)
    def _():
        acc_ref[...] = x_ref[...] + y

    @pl.when(f > 0)
    def _():
        acc_ref[...] += y

    @pl.when(f == pl.num_programs(1) - 1)
    def _():
        o_ref[...] = acc_ref[...]


def _ffn(x, h, wg, wu, wd, *, tm=1024, tf=1408):
    t, d = x.shape
    ff = wg.shape[1]
    return pl.pallas_call(
        _ffn_kernel,
        grid=(t // tm, ff // tf),
        in_specs=[pl.BlockSpec((tm, d), lambda i, f: (i, 0)),
                  pl.BlockSpec((tm, d), lambda i, f: (i, 0)),
                  pl.BlockSpec((d, tf), lambda i, f: (0, f)),
                  pl.BlockSpec((d, tf), lambda i, f: (0, f)),
                  pl.BlockSpec((tf, d), lambda i, f: (f, 0))],
        out_specs=pl.BlockSpec((tm, d), lambda i, f: (i, 0)),
        out_shape=jax.ShapeDtypeStruct((t, d), F32),
        scratch_shapes=[pltpu.VMEM((tm, d), F32)],
        compiler_params=_cparams(("parallel", "arbitrary")),
        name="ffn_dense",
    )(x, h, wg, wu, wd)


def _top2_route(logits):
    lane_i = lax.broadcasted_iota(jnp.int32, logits.shape, 1)
    lane = lane_i.astype(F32)
    neg = jnp.float32(-jnp.inf)
    z = jnp.where(lane_i < N_EXPERTS, logits, neg)
    m1 = jnp.max(z, axis=-1, keepdims=True)
    i1 = jnp.min(jnp.where(z == m1, lane, float(LANES)), axis=-1, keepdims=True)
    z2 = jnp.where(lane == i1, neg, z)
    m2 = jnp.max(z2, axis=-1, keepdims=True)
    i2 = jnp.min(jnp.where(z2 == m2, lane, float(LANES)), axis=-1, keepdims=True)
    e2 = jnp.exp(m2 - m1)
    w1 = 1.0 / (1.0 + e2)
    w2 = e2 / (1.0 + e2)
    return jnp.where(lane_i == 0, i1, jnp.where(lane_i == 1, i2,
                     jnp.where(lane_i == 2, w1, jnp.where(lane_i == 3, w2, 0.0))))


def _route_plan(route, tm):
    t = route.shape[0]
    e = route[:, 0:TOP_K].astype(jnp.int32)
    sel = (e[:, :, None] == jnp.arange(N_EXPERTS, dtype=jnp.int32)).astype(jnp.int32).sum(axis=1)
    csum = jnp.cumsum(sel, axis=0)
    padded = (csum[-1] + tm - 1) // tm * tm
    end = jnp.cumsum(padded)
    pos = (end - padded)[e] + jnp.take_along_axis(csum - sel, e, axis=1)
    n_rows = TOP_K * t + N_EXPERTS * tm
    flat = pos.reshape(-1)
    tok = jnp.zeros((n_rows,), jnp.int32).at[flat].set(
        jnp.repeat(jnp.arange(t, dtype=jnp.int32), TOP_K), unique_indices=True)
    n_tiles = n_rows // tm
    row0 = jnp.arange(n_tiles, dtype=jnp.int32) * tm
    valid = (row0 < end[-1]).astype(jnp.int32)
    tile_e = jnp.searchsorted(end, jnp.minimum(row0, end[-1] - 1), side="right").astype(jnp.int32)
    return tok.reshape(n_tiles, 1, tm), tile_e, valid, pos


def _moe_gmm_kernel(tile_e_ref, valid_ref, tok_ref, tok_next_ref, h_hbm, wg_ref, wu_ref, wd_ref,
                    o_ref, xg_ref, xb_ref, acc_ref, sem, *, nf, gather_steps):
    del tile_e_ref
    i = pl.program_id(0)
    f = pl.program_id(1)
    tm = xb_ref.shape[0]
    slot = i % 2

    def row_copy(tok, r, s):
        return pltpu.make_async_copy(h_hbm.at[pl.ds(tok[0, 0, r], 1), :],
                                     xg_ref.at[s, pl.ds(r, 1), :], sem.at[s])

    @pl.when((i == 0) & (f == 0))
    def _():
        def body(r, c):
            row_copy(tok_ref, r, 0).start()
            return c
        lax.fori_loop(0, tm, body, 0, unroll=8)

    valid = valid_ref[i] != 0
    prev_valid = valid_ref[jnp.maximum(i - 1, 0)] != 0

    @pl.when((f == 0) & ((i == 0) | prev_valid))
    def _():
        pltpu.make_async_copy(h_hbm.at[pl.ds(0, tm), :], xg_ref.at[slot], sem.at[slot]).wait()
        xb_ref[...] = xg_ref[slot].astype(BF16)

    def step(request_rows):
        if request_rows:
            rows = tm // gather_steps
            for r in range(rows):
                row_copy(tok_next_ref, f * rows + r, 1 - slot).start()
        y = jnp.dot(_swiglu_act(xb_ref[...], wg_ref[0], wu_ref[0]), wd_ref[0],
                    preferred_element_type=F32)

        @pl.when(f == 0)
        def _():
            acc_ref[...] = y

        @pl.when((f > 0) & (f < nf - 1))
        def _():
            acc_ref[...] += y

        @pl.when(f == nf - 1)
        def _():
            o_ref[...] = acc_ref[...] + y

    pl.when(valid & (f < gather_steps))(functools.partial(step, True))
    pl.when(valid & (f >= gather_steps))(functools.partial(step, False))

    @pl.when(jnp.logical_not(valid) & (f == nf - 1))
    def _():
        o_ref[...] = jnp.zeros_like(o_ref)


def _moe_gmm(tok, tile_e, valid, h, wg, wu, wd, *, tf=896, gather_steps=4):
    n_tiles, _, tm = tok.shape
    d = h.shape[1]
    ff = wg.shape[2]
    nf = ff // tf
    assert 2 <= gather_steps <= nf and tm % gather_steps == 0
    tok_spec = lambda imap: pl.BlockSpec((1, 1, tm), imap, memory_space=pltpu.SMEM)
    grid_spec = pltpu.PrefetchScalarGridSpec(
        num_scalar_prefetch=2,
        grid=(n_tiles, nf),
        in_specs=[tok_spec(lambda i, f, te, va: (i, 0, 0)),
                  tok_spec(lambda i, f, te, va: (jnp.minimum(i + 1, n_tiles - 1), 0, 0)),
                  pl.BlockSpec(memory_space=pl.ANY),
                  pl.BlockSpec((1, d, tf), lambda i, f, te, va: (te[i], 0, f)),
                  pl.BlockSpec((1, d, tf), lambda i, f, te, va: (te[i], 0, f)),
                  pl.BlockSpec((1, tf, d), lambda i, f, te, va: (te[i], f, 0))],
        out_specs=pl.BlockSpec((tm, d), lambda i, f, te, va: (i, 0)),
        scratch_shapes=[pltpu.VMEM((2, tm, d), F32), pltpu.VMEM((tm, d), BF16), pltpu.VMEM((tm, d), F32),
                        pltpu.SemaphoreType.DMA((2,))],
    )
    return pl.pallas_call(
        functools.partial(_moe_gmm_kernel, nf=nf, gather_steps=gather_steps),
        grid_spec=grid_spec,
        out_shape=jax.ShapeDtypeStruct((n_tiles * tm, d), F32),
        compiler_params=_cparams(("arbitrary", "arbitrary")),
        name="moe_gmm",
    )(tile_e, valid, tok, tok, h, wg, wu, wd)


def _moe_combine_kernel(pos_ref, pos_next_ref, x_ref, route_ref, y_hbm, o_ref, buf_ref, sem):
    i = pl.program_id(0)
    tc = x_ref.shape[0]
    slot = i % 2

    def gather_rows(pos, s):
        def body(r, c):
            for k in range(TOP_K):
                pltpu.make_async_copy(y_hbm.at[pl.ds(pos[0, 0, k * tc + r], 1), :],
                                      buf_ref.at[s, pl.ds(k * tc + r, 1), :], sem.at[s]).start()
            return c
        lax.fori_loop(0, tc, body, 0, unroll=4)

    @pl.when(i == 0)
    def _():
        gather_rows(pos_ref, 0)

    @pl.when(i + 1 < pl.num_programs(0))
    def _():
        gather_rows(pos_next_ref, 1 - slot)

    pltpu.make_async_copy(y_hbm.at[pl.ds(0, TOP_K * tc), :], buf_ref.at[slot], sem.at[slot]).wait()
    y = buf_ref[slot]
    w = route_ref[...]
    o_ref[...] = (x_ref[...] + w[:, TOP_K:TOP_K + 1] * y[0:tc]
                  + w[:, TOP_K + 1:TOP_K + 2] * y[tc:2 * tc])


def _moe_combine(x, route, y_rows, pos, *, tc=512):
    t, d = x.shape
    n = t // tc
    pos_t = pos.reshape(n, tc, TOP_K).transpose(0, 2, 1).reshape(n, 1, TOP_K * tc)
    pos_spec = lambda imap: pl.BlockSpec((1, 1, TOP_K * tc), imap, memory_space=pltpu.SMEM)
    return pl.pallas_call(
        _moe_combine_kernel,
        grid=(n,),
        in_specs=[pos_spec(lambda i: (i, 0, 0)),
                  pos_spec(lambda i: (jnp.minimum(i + 1, n - 1), 0, 0)),
                  pl.BlockSpec((tc, d), lambda i: (i, 0)),
                  pl.BlockSpec((tc, LANES), lambda i: (i, 0)),
                  pl.BlockSpec(memory_space=pl.ANY)],
        out_specs=pl.BlockSpec((tc, d), lambda i: (i, 0)),
        out_shape=jax.ShapeDtypeStruct((t, d), F32),
        scratch_shapes=[pltpu.VMEM((2, TOP_K * tc, d), F32), pltpu.SemaphoreType.DMA((2,))],
        compiler_params=_cparams(("arbitrary",)),
        name="moe_combine",
    )(pos_t, pos_t, x, route, y_rows)


def _row(v, reps=1):
    return jnp.tile(v.astype(F32), reps).reshape(1, -1)


def kernel(x, positions, attn_norm_g, w_in, w_gate, mla_q_norm_g, mla_kv_norm_g, mla_w_uq, mla_w_ukv, mla_qn_nope_g, mla_qn_rope_g, mla_kn_nope_g, mla_kn_rope_g, diff_qn_g, diff_kn_g, diff_lam_q1, diff_lam_k1, diff_lam_q2, diff_lam_k2, diff_subln_g, w_branch_a, w_branch_b, w_out, ffn_norm_g, dense_w_gate, dense_w_up, dense_w_down, moe_w_router, moe_w_gate, moe_w_up, moe_w_down):
    batch, seq, d = x.shape
    t = batch * seq
    xf = x.reshape(t, d)
    pos_f = positions.astype(F32)
    cos, sin = _rope_tables(pos_f.reshape(t, 1))
    pos_col = pos_f.reshape(batch, seq, 1)
    pos_row = pos_f.reshape(batch, 1, seq)
    slopes = 2.0 ** (-8.0 * jnp.arange(1, DIFF_HEADS + 1, dtype=F32) / DIFF_HEADS)
    n_in_head = MLA_Q_LORA + MLA_KV_LORA + MLA_ROPE

    for layer in range(DEPTH):
        w1 = jnp.concatenate(
            [w_in[layer][:, :n_in_head], jnp.zeros((d, 1024 - n_in_head), F32),
             w_in[layer][:, n_in_head:], w_gate[layer]], axis=1).astype(BF16)
        wuq = mla_w_uq[layer].reshape(MLA_Q_LORA, MLA_HEADS, MLA_NOPE + MLA_ROPE)
        wuq = jnp.concatenate([wuq[:, :, :MLA_NOPE].reshape(MLA_Q_LORA, -1),
                               wuq[:, :, MLA_NOPE:].reshape(MLA_Q_LORA, -1)], axis=1).astype(BF16)
        wukv = mla_w_ukv[layer].reshape(MLA_KV_LORA, MLA_HEADS, MLA_NOPE + MLA_V)
        wukv = jnp.concatenate([wukv[:, :, :MLA_NOPE].reshape(MLA_KV_LORA, -1),
                                wukv[:, :, MLA_NOPE:].reshape(MLA_KV_LORA, -1)], axis=1).astype(BF16)
        gkr = jnp.concatenate([mla_kn_rope_g[layer].astype(F32), jnp.zeros((LANES - MLA_ROPE,), F32)]).reshape(1, -1)

        p = _norm_proj(xf, _row(attn_norm_g[layer]), w1)
        q_bound = MLA_Q_SCALE * jnp.sqrt(_norm_bound(MLA_NOPE, mla_qn_nope_g[layer]) ** 2
                                         + _norm_bound(MLA_ROPE, mla_qn_rope_g[layer]) ** 2)
        k_bound = jnp.sqrt(_norm_bound(MLA_NOPE, mla_kn_nope_g[layer]) ** 2
                           + _norm_bound(MLA_ROPE, mla_kn_rope_g[layer]) ** 2)
        shift, use_bounded = _score_shift(q_bound, k_bound)
        shift_row = jnp.where(jnp.arange(LANES) % 64 == 0, -shift, 0.0).astype(F32).reshape(1, LANES)
        q_m, k_m, v_m = _mla_prep(
            p, cos, sin, _row(mla_q_norm_g[layer]), _row(mla_kv_norm_g[layer]), wuq, wukv,
            _row(mla_qn_nope_g[layer]), _row(mla_qn_rope_g[layer], 2), _row(mla_kn_nope_g[layer]), gkr,
            shift_row, batch=batch, seq=seq)
        o_a = _mla_attn(use_bounded.astype(jnp.int32).reshape(1), q_m, k_m, v_m).reshape(t, -1)

        lambda_init = 0.8 - 0.6 * math.exp(-0.3 * layer)
        qn, kn, vt_d = _diff_prep(p, _row(diff_qn_g[layer], 2), _row(diff_kn_g[layer], 2),
                                  batch=batch, seq=seq)
        lam4 = jnp.stack([diff_lam_q1[layer], diff_lam_k1[layer],
                          diff_lam_q2[layer], diff_lam_k2[layer]]).astype(F32)
        d_shift, d_bounded = _score_shift(DIFF_Q_SCALE * _norm_bound(DIFF_HEAD_DIM, diff_qn_g[layer]),
                                          _norm_bound(DIFF_HEAD_DIM, diff_kn_g[layer]))
        scal = jnp.concatenate([slopes, d_shift.reshape(1), d_bounded.astype(F32).reshape(1)])
        o_b = _diff_attn(scal, qn.reshape(batch, seq, -1), kn.reshape(batch, seq, -1), vt_d,
                         pos_row, pos_col, lam4, _row(diff_subln_g[layer]),
                         lambda_init=lambda_init).reshape(t, -1)

        j = layer // 2
        wa, wb, wo = (w.astype(BF16) for w in (w_branch_a[layer], w_branch_b[layer], w_out[layer]))
        if layer % 2 == 0:
            xf, h = _merge(xf, o_a, o_b, p, wa, wb, wo, _row(ffn_norm_g[layer]))
            xf = _ffn(xf, h, dense_w_gate[j].astype(BF16), dense_w_up[j].astype(BF16),
                      dense_w_down[j].astype(BF16))
        else:
            wr = jnp.pad(moe_w_router[j].astype(F32), ((0, 0), (0, LANES - N_EXPERTS)))
            wr_hi = wr.astype(BF16)
            wr_lo = (wr - wr_hi.astype(F32)).astype(BF16)
            xf, h, route = _merge(xf, o_a, o_b, p, wa, wb, wo, _row(ffn_norm_g[layer]),
                                  router=(wr_hi, wr_lo))
            tok, tile_e, valid, pos = _route_plan(route, MOE_TILE_ROWS)
            y_rows = _moe_gmm(tok, tile_e, valid, h, moe_w_gate[j].astype(BF16),
                              moe_w_up[j].astype(BF16), moe_w_down[j].astype(BF16))
            xf = _moe_combine(xf, route, y_rows, pos)
    return xf.reshape(batch, seq, d)
```

```python
import functools
import math

import jax
import jax.numpy as jnp
from jax import lax
from jax.experimental import pallas as pl
from jax.experimental.pallas import tpu as pltpu

D_MODEL = 1024
DEPTH = 2
MLA_HEADS = 8
MLA_Q_LORA = 512
MLA_KV_LORA = 256
MLA_NOPE = 128
MLA_ROPE = 64
MLA_V = 128
ROPE_THETA = 10000.0
DIFF_HEADS = 8
DIFF_HEAD_DIM = 64
D_FF_DENSE = 2816
N_EXPERTS = 8
TOP_K = 2
D_FF_EXPERT = 3584
EPS = 1e-6

LANES = 128
LOG2E = 1.4426950408889634
MLA_Q_SCALE = (MLA_NOPE + MLA_ROPE) ** -0.5 * LOG2E
DIFF_Q_SCALE = DIFF_HEAD_DIM ** -0.5 * LOG2E
NEG_BIG = -1e30
BF16 = jnp.bfloat16
F32 = jnp.float32

P_WIDTH = 6144
P_DQ_BLK = 1
P_DK_BLK = 2
P_DV_BLK = 3
P_GATE_COL0 = 4096
MOE_TILE_ROWS = 1024
VMEM_LIMIT = 52 * 1024 * 1024


def _cparams(sem):
    return pltpu.CompilerParams(dimension_semantics=sem, vmem_limit_bytes=VMEM_LIMIT)


def _rms(x, gain):
    ms = jnp.mean(x * x, axis=-1, keepdims=True)
    return x * lax.rsqrt(ms + EPS) * gain


def _ms_half_lanes(xb):
    lane = lax.broadcasted_iota(jnp.int32, xb.shape, 1)
    lo = lane < 64
    sq = xb * xb
    s_lo = jnp.sum(jnp.where(lo, sq, 0.0), axis=-1, keepdims=True)
    s_hi = jnp.sum(jnp.where(lo, 0.0, sq), axis=-1, keepdims=True)
    return jnp.where(lo, s_lo, s_hi) * (1.0 / 64.0)


def _norm_proj_kernel(x_ref, g_ref, w_ref, o_ref, h_ref, *, gate_tile0):
    j = pl.program_id(1)

    @pl.when(j == 0)
    def _():
        h_ref[...] = _rms(x_ref[...], g_ref[...]).astype(BF16)

    y = jnp.dot(h_ref[...], w_ref[...], preferred_element_type=F32)

    @pl.when(j < gate_tile0)
    def _():
        o_ref[...] = y.astype(BF16)

    @pl.when(j >= gate_tile0)
    def _():
        o_ref[...] = jax.nn.sigmoid(y).astype(BF16)


def _norm_proj(x, gain, w, *, tm=1024, tn=2048):
    t, d = x.shape
    n = w.shape[1]
    return pl.pallas_call(
        functools.partial(_norm_proj_kernel, gate_tile0=P_GATE_COL0 // tn),
        grid=(t // tm, n // tn),
        in_specs=[
            pl.BlockSpec((tm, d), lambda i, j: (i, 0)),
            pl.BlockSpec((1, d), lambda i, j: (0, 0)),
            pl.BlockSpec((d, tn), lambda i, j: (0, j)),
        ],
        out_specs=pl.BlockSpec((tm, tn), lambda i, j: (i, j)),
        out_shape=jax.ShapeDtypeStruct((t, n), BF16),
        scratch_shapes=[pltpu.VMEM((tm, d), BF16)],
        compiler_params=_cparams(("parallel", "arbitrary")),
        name="norm_proj",
    )(x, gain, w)


def _rope_table_kernel(pos_ref, freq_ref, cos_ref, sin_ref):
    ang = pos_ref[...] * freq_ref[...]
    lane = lax.broadcasted_iota(jnp.int32, ang.shape, 1)
    first_half = (lane % 64) < 32
    cos_ref[...] = jnp.cos(ang)
    sin_ref[...] = jnp.where(first_half, -1.0, 1.0) * jnp.sin(ang)


def _rope_tables(pos_col, *, tm=2048):
    t = pos_col.shape[0]
    half = MLA_ROPE // 2
    inv_freq = ROPE_THETA ** (-jnp.arange(half, dtype=F32) / half)
    freq = jnp.tile(inv_freq, LANES // half).reshape(1, LANES)
    return pl.pallas_call(
        _rope_table_kernel,
        grid=(t // tm,),
        in_specs=[pl.BlockSpec((tm, 1), lambda i: (i, 0)),
                  pl.BlockSpec((1, LANES), lambda i: (0, 0))],
        out_specs=[pl.BlockSpec((tm, LANES), lambda i: (i, 0))] * 2,
        out_shape=[jax.ShapeDtypeStruct((t, LANES), F32)] * 2,
        compiler_params=_cparams(("parallel",)),
        name="rope_tables",
    )(pos_col, freq)


def _rope(n, cos, sin_signed):
    w = n.shape[-1]
    lane = lax.broadcasted_iota(jnp.int32, n.shape, 1)
    first_half = (lane % 64) < 32
    swapped = jnp.where(first_half, pltpu.roll(n, w - 32, 1), pltpu.roll(n, 32, 1))
    return n * cos + swapped * sin_signed


def _mla_prep_kernel(p_ref, cos_ref, sin_ref, gq_ref, gkv_ref, wuq_ref, wukv_ref,
                     gqn_ref, gqr_ref, gkn_ref, gkr_ref, shift_ref, q_ref, k_ref, v_ref, *, q_scale):
    cq = p_ref[:, 0:MLA_Q_LORA].astype(F32)
    ckv = p_ref[:, MLA_Q_LORA:MLA_Q_LORA + MLA_KV_LORA].astype(F32)
    kr = p_ref[:, 768:896].astype(F32)
    cos = cos_ref[...]
    sin = sin_ref[...]

    q = jnp.dot(_rms(cq, gq_ref[...]).astype(BF16), wuq_ref[...], preferred_element_type=F32)
    kv = jnp.dot(_rms(ckv, gkv_ref[...]).astype(BF16), wukv_ref[...], preferred_element_type=F32)

    n_nope = MLA_HEADS * MLA_NOPE
    kr_ms = jnp.sum(kr * kr, axis=-1, keepdims=True) * (1.0 / MLA_ROPE)
    kr_n = kr * lax.rsqrt(kr_ms + EPS) * gkr_ref[...]
    kr_r = _rope(kr_n, cos, sin)
    kr_dup = kr_r + pltpu.roll(kr_r, 64, 1)

    lane = lax.broadcasted_iota(jnp.int32, (q.shape[0], LANES), 1)
    one_hot = jnp.where(lane % 64 == 0, 1.0, 0.0)
    for pair in range(MLA_HEADS // 2):
        qr = q[:, n_nope + pair * LANES:n_nope + (pair + 1) * LANES]
        qr_n = qr * lax.rsqrt(_ms_half_lanes(qr) + EPS) * gqr_ref[...]
        qr_r = _rope(qr_n, cos, sin) * q_scale
        for sub in range(2):
            h = 2 * pair + sub
            keep = (lane < 64) if sub == 0 else (lane >= 64)
            qn = _rms(q[:, h * MLA_NOPE:(h + 1) * MLA_NOPE], gqn_ref[...]) * q_scale
            q_ref[0, h, :, 0:LANES] = qn.astype(BF16)
            q_ref[0, h, :, LANES:2 * LANES] = jnp.where(keep, qr_r, shift_ref[...]).astype(BF16)
            kn = _rms(kv[:, h * MLA_NOPE:(h + 1) * MLA_NOPE], gkn_ref[...])
            k_ref[0, h, :, 0:LANES] = kn.astype(BF16)
            k_ref[0, h, :, LANES:2 * LANES] = jnp.where(keep, kr_dup, one_hot).astype(BF16)
            v_ref[0, h, :, :] = kv[:, n_nope + h * MLA_V:n_nope + (h + 1) * MLA_V].T.astype(BF16)


def _mla_prep(p, cos, sin, gq, gkv, wuq, wukv, gqn, gqr, gkn, gkr, shift_row, *, batch, seq, tm=512):
    nb = seq // tm
    full = lambda a: pl.BlockSpec(a.shape, lambda b, i: (0,) * a.ndim)
    row = lambda b, i: (b * nb + i, 0)
    hshape = lambda w: jax.ShapeDtypeStruct((batch, MLA_HEADS, seq, w), BF16)
    hspec = lambda w: pl.BlockSpec((1, MLA_HEADS, tm, w), lambda b, i: (b, 0, i, 0))
    return pl.pallas_call(
        functools.partial(_mla_prep_kernel, q_scale=MLA_Q_SCALE),
        grid=(batch, nb),
        in_specs=[pl.BlockSpec((tm, 1024), row),
                  pl.BlockSpec((tm, LANES), row), pl.BlockSpec((tm, LANES), row),
                  full(gq), full(gkv), full(wuq), full(wukv),
                  full(gqn), full(gqr), full(gkn), full(gkr), full(shift_row)],
        out_specs=[hspec(2 * LANES), hspec(2 * LANES),
                   pl.BlockSpec((1, MLA_HEADS, MLA_V, tm), lambda b, i: (b, 0, 0, i))],
        out_shape=[hshape(2 * LANES), hshape(2 * LANES),
                   jax.ShapeDtypeStruct((batch, MLA_HEADS, MLA_V, seq), BF16)],
        compiler_params=_cparams(("parallel", "parallel")),
        name="mla_prep",
    )(p, cos, sin, gq, gkv, wuq, wukv, gqn, gqr, gkn, gkr, shift_row)


BOUND_MARGIN = 1.02
MAX_SHIFT_SPAN = 100.0


def _norm_bound(n, gain):
    return math.sqrt(n) * jnp.max(jnp.abs(gain.astype(F32)))


def _score_shift(q_bound, k_bound):
    shift = (BOUND_MARGIN * q_bound * k_bound).astype(BF16).astype(F32)
    return shift, (2.0 * shift <= MAX_SHIFT_SPAN)


def _nt_dot(a, b):
    return lax.dot_general(a, b, (((1,), (1,)), ((), ())), preferred_element_type=F32)


def _lane_partial_sum(p):
    acc = p[:, 0:LANES]
    for c in range(1, p.shape[1] // LANES):
        acc = acc + p[:, c * LANES:(c + 1) * LANES]
    return acc


def _online_update(s, m, l):
    m_new = jnp.maximum(m, jnp.max(s, axis=-1, keepdims=True))
    alpha = jnp.exp2(m - m_new)
    p = jnp.exp2(s - m_new)
    l_new = alpha * l + jnp.sum(p, axis=-1, keepdims=True)
    return p, alpha, m_new, l_new


def _sublane_partial_sum(p):
    return jnp.sum(p.reshape(p.shape[0] // 8, 8, p.shape[1]), axis=0)


def _online_update_t(s, m, l):
    m_new = jnp.maximum(m, jnp.max(s, axis=0, keepdims=True))
    alpha = jnp.exp2(m - m_new)
    p = jnp.exp2(s - m_new)
    return p, alpha, m_new, alpha * l + jnp.sum(p, axis=0, keepdims=True)


def _mla_attn_kernel(flag_ref, q_ref, k_ref, vt_ref, o_ref, *, tk, nk, unroll):
    q = q_ref[0, 0]
    tq = q.shape[0]

    def chunk(j):
        off = pl.multiple_of(j * tk, tk)
        return k_ref[0, 0, pl.ds(off, tk), :], vt_ref[0, 0, :, pl.ds(off, tk)]

    def bounded():
        def body(j, carry):
            l, acc = carry
            k, vt = chunk(j)
            p = jnp.exp2(_nt_dot(k, q))
            return (l + _sublane_partial_sum(p),
                    acc + jnp.dot(vt, p.astype(BF16), preferred_element_type=F32))

        init = (jnp.zeros((8, tq), F32), jnp.zeros((MLA_V, tq), F32))
        l, acc = lax.fori_loop(0, nk, body, init, unroll=unroll)
        return acc / jnp.sum(l, axis=0, keepdims=True)

    def online():
        def body(j, carry):
            m, l, acc = carry
            k, vt = chunk(j)
            p, alpha, m, l = _online_update_t(_nt_dot(k, q), m, l)
            return m, l, alpha * acc + jnp.dot(vt, p.astype(BF16), preferred_element_type=F32)

        init = (jnp.full((1, tq), NEG_BIG, F32), jnp.zeros((1, tq), F32), jnp.zeros((MLA_V, tq), F32))
        _, l, acc = lax.fori_loop(0, nk, body, init)
        return acc / l

    o_ref[0] = lax.cond(flag_ref[0] != 0, bounded, online).T.astype(BF16)


def _mla_attn(flag, q, k, v, *, tq=1024, tk=512, unroll=8):
    b, h, s, w = q.shape
    return pl.pallas_call(
        functools.partial(_mla_attn_kernel, tk=tk, nk=s // tk, unroll=unroll),
        grid=(b, h, s // tq),
        in_specs=[pl.BlockSpec(memory_space=pltpu.SMEM),
                  pl.BlockSpec((1, 1, tq, w), lambda bi, hi, qi: (bi, hi, qi, 0)),
                  pl.BlockSpec((1, 1, s, w), lambda bi, hi, qi: (bi, hi, 0, 0)),
                  pl.BlockSpec((1, 1, MLA_V, s), lambda bi, hi, qi: (bi, hi, 0, 0))],
        out_specs=pl.BlockSpec((1, tq, MLA_V), lambda bi, hi, qi: (bi, qi, hi)),
        out_shape=jax.ShapeDtypeStruct((b, s, h * MLA_V), BF16),
        compiler_params=_cparams(("parallel", "parallel", "arbitrary")),
        name="mla_attn",
    )(flag, q, k, v)


def _diff_prep_kernel(q_ref, k_ref, v_ref, gq_ref, gk_ref, qn_ref, kn_ref, vt_ref, *, q_scale):
    for c in range(DIFF_HEADS):
        sl = slice(c * LANES, (c + 1) * LANES)
        q = q_ref[:, sl].astype(F32)
        k = k_ref[:, sl].astype(F32)
        qn_ref[:, sl] = (q * lax.rsqrt(_ms_half_lanes(q) + EPS) * (gq_ref[...] * q_scale)).astype(BF16)
        kn_ref[:, sl] = (k * lax.rsqrt(_ms_half_lanes(k) + EPS) * gk_ref[...]).astype(BF16)
        vt_ref[0, c] = v_ref[:, sl].astype(F32).T.astype(BF16)


def _diff_prep(p, gq, gk, *, batch, seq, tm=1024):
    t = p.shape[0]
    nb = seq // tm
    w = DIFF_HEADS * 2 * DIFF_HEAD_DIM
    row = lambda blk: (lambda b, i: (b * nb + i, blk))
    gain = pl.BlockSpec((1, LANES), lambda b, i: (0, 0))
    return pl.pallas_call(
        functools.partial(_diff_prep_kernel, q_scale=DIFF_Q_SCALE),
        grid=(batch, nb),
        in_specs=[pl.BlockSpec((tm, w), row(P_DQ_BLK)), pl.BlockSpec((tm, w), row(P_DK_BLK)),
                  pl.BlockSpec((tm, w), row(P_DV_BLK)), gain, gain],
        out_specs=[pl.BlockSpec((tm, w), row(0)), pl.BlockSpec((tm, w), row(0)),
                   pl.BlockSpec((1, DIFF_HEADS, LANES, tm), lambda b, i: (b, 0, 0, i))],
        out_shape=[jax.ShapeDtypeStruct((t, w), BF16), jax.ShapeDtypeStruct((t, w), BF16),
                   jax.ShapeDtypeStruct((batch, DIFF_HEADS, LANES, seq), BF16)],
        compiler_params=_cparams(("parallel", "parallel")),
        name="diff_prep",
    )(p, p, p, gq, gk)


def _diff_attn_kernel(scal_ref, q_ref, k_ref, vt_ref, pq_ref, pk_ref, lam_ref, g_ref, o_ref,
                      *, tk, nk, unroll, lambda_init):
    q = q_ref[0]
    tq = q.shape[0]
    lane = lax.broadcasted_iota(jnp.int32, q.shape, 1)
    zero = jnp.zeros_like(q)
    q0 = jnp.where(lane < DIFF_HEAD_DIM, q, zero)
    q1 = jnp.where(lane < DIFF_HEAD_DIM, zero, q)
    pq = pq_ref[0]
    slope = scal_ref[pl.program_id(1)] * LOG2E
    shift = scal_ref[DIFF_HEADS]
    use_bounded = scal_ref[DIFF_HEADS + 1]

    spq = slope * pq

    def chunk(j, shifted):
        off = pl.multiple_of(j * tk, tk)
        k = k_ref[0, pl.ds(off, tk), :]
        vt = vt_ref[0, 0, :, pl.ds(off, tk)]
        spk = slope * pk_ref[0, pl.ds(off, tk), :]
        bias = jnp.abs(jnp.concatenate([spk] * (tq // LANES), axis=1) - spq)
        return k, vt, (bias + shift if shifted else bias)

    def pv(vt, p):
        return jnp.dot(vt, p.astype(BF16), preferred_element_type=F32)

    def bounded():
        def body(j, carry):
            l0, a0, l1, a1 = carry
            k, vt, bias = chunk(j, False)
            p0 = jnp.exp2(_nt_dot(k, q0) - bias)
            p1 = jnp.exp2(_nt_dot(k, q1) - bias)
            return (l0 + _sublane_partial_sum(p0), a0 + pv(vt, p0),
                    l1 + _sublane_partial_sum(p1), a1 + pv(vt, p1))

        zl = jnp.zeros((8, tq), F32)
        za = jnp.zeros((LANES, tq), F32)
        l0, a0, l1, a1 = lax.fori_loop(0, nk, body, (zl, za, zl, za), unroll=unroll)
        return (a0 / jnp.sum(l0, axis=0, keepdims=True), a1 / jnp.sum(l1, axis=0, keepdims=True))

    def online():
        def body(j, carry):
            m0, l0, a0, m1, l1, a1 = carry
            k, vt, bias = chunk(j, True)
            p0, al0, m0, l0 = _online_update_t(_nt_dot(k, q0) - bias, m0, l0)
            a0 = al0 * a0 + pv(vt, p0)
            p1, al1, m1, l1 = _online_update_t(_nt_dot(k, q1) - bias, m1, l1)
            a1 = al1 * a1 + pv(vt, p1)
            return m0, l0, a0, m1, l1, a1

        neg = jnp.full((1, tq), NEG_BIG, F32)
        z1 = jnp.zeros((1, tq), F32)
        za = jnp.zeros((LANES, tq), F32)
        _, l0, a0, _, l1, a1 = lax.fori_loop(0, nk, body, (neg, z1, za, neg, z1, za))
        return a0 / l0, a1 / l1

    o0, o1 = lax.cond(use_bounded != 0.0, bounded, online)
    lam = (jnp.exp(jnp.sum(lam_ref[0:1, :] * lam_ref[1:2, :], axis=-1, keepdims=True))
           - jnp.exp(jnp.sum(lam_ref[2:3, :] * lam_ref[3:4, :], axis=-1, keepdims=True))
           + lambda_init)
    o = (o0 - lam * o1).T
    o_ref[0] = (_rms(o, g_ref[...]) * (1.0 - lambda_init)).astype(BF16)


def _diff_attn(scal, qn, kn, vt, pos_row, pos_col, lam4, subln_g, *, lambda_init, tq=512, tk=512, unroll=8):
    b, s, _ = qn.shape
    return pl.pallas_call(
        functools.partial(_diff_attn_kernel, tk=tk, nk=s // tk, unroll=unroll, lambda_init=lambda_init),
        grid=(b, DIFF_HEADS, s // tq),
        in_specs=[pl.BlockSpec(memory_space=pltpu.SMEM),
                  pl.BlockSpec((1, tq, LANES), lambda bi, hi, qi: (bi, qi, hi)),
                  pl.BlockSpec((1, s, LANES), lambda bi, hi, qi: (bi, 0, hi)),
                  pl.BlockSpec((1, 1, LANES, s), lambda bi, hi, qi: (bi, hi, 0, 0)),
                  pl.BlockSpec((1, 1, tq), lambda bi, hi, qi: (bi, 0, qi)),
                  pl.BlockSpec((1, s, LANES), lambda bi, hi, qi: (bi, 0, 0)),
                  pl.BlockSpec((4, DIFF_HEAD_DIM), lambda bi, hi, qi: (0, 0)),
                  pl.BlockSpec((1, LANES), lambda bi, hi, qi: (0, 0))],
        out_specs=pl.BlockSpec((1, tq, LANES), lambda bi, hi, qi: (bi, qi, hi)),
        out_shape=jax.ShapeDtypeStruct((b, s, DIFF_HEADS * LANES), BF16),
        compiler_params=_cparams(("parallel", "parallel", "arbitrary")),
        name="diff_attn",
    )(scal, qn, kn, vt, pos_row, pos_col, lam4, subln_g)


def _split_dot_f32(a, w_hi, w_lo):
    a_hi = a.astype(BF16)
    a_lo = (a - a_hi.astype(F32)).astype(BF16)
    return (jnp.dot(a_hi, w_hi, preferred_element_type=F32)
            + jnp.dot(a_lo, w_hi, preferred_element_type=F32)
            + jnp.dot(a_hi, w_lo, preferred_element_type=F32))


def _merge_kernel(x_ref, oa_ref, ob_ref, ga_ref, gb_ref, wa_ref, wb_ref, wo_ref, gf_ref, *rest,
                  with_router):
    if with_router:
        wr_hi_ref, wr_lo_ref, xo_ref, h_ref, route_ref = rest
    else:
        xo_ref, h_ref = rest
    ya = jnp.dot(oa_ref[...], wa_ref[...], preferred_element_type=F32)
    yb = jnp.dot(ob_ref[...], wb_ref[...], preferred_element_type=F32)
    merged = ga_ref[...].astype(F32) * ya + gb_ref[...].astype(F32) * yb
    x = x_ref[...] + jnp.dot(merged.astype(BF16), wo_ref[...], preferred_element_type=F32)
    xo_ref[...] = x
    h = _rms(x, gf_ref[...])
    h_ref[...] = h.astype(h_ref.dtype)
    if with_router:
        route_ref[...] = _top2_route(_split_dot_f32(h, wr_hi_ref[...], wr_lo_ref[...]))


def _merge(x, oa, ob, p, wa, wb, wo, gf, router=None, *, tm=512):
    t, d = x.shape
    ga_blk = P_GATE_COL0 // d
    row = lambda i: (i, 0)
    full = lambda a: pl.BlockSpec(a.shape, lambda i: (0,) * a.ndim)
    in_specs = [pl.BlockSpec((tm, d), row), pl.BlockSpec((tm, d), row), pl.BlockSpec((tm, d), row),
                pl.BlockSpec((tm, d), lambda i: (i, ga_blk)),
                pl.BlockSpec((tm, d), lambda i: (i, ga_blk + 1)),
                full(wa), full(wb), full(wo), full(gf)]
    args = [x, oa, ob, p, p, wa, wb, wo, gf]
    out_specs = [pl.BlockSpec((tm, d), row), pl.BlockSpec((tm, d), row)]
    h_dtype = BF16 if router is None else F32
    out_shape = [jax.ShapeDtypeStruct((t, d), F32), jax.ShapeDtypeStruct((t, d), h_dtype)]
    if router is not None:
        in_specs += [full(router[0]), full(router[1])]
        args += list(router)
        out_specs.append(pl.BlockSpec((tm, LANES), row))
        out_shape.append(jax.ShapeDtypeStruct((t, LANES), F32))
    return pl.pallas_call(
        functools.partial(_merge_kernel, with_router=router is not None),
        grid=(t // tm,),
        in_specs=in_specs,
        out_specs=out_specs,
        out_shape=out_shape,
        compiler_params=_cparams(("parallel",)),
        name="merge_router" if router is not None else "merge",
    )(*args)


def _swiglu_act(h, wg, wu):
    g = jnp.dot(h, wg, preferred_element_type=F32)
    u = jnp.dot(h, wu, preferred_element_type=F32)
    return (g * jax.nn.sigmoid(g) * u).astype(BF16)


def _ffn_kernel(x_ref, h_ref, wg_ref, wu_ref, wd_ref, o_ref, acc_ref):
    f = pl.program_id(1)
    y = jnp.dot(_swiglu_act(h_ref[...], wg_ref[...], wu_ref[...]), wd_ref[...],
                preferred_element_type=F32)

    @pl.when(f == 0)
    def _():
        acc_ref[...] = x_ref[...] + y

    @pl.when(f > 0)
    def _():
        acc_ref[...] += y

    @pl.when(f == pl.num_programs(1) - 1)
    def _():
        o_ref[...] = acc_ref[...]


def _ffn(x, h, wg, wu, wd, *, tm=1024, tf=1408):
    t, d = x.shape
    ff = wg.shape[1]
    return pl.pallas_call(
        _ffn_kernel,
        grid=(t // tm, ff // tf),
        in_specs=[pl.BlockSpec((tm, d), lambda i, f: (i, 0)),
                  pl.BlockSpec((tm, d), lambda i, f: (i, 0)),
                  pl.BlockSpec((d, tf), lambda i, f: (0, f)),
                  pl.BlockSpec((d, tf), lambda i, f: (0, f)),
                  pl.BlockSpec((tf, d), lambda i, f: (f, 0))],
        out_specs=pl.BlockSpec((tm, d), lambda i, f: (i, 0)),
        out_shape=jax.ShapeDtypeStruct((t, d), F32),
        scratch_shapes=[pltpu.VMEM((tm, d), F32)],
        compiler_params=_cparams(("parallel", "arbitrary")),
        name="ffn_dense",
    )(x, h, wg, wu, wd)


def _top2_route(logits):
    lane_i = lax.broadcasted_iota(jnp.int32, logits.shape, 1)
    lane = lane_i.astype(F32)
    neg = jnp.float32(-jnp.inf)
    z = jnp.where(lane_i < N_EXPERTS, logits, neg)
    m1 = jnp.max(z, axis=-1, keepdims=True)
    i1 = jnp.min(jnp.where(z == m1, lane, float(LANES)), axis=-1, keepdims=True)
    z2 = jnp.where(lane == i1, neg, z)
    m2 = jnp.max(z2, axis=-1, keepdims=True)
    i2 = jnp.min(jnp.where(z2 == m2, lane, float(LANES)), axis=-1, keepdims=True)
    e2 = jnp.exp(m2 - m1)
    w1 = 1.0 / (1.0 + e2)
    w2 = e2 / (1.0 + e2)
    return jnp.where(lane_i == 0, i1, jnp.where(lane_i == 1, i2,
                     jnp.where(lane_i == 2, w1, jnp.where(lane_i == 3, w2, 0.0))))


def _route_plan(route, tm):
    t = route.shape[0]
    e = route[:, 0:TOP_K].astype(jnp.int32)
    sel = (e[:, :, None] == jnp.arange(N_EXPERTS, dtype=jnp.int32)).astype(jnp.int32).sum(axis=1)
    csum = jnp.cumsum(sel, axis=0)
    padded = (csum[-1] + tm - 1) // tm * tm
    end = jnp.cumsum(padded)
    pos = (end - padded)[e] + jnp.take_along_axis(csum - sel, e, axis=1)
    n_rows = TOP_K * t + N_EXPERTS * tm
    flat = pos.reshape(-1)
    tok = jnp.zeros((n_rows,), jnp.int32).at[flat].set(
        jnp.repeat(jnp.arange(t, dtype=jnp.int32), TOP_K), unique_indices=True)
    n_tiles = n_rows // tm
    row0 = jnp.arange(n_tiles, dtype=jnp.int32) * tm
    valid = (row0 < end[-1]).astype(jnp.int32)
    tile_e = jnp.searchsorted(end, jnp.minimum(row0, end[-1] - 1), side="right").astype(jnp.int32)
    return tok.reshape(n_tiles, 1, tm), tile_e, valid, pos


def _moe_gmm_kernel(tile_e_ref, valid_ref, tok_ref, tok_next_ref, h_hbm, wg_ref, wu_ref, wd_ref,
                    o_ref, xg_ref, xb_ref, acc_ref, sem, *, nf, gather_steps):
    del tile_e_ref
    i = pl.program_id(0)
    f = pl.program_id(1)
    tm = xb_ref.shape[0]
    slot = i % 2

    def row_copy(tok, r, s):
        return pltpu.make_async_copy(h_hbm.at[pl.ds(tok[0, 0, r], 1), :],
                                     xg_ref.at[s, pl.ds(r, 1), :], sem.at[s])

    @pl.when((i == 0) & (f == 0))
    def _():
        def body(r, c):
            row_copy(tok_ref, r, 0).start()
            return c
        lax.fori_loop(0, tm, body, 0, unroll=8)

    valid = valid_ref[i] != 0
    prev_valid = valid_ref[jnp.maximum(i - 1, 0)] != 0

    @pl.when((f == 0) & ((i == 0) | prev_valid))
    def _():
        pltpu.make_async_copy(h_hbm.at[pl.ds(0, tm), :], xg_ref.at[slot], sem.at[slot]).wait()
        xb_ref[...] = xg_ref[slot].astype(BF16)

    def step(request_rows):
        if request_rows:
            rows = tm // gather_steps
            for r in range(rows):
                row_copy(tok_next_ref, f * rows + r, 1 - slot).start()
        y = jnp.dot(_swiglu_act(xb_ref[...], wg_ref[0], wu_ref[0]), wd_ref[0],
                    preferred_element_type=F32)

        @pl.when(f == 0)
        def _():
            acc_ref[...] = y

        @pl.when((f > 0) & (f < nf - 1))
        def _():
            acc_ref[...] += y

        @pl.when(f == nf - 1)
        def _():
            o_ref[...] = acc_ref[...] + y

    pl.when(valid & (f < gather_steps))(functools.partial(step, True))
    pl.when(valid & (f >= gather_steps))(functools.partial(step, False))

    @pl.when(jnp.logical_not(valid) & (f == nf - 1))
    def _():
        o_ref[...] = jnp.zeros_like(o_ref)


def _moe_gmm(tok, tile_e, valid, h, wg, wu, wd, *, tf=896, gather_steps=4):
    n_tiles, _, tm = tok.shape
    d = h.shape[1]
    ff = wg.shape[2]
    nf = ff // tf
    assert 2 <= gather_steps <= nf and tm % gather_steps == 0
    tok_spec = lambda imap: pl.BlockSpec((1, 1, tm), imap, memory_space=pltpu.SMEM)
    grid_spec = pltpu.PrefetchScalarGridSpec(
        num_scalar_prefetch=2,
        grid=(n_tiles, nf),
        in_specs=[tok_spec(lambda i, f, te, va: (i, 0, 0)),
                  tok_spec(lambda i, f, te, va: (jnp.minimum(i + 1, n_tiles - 1), 0, 0)),
                  pl.BlockSpec(memory_space=pl.ANY),
                  pl.BlockSpec((1, d, tf), lambda i, f, te, va: (te[i], 0, f)),
                  pl.BlockSpec((1, d, tf), lambda i, f, te, va: (te[i], 0, f)),
                  pl.BlockSpec((1, tf, d), lambda i, f, te, va: (te[i], f, 0))],
        out_specs=pl.BlockSpec((tm, d), lambda i, f, te, va: (i, 0)),
        scratch_shapes=[pltpu.VMEM((2, tm, d), F32), pltpu.VMEM((tm, d), BF16), pltpu.VMEM((tm, d), F32),
                        pltpu.SemaphoreType.DMA((2,))],
    )
    return pl.pallas_call(
        functools.partial(_moe_gmm_kernel, nf=nf, gather_steps=gather_steps),
        grid_spec=grid_spec,
        out_shape=jax.ShapeDtypeStruct((n_tiles * tm, d), F32),
        compiler_params=_cparams(("arbitrary", "arbitrary")),
        name="moe_gmm",
    )(tile_e, valid, tok, tok, h, wg, wu, wd)


def _moe_combine_kernel(pos_ref, pos_next_ref, x_ref, route_ref, y_hbm, o_ref, buf_ref, sem):
    i = pl.program_id(0)
    tc = x_ref.shape[0]
    slot = i % 2

    def gather_rows(pos, s):
        def body(r, c):
            for k in range(TOP_K):
                pltpu.make_async_copy(y_hbm.at[pl.ds(pos[0, 0, k * tc + r], 1), :],
                                      buf_ref.at[s, pl.ds(k * tc + r, 1), :], sem.at[s]).start()
            return c
        lax.fori_loop(0, tc, body, 0, unroll=4)

    @pl.when(i == 0)
    def _():
        gather_rows(pos_ref, 0)

    @pl.when(i + 1 < pl.num_programs(0))
    def _():
        gather_rows(pos_next_ref, 1 - slot)

    pltpu.make_async_copy(y_hbm.at[pl.ds(0, TOP_K * tc), :], buf_ref.at[slot], sem.at[slot]).wait()
    y = buf_ref[slot]
    w = route_ref[...]
    o_ref[...] = (x_ref[...] + w[:, TOP_K:TOP_K + 1] * y[0:tc]
                  + w[:, TOP_K + 1:TOP_K + 2] * y[tc:2 * tc])


def _moe_combine(x, route, y_rows, pos, *, tc=512):
    t, d = x.shape
    n = t // tc
    pos_t = pos.reshape(n, tc, TOP_K).transpose(0, 2, 1).reshape(n, 1, TOP_K * tc)
    pos_spec = lambda imap: pl.BlockSpec((1, 1, TOP_K * tc), imap, memory_space=pltpu.SMEM)
    return pl.pallas_call(
        _moe_combine_kernel,
        grid=(n,),
        in_specs=[pos_spec(lambda i: (i, 0, 0)),
                  pos_spec(lambda i: (jnp.minimum(i + 1, n - 1), 0, 0)),
                  pl.BlockSpec((tc, d), lambda i: (i, 0)),
                  pl.BlockSpec((tc, LANES), lambda i: (i, 0)),
                  pl.BlockSpec(memory_space=pl.ANY)],
        out_specs=pl.BlockSpec((tc, d), lambda i: (i, 0)),
        out_shape=jax.ShapeDtypeStruct((t, d), F32),
        scratch_shapes=[pltpu.VMEM((2, TOP_K * tc, d), F32), pltpu.SemaphoreType.DMA((2,))],
        compiler_params=_cparams(("arbitrary",)),
        name="moe_combine",
    )(pos_t, pos_t, x, route, y_rows)


def _row(v, reps=1):
    return jnp.tile(v.astype(F32), reps).reshape(1, -1)


def kernel(x, positions, attn_norm_g, w_in, w_gate, mla_q_norm_g, mla_kv_norm_g, mla_w_uq, mla_w_ukv, mla_qn_nope_g, mla_qn_rope_g, mla_kn_nope_g, mla_kn_rope_g, diff_qn_g, diff_kn_g, diff_lam_q1, diff_lam_k1, diff_lam_q2, diff_lam_k2, diff_subln_g, w_branch_a, w_branch_b, w_out, ffn_norm_g, dense_w_gate, dense_w_up, dense_w_down, moe_w_router, moe_w_gate, moe_w_up, moe_w_down):
    batch, seq, d = x.shape
    t = batch * seq
    xf = x.reshape(t, d)
    pos_f = positions.astype(F32)
    cos, sin = _rope_tables(pos_f.reshape(t, 1))
    pos_col = jnp.broadcast_to(pos_f.reshape(batch, seq, 1), (batch, seq, LANES))
    pos_row = pos_f.reshape(batch, 1, seq)
    slopes = 2.0 ** (-8.0 * jnp.arange(1, DIFF_HEADS + 1, dtype=F32) / DIFF_HEADS)
    n_in_head = MLA_Q_LORA + MLA_KV_LORA + MLA_ROPE

    for layer in range(DEPTH):
        w1 = jnp.concatenate(
            [w_in[layer][:, :n_in_head], jnp.zeros((d, 1024 - n_in_head), F32),
             w_in[layer][:, n_in_head:], w_gate[layer]], axis=1).astype(BF16)
        wuq = mla_w_uq[layer].reshape(MLA_Q_LORA, MLA_HEADS, MLA_NOPE + MLA_ROPE)
        wuq = jnp.concatenate([wuq[:, :, :MLA_NOPE].reshape(MLA_Q_LORA, -1),
                               wuq[:, :, MLA_NOPE:].reshape(MLA_Q_LORA, -1)], axis=1).astype(BF16)
        wukv = mla_w_ukv[layer].reshape(MLA_KV_LORA, MLA_HEADS, MLA_NOPE + MLA_V)
        wukv = jnp.concatenate([wukv[:, :, :MLA_NOPE].reshape(MLA_KV_LORA, -1),
                                wukv[:, :, MLA_NOPE:].reshape(MLA_KV_LORA, -1)], axis=1).astype(BF16)
        gkr = jnp.concatenate([mla_kn_rope_g[layer].astype(F32), jnp.zeros((LANES - MLA_ROPE,), F32)]).reshape(1, -1)

        p = _norm_proj(xf, _row(attn_norm_g[layer]), w1)
        q_bound = MLA_Q_SCALE * jnp.sqrt(_norm_bound(MLA_NOPE, mla_qn_nope_g[layer]) ** 2
                                         + _norm_bound(MLA_ROPE, mla_qn_rope_g[layer]) ** 2)
        k_bound = jnp.sqrt(_norm_bound(MLA_NOPE, mla_kn_nope_g[layer]) ** 2
                           + _norm_bound(MLA_ROPE, mla_kn_rope_g[layer]) ** 2)
        shift, use_bounded = _score_shift(q_bound, k_bound)
        shift_row = jnp.where(jnp.arange(LANES) % 64 == 0, -shift, 0.0).astype(F32).reshape(1, LANES)
        q_m, k_m, v_m = _mla_prep(
            p, cos, sin, _row(mla_q_norm_g[layer]), _row(mla_kv_norm_g[layer]), wuq, wukv,
            _row(mla_qn_nope_g[layer]), _row(mla_qn_rope_g[layer], 2), _row(mla_kn_nope_g[layer]), gkr,
            shift_row, batch=batch, seq=seq)
        o_a = _mla_attn(use_bounded.astype(jnp.int32).reshape(1), q_m, k_m, v_m).reshape(t, -1)

        lambda_init = 0.8 - 0.6 * math.exp(-0.3 * layer)
        qn, kn, vt_d = _diff_prep(p, _row(diff_qn_g[layer], 2), _row(diff_kn_g[layer], 2),
                                  batch=batch, seq=seq)
        lam4 = jnp.stack([diff_lam_q1[layer], diff_lam_k1[layer],
                          diff_lam_q2[layer], diff_lam_k2[layer]]).astype(F32)
        d_shift, d_bounded = _score_shift(DIFF_Q_SCALE * _norm_bound(DIFF_HEAD_DIM, diff_qn_g[layer]),
                                          _norm_bound(DIFF_HEAD_DIM, diff_kn_g[layer]))
        scal = jnp.concatenate([slopes, d_shift.reshape(1), d_bounded.astype(F32).reshape(1)])
        o_b = _diff_attn(scal, qn.reshape(batch, seq, -1), kn.reshape(batch, seq, -1), vt_d,
                         pos_row, pos_col, lam4, _row(diff_subln_g[layer]),
                         lambda_init=lambda_init).reshape(t, -1)

        j = layer // 2
        wa, wb, wo = (w.astype(BF16) for w in (w_branch_a[layer], w_branch_b[layer], w_out[layer]))
        if layer % 2 == 0:
            xf, h = _merge(xf, o_a, o_b, p, wa, wb, wo, _row(ffn_norm_g[layer]))
            xf = _ffn(xf, h, dense_w_gate[j].astype(BF16), dense_w_up[j].astype(BF16),
                      dense_w_down[j].astype(BF16))
        else:
            wr = jnp.pad(moe_w_router[j].astype(F32), ((0, 0), (0, LANES - N_EXPERTS)))
            wr_hi = wr.astype(BF16)
            wr_lo = (wr - wr_hi.astype(F32)).astype(BF16)
            xf, h, route = _merge(xf, o_a, o_b, p, wa, wb, wo, _row(ffn_norm_g[layer]),
                                  router=(wr_hi, wr_lo))
            tok, tile_e, valid, pos = _route_plan(route, MOE_TILE_ROWS)
            y_rows = _moe_gmm(tok, tile_e, valid, h, moe_w_gate[j].astype(BF16),
                              moe_w_up[j].astype(BF16), moe_w_down[j].astype(BF16))
            xf = _moe_combine(xf, route, y_rows, pos)
    return xf.reshape(batch, seq, d)
```

```python
import functools
import math

import jax
import jax.numpy as jnp
from jax import lax
from jax.experimental import pallas as pl
from jax.experimental.pallas import tpu as pltpu

D_MODEL = 1024
DEPTH = 2
MLA_HEADS = 8
MLA_Q_LORA = 512
MLA_KV_LORA = 256
MLA_NOPE = 128
MLA_ROPE = 64
MLA_V = 128
ROPE_THETA = 10000.0
DIFF_HEADS = 8
DIFF_HEAD_DIM = 64
D_FF_DENSE = 2816
N_EXPERTS = 8
TOP_K = 2
D_FF_EXPERT = 3584
EPS = 1e-6

LANES = 128
LOG2E = 1.4426950408889634
MLA_Q_SCALE = (MLA_NOPE + MLA_ROPE) ** -0.5 * LOG2E
DIFF_Q_SCALE = DIFF_HEAD_DIM ** -0.5 * LOG2E
NEG_BIG = -1e30
BF16 = jnp.bfloat16
F32 = jnp.float32

P_WIDTH = 6144
P_DQ_BLK = 1
P_DK_BLK = 2
P_DV_BLK = 3
P_GATE_COL0 = 4096
MOE_TILE_ROWS = 1024
VMEM_LIMIT = 52 * 1024 * 1024


def _cparams(sem):
    return pltpu.CompilerParams(dimension_semantics=sem, vmem_limit_bytes=VMEM_LIMIT)


def _rms(x, gain):
    ms = jnp.mean(x * x, axis=-1, keepdims=True)
    return x * lax.rsqrt(ms + EPS) * gain


def _ms_half_lanes(xb):
    lane = lax.broadcasted_iota(jnp.int32, xb.shape, 1)
    lo = lane < 64
    sq = xb * xb
    s_lo = jnp.sum(jnp.where(lo, sq, 0.0), axis=-1, keepdims=True)
    s_hi = jnp.sum(jnp.where(lo, 0.0, sq), axis=-1, keepdims=True)
    return jnp.where(lo, s_lo, s_hi) * (1.0 / 64.0)


def _norm_proj_kernel(x_ref, g_ref, w_ref, o_ref, h_ref, *, gate_tile0):
    j = pl.program_id(1)

    @pl.when(j == 0)
    def _():
        h_ref[...] = _rms(x_ref[...], g_ref[...]).astype(BF16)

    y = jnp.dot(h_ref[...], w_ref[...], preferred_element_type=F32)

    @pl.when(j < gate_tile0)
    def _():
        o_ref[...] = y.astype(BF16)

    @pl.when(j >= gate_tile0)
    def _():
        o_ref[...] = jax.nn.sigmoid(y).astype(BF16)


def _norm_proj(x, gain, w, *, tm=1024, tn=2048):
    t, d = x.shape
    n = w.shape[1]
    return pl.pallas_call(
        functools.partial(_norm_proj_kernel, gate_tile0=P_GATE_COL0 // tn),
        grid=(t // tm, n // tn),
        in_specs=[
            pl.BlockSpec((tm, d), lambda i, j: (i, 0)),
            pl.BlockSpec((1, d), lambda i, j: (0, 0)),
            pl.BlockSpec((d, tn), lambda i, j: (0, j)),
        ],
        out_specs=pl.BlockSpec((tm, tn), lambda i, j: (i, j)),
        out_shape=jax.ShapeDtypeStruct((t, n), BF16),
        scratch_shapes=[pltpu.VMEM((tm, d), BF16)],
        compiler_params=_cparams(("parallel", "arbitrary")),
        name="norm_proj",
    )(x, gain, w)


def _rope_table_kernel(pos_ref, freq_ref, cos_ref, sin_ref):
    ang = pos_ref[...] * freq_ref[...]
    lane = lax.broadcasted_iota(jnp.int32, ang.shape, 1)
    first_half = (lane % 64) < 32
    cos_ref[...] = jnp.cos(ang)
    sin_ref[...] = jnp.where(first_half, -1.0, 1.0) * jnp.sin(ang)


def _rope_tables(pos_col, *, tm=2048):
    t = pos_col.shape[0]
    half = MLA_ROPE // 2
    inv_freq = ROPE_THETA ** (-jnp.arange(half, dtype=F32) / half)
    freq = jnp.tile(inv_freq, LANES // half).reshape(1, LANES)
    return pl.pallas_call(
        _rope_table_kernel,
        grid=(t // tm,),
        in_specs=[pl.BlockSpec((tm, 1), lambda i: (i, 0)),
                  pl.BlockSpec((1, LANES), lambda i: (0, 0))],
        out_specs=[pl.BlockSpec((tm, LANES), lambda i: (i, 0))] * 2,
        out_shape=[jax.ShapeDtypeStruct((t, LANES), F32)] * 2,
        compiler_params=_cparams(("parallel",)),
        name="rope_tables",
    )(pos_col, freq)


def _rope(n, cos, sin_signed):
    w = n.shape[-1]
    lane = lax.broadcasted_iota(jnp.int32, n.shape, 1)
    first_half = (lane % 64) < 32
    swapped = jnp.where(first_half, pltpu.roll(n, w - 32, 1), pltpu.roll(n, 32, 1))
    return n * cos + swapped * sin_signed


def _mla_prep_kernel(p_ref, cos_ref, sin_ref, gq_ref, gkv_ref, wuq_ref, wukv_ref,
                     gqn_ref, gqr_ref, gkn_ref, gkr_ref, shift_ref, q_ref, k_ref, v_ref, *, q_scale):
    cq = p_ref[:, 0:MLA_Q_LORA].astype(F32)
    ckv = p_ref[:, MLA_Q_LORA:MLA_Q_LORA + MLA_KV_LORA].astype(F32)
    kr = p_ref[:, 768:896].astype(F32)
    cos = cos_ref[...]
    sin = sin_ref[...]

    q = jnp.dot(_rms(cq, gq_ref[...]).astype(BF16), wuq_ref[...], preferred_element_type=F32)
    kv = jnp.dot(_rms(ckv, gkv_ref[...]).astype(BF16), wukv_ref[...], preferred_element_type=F32)

    n_nope = MLA_HEADS * MLA_NOPE
    kr_ms = jnp.sum(kr * kr, axis=-1, keepdims=True) * (1.0 / MLA_ROPE)
    kr_n = kr * lax.rsqrt(kr_ms + EPS) * gkr_ref[...]
    kr_r = _rope(kr_n, cos, sin)
    kr_dup = kr_r + pltpu.roll(kr_r, 64, 1)

    lane = lax.broadcasted_iota(jnp.int32, (q.shape[0], LANES), 1)
    one_hot = jnp.where(lane % 64 == 0, 1.0, 0.0)
    for pair in range(MLA_HEADS // 2):
        qr = q[:, n_nope + pair * LANES:n_nope + (pair + 1) * LANES]
        qr_n = qr * lax.rsqrt(_ms_half_lanes(qr) + EPS) * gqr_ref[...]
        qr_r = _rope(qr_n, cos, sin) * q_scale
        for sub in range(2):
            h = 2 * pair + sub
            keep = (lane < 64) if sub == 0 else (lane >= 64)
            qn = _rms(q[:, h * MLA_NOPE:(h + 1) * MLA_NOPE], gqn_ref[...]) * q_scale
            q_ref[0, h, 0:LANES, :] = qn.T.astype(BF16)
            q_ref[0, h, LANES:2 * LANES, :] = jnp.where(keep, qr_r, shift_ref[...]).T.astype(BF16)
            kn = _rms(kv[:, h * MLA_NOPE:(h + 1) * MLA_NOPE], gkn_ref[...])
            k_ref[0, h, :, 0:LANES] = kn.astype(BF16)
            k_ref[0, h, :, LANES:2 * LANES] = jnp.where(keep, kr_dup, one_hot).astype(BF16)
            v_ref[0, h, :, :] = kv[:, n_nope + h * MLA_V:n_nope + (h + 1) * MLA_V].T.astype(BF16)


def _mla_prep(p, cos, sin, gq, gkv, wuq, wukv, gqn, gqr, gkn, gkr, shift_row, *, batch, seq, tm=512):
    nb = seq // tm
    full = lambda a: pl.BlockSpec(a.shape, lambda b, i: (0,) * a.ndim)
    row = lambda b, i: (b * nb + i, 0)
    hshape = lambda w: jax.ShapeDtypeStruct((batch, MLA_HEADS, seq, w), BF16)
    hspec = lambda w: pl.BlockSpec((1, MLA_HEADS, tm, w), lambda b, i: (b, 0, i, 0))
    tshape = lambda w: jax.ShapeDtypeStruct((batch, MLA_HEADS, w, seq), BF16)
    tspec = lambda w: pl.BlockSpec((1, MLA_HEADS, w, tm), lambda b, i: (b, 0, 0, i))
    return pl.pallas_call(
        functools.partial(_mla_prep_kernel, q_scale=MLA_Q_SCALE),
        grid=(batch, nb),
        in_specs=[pl.BlockSpec((tm, 1024), row),
                  pl.BlockSpec((tm, LANES), row), pl.BlockSpec((tm, LANES), row),
                  full(gq), full(gkv), full(wuq), full(wukv),
                  full(gqn), full(gqr), full(gkn), full(gkr), full(shift_row)],
        out_specs=[tspec(2 * LANES), hspec(2 * LANES), tspec(MLA_V)],
        out_shape=[tshape(2 * LANES), hshape(2 * LANES), tshape(MLA_V)],
        compiler_params=_cparams(("parallel", "parallel")),
        name="mla_prep",
    )(p, cos, sin, gq, gkv, wuq, wukv, gqn, gqr, gkn, gkr, shift_row)


BOUND_MARGIN = 1.02
MAX_SHIFT_SPAN = 100.0


def _norm_bound(n, gain):
    return math.sqrt(n) * jnp.max(jnp.abs(gain.astype(F32)))


def _score_shift(q_bound, k_bound):
    shift = (BOUND_MARGIN * q_bound * k_bound).astype(BF16).astype(F32)
    return shift, (2.0 * shift <= MAX_SHIFT_SPAN)


def _nt_dot(a, b):
    return lax.dot_general(a, b, (((1,), (1,)), ((), ())), preferred_element_type=F32)


def _lane_partial_sum(p):
    acc = p[:, 0:LANES]
    for c in range(1, p.shape[1] // LANES):
        acc = acc + p[:, c * LANES:(c + 1) * LANES]
    return acc


def _online_update(s, m, l):
    m_new = jnp.maximum(m, jnp.max(s, axis=-1, keepdims=True))
    alpha = jnp.exp2(m - m_new)
    p = jnp.exp2(s - m_new)
    l_new = alpha * l + jnp.sum(p, axis=-1, keepdims=True)
    return p, alpha, m_new, l_new


def _sublane_partial_sum(p):
    return jnp.sum(p.reshape(p.shape[0] // 8, 8, p.shape[1]), axis=0)


def _online_update_t(s, m, l):
    m_new = jnp.maximum(m, jnp.max(s, axis=0, keepdims=True))
    alpha = jnp.exp2(m - m_new)
    p = jnp.exp2(s - m_new)
    return p, alpha, m_new, alpha * l + jnp.sum(p, axis=0, keepdims=True)


def _mla_attn_kernel(flag_ref, qt_ref, k_ref, vt_ref, o_ref, *, tk, nk, unroll):
    tq = qt_ref.shape[3]
    qt = qt_ref[0, 0]

    def scores(k):
        return jnp.dot(k, qt, preferred_element_type=F32)

    def chunk(j):
        off = pl.multiple_of(j * tk, tk)
        return k_ref[0, 0, pl.ds(off, tk), :], vt_ref[0, 0, :, pl.ds(off, tk)]

    def bounded():
        def body(j, carry):
            l, acc = carry
            k, vt = chunk(j)
            p = jnp.exp2(scores(k))
            return (l + _sublane_partial_sum(p),
                    acc + jnp.dot(vt, p.astype(BF16), preferred_element_type=F32))

        init = (jnp.zeros((8, tq), F32), jnp.zeros((MLA_V, tq), F32))
        l, acc = lax.fori_loop(0, nk, body, init, unroll=unroll)
        return acc / jnp.sum(l, axis=0, keepdims=True)

    def online():
        def body(j, carry):
            m, l, acc = carry
            k, vt = chunk(j)
            p, alpha, m, l = _online_update_t(scores(k), m, l)
            return m, l, alpha * acc + jnp.dot(vt, p.astype(BF16), preferred_element_type=F32)

        init = (jnp.full((1, tq), NEG_BIG, F32), jnp.zeros((1, tq), F32), jnp.zeros((MLA_V, tq), F32))
        _, l, acc = lax.fori_loop(0, nk, body, init)
        return acc / l

    o_ref[0] = lax.cond(flag_ref[0] != 0, bounded, online).T.astype(BF16)


def _mla_attn(flag, qt, k, vt, *, tq=1024, tk=512, unroll=8):
    b, h, s, w = k.shape
    return pl.pallas_call(
        functools.partial(_mla_attn_kernel, tk=tk, nk=s // tk, unroll=unroll),
        grid=(b, h, s // tq),
        in_specs=[pl.BlockSpec(memory_space=pltpu.SMEM),
                  pl.BlockSpec((1, 1, w, tq), lambda bi, hi, qi: (bi, hi, 0, qi)),
                  pl.BlockSpec((1, 1, s, w), lambda bi, hi, qi: (bi, hi, 0, 0)),
                  pl.BlockSpec((1, 1, MLA_V, s), lambda bi, hi, qi: (bi, hi, 0, 0))],
        out_specs=pl.BlockSpec((1, tq, MLA_V), lambda bi, hi, qi: (bi, qi, hi)),
        out_shape=jax.ShapeDtypeStruct((b, s, h * MLA_V), BF16),
        compiler_params=_cparams(("parallel", "parallel", "arbitrary")),
        name="mla_attn",
    )(flag, qt, k, vt)


def _diff_prep_kernel(q_ref, k_ref, v_ref, gq_ref, gk_ref, qt_ref, kn_ref, vt_ref, *, q_scale):
    for c in range(DIFF_HEADS):
        sl = slice(c * LANES, (c + 1) * LANES)
        q = q_ref[:, sl].astype(F32)
        k = k_ref[:, sl].astype(F32)
        qt_ref[0, c] = (q * lax.rsqrt(_ms_half_lanes(q) + EPS) * (gq_ref[...] * q_scale)).T.astype(BF16)
        kn_ref[:, sl] = (k * lax.rsqrt(_ms_half_lanes(k) + EPS) * gk_ref[...]).astype(BF16)
        vt_ref[0, c] = v_ref[:, sl].astype(F32).T.astype(BF16)


def _diff_prep(p, gq, gk, *, batch, seq, tm=1024):
    t = p.shape[0]
    nb = seq // tm
    w = DIFF_HEADS * 2 * DIFF_HEAD_DIM
    row = lambda blk: (lambda b, i: (b * nb + i, blk))
    gain = pl.BlockSpec((1, LANES), lambda b, i: (0, 0))
    tshape = jax.ShapeDtypeStruct((batch, DIFF_HEADS, LANES, seq), BF16)
    tspec = pl.BlockSpec((1, DIFF_HEADS, LANES, tm), lambda b, i: (b, 0, 0, i))
    return pl.pallas_call(
        functools.partial(_diff_prep_kernel, q_scale=DIFF_Q_SCALE),
        grid=(batch, nb),
        in_specs=[pl.BlockSpec((tm, w), row(P_DQ_BLK)), pl.BlockSpec((tm, w), row(P_DK_BLK)),
                  pl.BlockSpec((tm, w), row(P_DV_BLK)), gain, gain],
        out_specs=[tspec, pl.BlockSpec((tm, w), row(0)), tspec],
        out_shape=[tshape, jax.ShapeDtypeStruct((t, w), BF16), tshape],
        compiler_params=_cparams(("parallel", "parallel")),
        name="diff_prep",
    )(p, p, p, gq, gk)


def _diff_attn_kernel(scal_ref, qt_ref, k_ref, vt_ref, pq_ref, pk_ref, lam_ref, g_ref, o_ref,
                      *, tk, nk, unroll, lambda_init):
    qt = qt_ref[0, 0]
    tq = qt.shape[1]
    row = lax.broadcasted_iota(jnp.int32, qt.shape, 0)
    zero = jnp.zeros_like(qt)
    q0t = jnp.where(row < DIFF_HEAD_DIM, qt, zero)
    q1t = jnp.where(row < DIFF_HEAD_DIM, zero, qt)
    pq = pq_ref[0]
    slope = scal_ref[pl.program_id(1)] * LOG2E
    shift = scal_ref[DIFF_HEADS]
    use_bounded = scal_ref[DIFF_HEADS + 1]

    spq = slope * pq

    def chunk(j, shifted):
        off = pl.multiple_of(j * tk, tk)
        k = k_ref[0, pl.ds(off, tk), :]
        vt = vt_ref[0, 0, :, pl.ds(off, tk)]
        spk = slope * pk_ref[0, pl.ds(off, tk), :]
        bias = jnp.abs(jnp.concatenate([spk] * (tq // LANES), axis=1) - spq)
        return k, vt, (bias + shift if shifted else bias)

    def scores(k, qt):
        return jnp.dot(k, qt, preferred_element_type=F32)

    def pv(vt, p):
        return jnp.dot(vt, p.astype(BF16), preferred_element_type=F32)

    def bounded():
        def body(j, carry):
            l0, a0, l1, a1 = carry
            k, vt, bias = chunk(j, False)
            p0 = jnp.exp2(scores(k, q0t) - bias)
            p1 = jnp.exp2(scores(k, q1t) - bias)
            return (l0 + _sublane_partial_sum(p0), a0 + pv(vt, p0),
                    l1 + _sublane_partial_sum(p1), a1 + pv(vt, p1))

        zl = jnp.zeros((8, tq), F32)
        za = jnp.zeros((LANES, tq), F32)
        l0, a0, l1, a1 = lax.fori_loop(0, nk, body, (zl, za, zl, za), unroll=unroll)
        return (a0 / jnp.sum(l0, axis=0, keepdims=True), a1 / jnp.sum(l1, axis=0, keepdims=True))

    def online():
        def body(j, carry):
            m0, l0, a0, m1, l1, a1 = carry
            k, vt, bias = chunk(j, True)
            p0, al0, m0, l0 = _online_update_t(scores(k, q0t) - bias, m0, l0)
            a0 = al0 * a0 + pv(vt, p0)
            p1, al1, m1, l1 = _online_update_t(scores(k, q1t) - bias, m1, l1)
            a1 = al1 * a1 + pv(vt, p1)
            return m0, l0, a0, m1, l1, a1

        neg = jnp.full((1, tq), NEG_BIG, F32)
        z1 = jnp.zeros((1, tq), F32)
        za = jnp.zeros((LANES, tq), F32)
        _, l0, a0, _, l1, a1 = lax.fori_loop(0, nk, body, (neg, z1, za, neg, z1, za))
        return a0 / l0, a1 / l1

    o0, o1 = lax.cond(use_bounded != 0.0, bounded, online)
    lam = (jnp.exp(jnp.sum(lam_ref[0:1, :] * lam_ref[1:2, :], axis=-1, keepdims=True))
           - jnp.exp(jnp.sum(lam_ref[2:3, :] * lam_ref[3:4, :], axis=-1, keepdims=True))
           + lambda_init)
    o = (o0 - lam * o1).T
    o_ref[0] = (_rms(o, g_ref[...]) * (1.0 - lambda_init)).astype(BF16)


def _diff_attn(scal, qt, kn, vt, pos_row, pos_col, lam4, subln_g, *, lambda_init, tq=512, tk=512, unroll=8):
    b, s, _ = kn.shape
    return pl.pallas_call(
        functools.partial(_diff_attn_kernel, tk=tk, nk=s // tk, unroll=unroll, lambda_init=lambda_init),
        grid=(b, DIFF_HEADS, s // tq),
        in_specs=[pl.BlockSpec(memory_space=pltpu.SMEM),
                  pl.BlockSpec((1, 1, LANES, tq), lambda bi, hi, qi: (bi, hi, 0, qi)),
                  pl.BlockSpec((1, s, LANES), lambda bi, hi, qi: (bi, 0, hi)),
                  pl.BlockSpec((1, 1, LANES, s), lambda bi, hi, qi: (bi, hi, 0, 0)),
                  pl.BlockSpec((1, 1, tq), lambda bi, hi, qi: (bi, 0, qi)),
                  pl.BlockSpec((1, s, LANES), lambda bi, hi, qi: (bi, 0, 0)),
                  pl.BlockSpec((4, DIFF_HEAD_DIM), lambda bi, hi, qi: (0, 0)),
                  pl.BlockSpec((1, LANES), lambda bi, hi, qi: (0, 0))],
        out_specs=pl.BlockSpec((1, tq, LANES), lambda bi, hi, qi: (bi, qi, hi)),
        out_shape=jax.ShapeDtypeStruct((b, s, DIFF_HEADS * LANES), BF16),
        compiler_params=_cparams(("parallel", "parallel", "arbitrary")),
        name="diff_attn",
    )(scal, qt, kn, vt, pos_row, pos_col, lam4, subln_g)


def _split_dot_f32(a, w_hi, w_lo):
    a_hi = a.astype(BF16)
    a_lo = (a - a_hi.astype(F32)).astype(BF16)
    return (jnp.dot(a_hi, w_hi, preferred_element_type=F32)
            + jnp.dot(a_lo, w_hi, preferred_element_type=F32)
            + jnp.dot(a_hi, w_lo, preferred_element_type=F32))


def _merge_kernel(x_ref, oa_ref, ob_ref, ga_ref, gb_ref, wa_ref, wb_ref, wo_ref, gf_ref, *rest,
                  with_router):
    if with_router:
        wr_hi_ref, wr_lo_ref, xo_ref, h_ref, route_ref = rest
    else:
        xo_ref, h_ref = rest
    ya = jnp.dot(oa_ref[...], wa_ref[...], preferred_element_type=F32)
    yb = jnp.dot(ob_ref[...], wb_ref[...], preferred_element_type=F32)
    merged = ga_ref[...].astype(F32) * ya + gb_ref[...].astype(F32) * yb
    x = x_ref[...] + jnp.dot(merged.astype(BF16), wo_ref[...], preferred_element_type=F32)
    xo_ref[...] = x
    h = _rms(x, gf_ref[...])
    h_ref[...] = h.astype(h_ref.dtype)
    if with_router:
        route_ref[...] = _top2_route(_split_dot_f32(h, wr_hi_ref[...], wr_lo_ref[...]))


def _merge(x, oa, ob, p, wa, wb, wo, gf, router=None, *, tm=512):
    t, d = x.shape
    ga_blk = P_GATE_COL0 // d
    row = lambda i: (i, 0)
    full = lambda a: pl.BlockSpec(a.shape, lambda i: (0,) * a.ndim)
    in_specs = [pl.BlockSpec((tm, d), row), pl.BlockSpec((tm, d), row), pl.BlockSpec((tm, d), row),
                pl.BlockSpec((tm, d), lambda i: (i, ga_blk)),
                pl.BlockSpec((tm, d), lambda i: (i, ga_blk + 1)),
                full(wa), full(wb), full(wo), full(gf)]
    args = [x, oa, ob, p, p, wa, wb, wo, gf]
    out_specs = [pl.BlockSpec((tm, d), row), pl.BlockSpec((tm, d), row)]
    h_dtype = BF16 if router is None else F32
    out_shape = [jax.ShapeDtypeStruct((t, d), F32), jax.ShapeDtypeStruct((t, d), h_dtype)]
    if router is not None:
        in_specs += [full(router[0]), full(router[1])]
        args += list(router)
        out_specs.append(pl.BlockSpec((tm, LANES), row))
        out_shape.append(jax.ShapeDtypeStruct((t, LANES), F32))
    return pl.pallas_call(
        functools.partial(_merge_kernel, with_router=router is not None),
        grid=(t // tm,),
        in_specs=in_specs,
        out_specs=out_specs,
        out_shape=out_shape,
        compiler_params=_cparams(("parallel",)),
        name="merge_router" if router is not None else "merge",
    )(*args)


def _swiglu_act(h, wg, wu):
    g = jnp.dot(h, wg, preferred_element_type=F32)
    u = jnp.dot(h, wu, preferred_element_type=F32)
    return (g * jax.nn.sigmoid(g) * u).astype(BF16)


def _ffn_kernel(x_ref, h_ref, wg_ref, wu_ref, wd_ref, o_ref, acc_ref):
    f = pl.program_id(1)
    y = jnp.dot(_swiglu_act(h_ref[...], wg_ref[...], wu_ref[...]), wd_ref[...],
                preferred_element_type=F32)

    @pl.when(f == 0)
    def _():
        acc_ref[...] = x_ref[...] + y

    @pl.when(f > 0)
    def _():
        acc_ref[...] += y

    @pl.when(f == pl.num_programs(1) - 1)
    def _():
        o_ref[...] = acc_ref[...]


def _ffn(x, h, wg, wu, wd, *, tm=1024, tf=1408):
    t, d = x.shape
    ff = wg.shape[1]
    return pl.pallas_call(
        _ffn_kernel,
        grid=(t // tm, ff // tf),
        in_specs=[pl.BlockSpec((tm, d), lambda i, f: (i, 0)),
                  pl.BlockSpec((tm, d), lambda i, f: (i, 0)),
                  pl.BlockSpec((d, tf), lambda i, f: (0, f)),
                  pl.BlockSpec((d, tf), lambda i, f: (0, f)),
                  pl.BlockSpec((tf, d), lambda i, f: (f, 0))],
        out_specs=pl.BlockSpec((tm, d), lambda i, f: (i, 0)),
        out_shape=jax.ShapeDtypeStruct((t, d), F32),
        scratch_shapes=[pltpu.VMEM((tm, d), F32)],
        compiler_params=_cparams(("parallel", "arbitrary")),
        name="ffn_dense",
    )(x, h, wg, wu, wd)


def _top2_route(logits):
    lane_i = lax.broadcasted_iota(jnp.int32, logits.shape, 1)
    lane = lane_i.astype(F32)
    neg = jnp.float32(-jnp.inf)
    z = jnp.where(lane_i < N_EXPERTS, logits, neg)
    m1 = jnp.max(z, axis=-1, keepdims=True)
    i1 = jnp.min(jnp.where(z == m1, lane, float(LANES)), axis=-1, keepdims=True)
    z2 = jnp.where(lane == i1, neg, z)
    m2 = jnp.max(z2, axis=-1, keepdims=True)
    i2 = jnp.min(jnp.where(z2 == m2, lane, float(LANES)), axis=-1, keepdims=True)
    e2 = jnp.exp(m2 - m1)
    w1 = 1.0 / (1.0 + e2)
    w2 = e2 / (1.0 + e2)
    return jnp.where(lane_i == 0, i1, jnp.where(lane_i == 1, i2,
                     jnp.where(lane_i == 2, w1, jnp.where(lane_i == 3, w2, 0.0))))


def _route_plan(route, tm):
    t = route.shape[0]
    e = route[:, 0:TOP_K].astype(jnp.int32)
    sel = (e[:, :, None] == jnp.arange(N_EXPERTS, dtype=jnp.int32)).astype(jnp.int32).sum(axis=1)
    csum = jnp.cumsum(sel, axis=0)
    padded = (csum[-1] + tm - 1) // tm * tm
    end = jnp.cumsum(padded)
    pos = (end - padded)[e] + jnp.take_along_axis(csum - sel, e, axis=1)
    n_rows = TOP_K * t + N_EXPERTS * tm
    flat = pos.reshape(-1)
    tok = jnp.zeros((n_rows,), jnp.int32).at[flat].set(
        jnp.repeat(jnp.arange(t, dtype=jnp.int32), TOP_K), unique_indices=True)
    n_tiles = n_rows // tm
    row0 = jnp.arange(n_tiles, dtype=jnp.int32) * tm
    valid = (row0 < end[-1]).astype(jnp.int32)
    tile_e = jnp.searchsorted(end, jnp.minimum(row0, end[-1] - 1), side="right").astype(jnp.int32)
    return tok.reshape(n_tiles, 1, tm), tile_e, valid, pos


def _moe_gmm_kernel(tile_e_ref, valid_ref, tok_ref, tok_next_ref, h_hbm, wg_ref, wu_ref, wd_ref,
                    o_ref, xg_ref, xb_ref, acc_ref, sem, *, nf, gather_steps):
    del tile_e_ref
    i = pl.program_id(0)
    f = pl.program_id(1)
    tm = xb_ref.shape[0]
    slot = i % 2

    def row_copy(tok, r, s):
        return pltpu.make_async_copy(h_hbm.at[pl.ds(tok[0, 0, r], 1), :],
                                     xg_ref.at[s, pl.ds(r, 1), :], sem.at[s])

    @pl.when((i == 0) & (f == 0))
    def _():
        def body(r, c):
            row_copy(tok_ref, r, 0).start()
            return c
        lax.fori_loop(0, tm, body, 0, unroll=8)

    valid = valid_ref[i] != 0
    prev_valid = valid_ref[jnp.maximum(i - 1, 0)] != 0

    @pl.when((f == 0) & ((i == 0) | prev_valid))
    def _():
        pltpu.make_async_copy(h_hbm.at[pl.ds(0, tm), :], xg_ref.at[slot], sem.at[slot]).wait()
        xb_ref[...] = xg_ref[slot].astype(BF16)

    def step(request_rows):
        if request_rows:
            rows = tm // gather_steps
            for r in range(rows):
                row_copy(tok_next_ref, f * rows + r, 1 - slot).start()
        y = jnp.dot(_swiglu_act(xb_ref[...], wg_ref[0], wu_ref[0]), wd_ref[0],
                    preferred_element_type=F32)

        @pl.when(f == 0)
        def _():
            acc_ref[...] = y

        @pl.when((f > 0) & (f < nf - 1))
        def _():
            acc_ref[...] += y

        @pl.when(f == nf - 1)
        def _():
            o_ref[...] = acc_ref[...] + y

    pl.when(valid & (f < gather_steps))(functools.partial(step, True))
    pl.when(valid & (f >= gather_steps))(functools.partial(step, False))

    @pl.when(jnp.logical_not(valid) & (f == nf - 1))
    def _():
        o_ref[...] = jnp.zeros_like(o_ref)


def _moe_gmm(tok, tile_e, valid, h, wg, wu, wd, *, tf=896, gather_steps=4):
    n_tiles, _, tm = tok.shape
    d = h.shape[1]
    ff = wg.shape[2]
    nf = ff // tf
    assert 2 <= gather_steps <= nf and tm % gather_steps == 0
    tok_spec = lambda imap: pl.BlockSpec((1, 1, tm), imap, memory_space=pltpu.SMEM)
    grid_spec = pltpu.PrefetchScalarGridSpec(
        num_scalar_prefetch=2,
        grid=(n_tiles, nf),
        in_specs=[tok_spec(lambda i, f, te, va: (i, 0, 0)),
                  tok_spec(lambda i, f, te, va: (jnp.minimum(i + 1, n_tiles - 1), 0, 0)),
                  pl.BlockSpec(memory_space=pl.ANY),
                  pl.BlockSpec((1, d, tf), lambda i, f, te, va: (te[i], 0, f)),
                  pl.BlockSpec((1, d, tf), lambda i, f, te, va: (te[i], 0, f)),
                  pl.BlockSpec((1, tf, d), lambda i, f, te, va: (te[i], f, 0))],
        out_specs=pl.BlockSpec((tm, d), lambda i, f, te, va: (i, 0)),
        scratch_shapes=[pltpu.VMEM((2, tm, d), F32), pltpu.VMEM((tm, d), BF16), pltpu.VMEM((tm, d), F32),
                        pltpu.SemaphoreType.DMA((2,))],
    )
    return pl.pallas_call(
        functools.partial(_moe_gmm_kernel, nf=nf, gather_steps=gather_steps),
        grid_spec=grid_spec,
        out_shape=jax.ShapeDtypeStruct((n_tiles * tm, d), F32),
        compiler_params=_cparams(("arbitrary", "arbitrary")),
        name="moe_gmm",
    )(tile_e, valid, tok, tok, h, wg, wu, wd)


def _moe_combine_kernel(pos_ref, pos_next_ref, x_ref, route_ref, y_hbm, o_ref, buf_ref, sem):
    i = pl.program_id(0)
    tc = x_ref.shape[0]
    slot = i % 2

    def gather_rows(pos, s):
        def body(r, c):
            for k in range(TOP_K):
                pltpu.make_async_copy(y_hbm.at[pl.ds(pos[0, 0, k * tc + r], 1), :],
                                      buf_ref.at[s, pl.ds(k * tc + r, 1), :], sem.at[s]).start()
            return c
        lax.fori_loop(0, tc, body, 0, unroll=4)

    @pl.when(i == 0)
    def _():
        gather_rows(pos_ref, 0)

    @pl.when(i + 1 < pl.num_programs(0))
    def _():
        gather_rows(pos_next_ref, 1 - slot)

    pltpu.make_async_copy(y_hbm.at[pl.ds(0, TOP_K * tc), :], buf_ref.at[slot], sem.at[slot]).wait()
    y = buf_ref[slot]
    w = route_ref[...]
    o_ref[...] = (x_ref[...] + w[:, TOP_K:TOP_K + 1] * y[0:tc]
                  + w[:, TOP_K + 1:TOP_K + 2] * y[tc:2 * tc])


def _moe_combine(x, route, y_rows, pos, *, tc=512):
    t, d = x.shape
    n = t // tc
    pos_t = pos.reshape(n, tc, TOP_K).transpose(0, 2, 1).reshape(n, 1, TOP_K * tc)
    pos_spec = lambda imap: pl.BlockSpec((1, 1, TOP_K * tc), imap, memory_space=pltpu.SMEM)
    return pl.pallas_call(
        _moe_combine_kernel,
        grid=(n,),
        in_specs=[pos_spec(lambda i: (i, 0, 0)),
                  pos_spec(lambda i: (jnp.minimum(i + 1, n - 1), 0, 0)),
                  pl.BlockSpec((tc, d), lambda i: (i, 0)),
                  pl.BlockSpec((tc, LANES), lambda i: (i, 0)),
                  pl.BlockSpec(memory_space=pl.ANY)],
        out_specs=pl.BlockSpec((tc, d), lambda i: (i, 0)),
        out_shape=jax.ShapeDtypeStruct((t, d), F32),
        scratch_shapes=[pltpu.VMEM((2, TOP_K * tc, d), F32), pltpu.SemaphoreType.DMA((2,))],
        compiler_params=_cparams(("arbitrary",)),
        name="moe_combine",
    )(pos_t, pos_t, x, route, y_rows)


def _row(v, reps=1):
    return jnp.tile(v.astype(F32), reps).reshape(1, -1)


def kernel(x, positions, attn_norm_g, w_in, w_gate, mla_q_norm_g, mla_kv_norm_g, mla_w_uq, mla_w_ukv, mla_qn_nope_g, mla_qn_rope_g, mla_kn_nope_g, mla_kn_rope_g, diff_qn_g, diff_kn_g, diff_lam_q1, diff_lam_k1, diff_lam_q2, diff_lam_k2, diff_subln_g, w_branch_a, w_branch_b, w_out, ffn_norm_g, dense_w_gate, dense_w_up, dense_w_down, moe_w_router, moe_w_gate, moe_w_up, moe_w_down):
    batch, seq, d = x.shape
    t = batch * seq
    xf = x.reshape(t, d)
    pos_f = positions.astype(F32)
    cos, sin = _rope_tables(pos_f.reshape(t, 1))
    pos_col = jnp.broadcast_to(pos_f.reshape(batch, seq, 1), (batch, seq, LANES))
    pos_row = pos_f.reshape(batch, 1, seq)
    slopes = 2.0 ** (-8.0 * jnp.arange(1, DIFF_HEADS + 1, dtype=F32) / DIFF_HEADS)
    n_in_head = MLA_Q_LORA + MLA_KV_LORA + MLA_ROPE

    for layer in range(DEPTH):
        w1 = jnp.concatenate(
            [w_in[layer][:, :n_in_head], jnp.zeros((d, 1024 - n_in_head), F32),
             w_in[layer][:, n_in_head:], w_gate[layer]], axis=1).astype(BF16)
        wuq = mla_w_uq[layer].reshape(MLA_Q_LORA, MLA_HEADS, MLA_NOPE + MLA_ROPE)
        wuq = jnp.concatenate([wuq[:, :, :MLA_NOPE].reshape(MLA_Q_LORA, -1),
                               wuq[:, :, MLA_NOPE:].reshape(MLA_Q_LORA, -1)], axis=1).astype(BF16)
        wukv = mla_w_ukv[layer].reshape(MLA_KV_LORA, MLA_HEADS, MLA_NOPE + MLA_V)
        wukv = jnp.concatenate([wukv[:, :, :MLA_NOPE].reshape(MLA_KV_LORA, -1),
                                wukv[:, :, MLA_NOPE:].reshape(MLA_KV_LORA, -1)], axis=1).astype(BF16)
        gkr = jnp.concatenate([mla_kn_rope_g[layer].astype(F32), jnp.zeros((LANES - MLA_ROPE,), F32)]).reshape(1, -1)

        p = _norm_proj(xf, _row(attn_norm_g[layer]), w1)
        q_bound = MLA_Q_SCALE * jnp.sqrt(_norm_bound(MLA_NOPE, mla_qn_nope_g[layer]) ** 2
                                         + _norm_bound(MLA_ROPE, mla_qn_rope_g[layer]) ** 2)
        k_bound = jnp.sqrt(_norm_bound(MLA_NOPE, mla_kn_nope_g[layer]) ** 2
                           + _norm_bound(MLA_ROPE, mla_kn_rope_g[layer]) ** 2)
        shift, use_bounded = _score_shift(q_bound, k_bound)
        shift_row = jnp.where(jnp.arange(LANES) % 64 == 0, -shift, 0.0).astype(F32).reshape(1, LANES)
        q_m, k_m, v_m = _mla_prep(
            p, cos, sin, _row(mla_q_norm_g[layer]), _row(mla_kv_norm_g[layer]), wuq, wukv,
            _row(mla_qn_nope_g[layer]), _row(mla_qn_rope_g[layer], 2), _row(mla_kn_nope_g[layer]), gkr,
            shift_row, batch=batch, seq=seq)
        o_a = _mla_attn(use_bounded.astype(jnp.int32).reshape(1), q_m, k_m, v_m).reshape(t, -1)

        lambda_init = 0.8 - 0.6 * math.exp(-0.3 * layer)
        qt_d, kn, vt_d = _diff_prep(p, _row(diff_qn_g[layer], 2), _row(diff_kn_g[layer], 2),
                                  batch=batch, seq=seq)
        lam4 = jnp.stack([diff_lam_q1[layer], diff_lam_k1[layer],
                          diff_lam_q2[layer], diff_lam_k2[layer]]).astype(F32)
        d_shift, d_bounded = _score_shift(DIFF_Q_SCALE * _norm_bound(DIFF_HEAD_DIM, diff_qn_g[layer]),
                                          _norm_bound(DIFF_HEAD_DIM, diff_kn_g[layer]))
        scal = jnp.concatenate([slopes, d_shift.reshape(1), d_bounded.astype(F32).reshape(1)])
        o_b = _diff_attn(scal, qt_d, kn.reshape(batch, seq, -1), vt_d,
                         pos_row, pos_col, lam4, _row(diff_subln_g[layer]),
                         lambda_init=lambda_init).reshape(t, -1)

        j = layer // 2
        wa, wb, wo = (w.astype(BF16) for w in (w_branch_a[layer], w_branch_b[layer], w_out[layer]))
        if layer % 2 == 0:
            xf, h = _merge(xf, o_a, o_b, p, wa, wb, wo, _row(ffn_norm_g[layer]))
            xf = _ffn(xf, h, dense_w_gate[j].astype(BF16), dense_w_up[j].astype(BF16),
                      dense_w_down[j].astype(BF16))
        else:
            wr = jnp.pad(moe_w_router[j].astype(F32), ((0, 0), (0, LANES - N_EXPERTS)))
            wr_hi = wr.astype(BF16)
            wr_lo = (wr - wr_hi.astype(F32)).astype(BF16)
            xf, h, route = _merge(xf, o_a, o_b, p, wa, wb, wo, _row(ffn_norm_g[layer]),
                                  router=(wr_hi, wr_lo))
            tok, tile_e, valid, pos = _route_plan(route, MOE_TILE_ROWS)
            y_rows = _moe_gmm(tok, tile_e, valid, h, moe_w_gate[j].astype(BF16),
                              moe_w_up[j].astype(BF16), moe_w_down[j].astype(BF16))
            xf = _moe_combine(xf, route, y_rows, pos)
    return xf.reshape(batch, seq, d)
```

```python
import functools
import math

import jax
import jax.numpy as jnp
from jax import lax
from jax.experimental import pallas as pl
from jax.experimental.pallas import tpu as pltpu

D_MODEL = 1024
DEPTH = 2
MLA_HEADS = 8
MLA_Q_LORA = 512
MLA_KV_LORA = 256
MLA_NOPE = 128
MLA_ROPE = 64
MLA_V = 128
ROPE_THETA = 10000.0
DIFF_HEADS = 8
DIFF_HEAD_DIM = 64
D_FF_DENSE = 2816
N_EXPERTS = 8
TOP_K = 2
D_FF_EXPERT = 3584
EPS = 1e-6

LANES = 128
LOG2E = 1.4426950408889634
MLA_Q_SCALE = (MLA_NOPE + MLA_ROPE) ** -0.5 * LOG2E
DIFF_Q_SCALE = DIFF_HEAD_DIM ** -0.5 * LOG2E
NEG_BIG = -1e30
BF16 = jnp.bfloat16
F32 = jnp.float32

P_WIDTH = 6144
P_DQ_BLK = 1
P_DK_BLK = 2
P_DV_BLK = 3
P_GATE_COL0 = 4096
MOE_TILE_ROWS = 512
VMEM_LIMIT = 52 * 1024 * 1024


def _cparams(sem):
    return pltpu.CompilerParams(dimension_semantics=sem, vmem_limit_bytes=VMEM_LIMIT)


def _rms(x, gain):
    ms = jnp.mean(x * x, axis=-1, keepdims=True)
    return x * lax.rsqrt(ms + EPS) * gain


def _ms_half_lanes(xb):
    lane = lax.broadcasted_iota(jnp.int32, xb.shape, 1)
    lo = lane < 64
    sq = xb * xb
    s_lo = jnp.sum(jnp.where(lo, sq, 0.0), axis=-1, keepdims=True)
    s_hi = jnp.sum(jnp.where(lo, 0.0, sq), axis=-1, keepdims=True)
    return jnp.where(lo, s_lo, s_hi) * (1.0 / 64.0)


def _norm_proj_kernel(x_ref, g_ref, w_ref, o_ref, h_ref, *, gate_tile0):
    j = pl.program_id(1)

    @pl.when(j == 0)
    def _():
        h_ref[...] = _rms(x_ref[...], g_ref[...]).astype(BF16)

    y = jnp.dot(h_ref[...], w_ref[...], preferred_element_type=F32)

    @pl.when(j < gate_tile0)
    def _():
        o_ref[...] = y.astype(BF16)

    @pl.when(j >= gate_tile0)
    def _():
        o_ref[...] = jax.nn.sigmoid(y).astype(BF16)


def _norm_proj(x, gain, w, *, tm=1024, tn=2048):
    t, d = x.shape
    n = w.shape[1]
    return pl.pallas_call(
        functools.partial(_norm_proj_kernel, gate_tile0=P_GATE_COL0 // tn),
        grid=(t // tm, n // tn),
        in_specs=[
            pl.BlockSpec((tm, d), lambda i, j: (i, 0)),
            pl.BlockSpec((1, d), lambda i, j: (0, 0)),
            pl.BlockSpec((d, tn), lambda i, j: (0, j)),
        ],
        out_specs=pl.BlockSpec((tm, tn), lambda i, j: (i, j)),
        out_shape=jax.ShapeDtypeStruct((t, n), BF16),
        scratch_shapes=[pltpu.VMEM((tm, d), BF16)],
        compiler_params=_cparams(("parallel", "arbitrary")),
        name="norm_proj",
    )(x, gain, w)


def _rope_table_kernel(pos_ref, freq_ref, cos_ref, sin_ref):
    ang = pos_ref[...] * freq_ref[...]
    lane = lax.broadcasted_iota(jnp.int32, ang.shape, 1)
    first_half = (lane % 64) < 32
    cos_ref[...] = jnp.cos(ang)
    sin_ref[...] = jnp.where(first_half, -1.0, 1.0) * jnp.sin(ang)


def _rope_tables(pos_col, *, tm=2048):
    t = pos_col.shape[0]
    half = MLA_ROPE // 2
    inv_freq = ROPE_THETA ** (-jnp.arange(half, dtype=F32) / half)
    freq = jnp.tile(inv_freq, LANES // half).reshape(1, LANES)
    return pl.pallas_call(
        _rope_table_kernel,
        grid=(t // tm,),
        in_specs=[pl.BlockSpec((tm, 1), lambda i: (i, 0)),
                  pl.BlockSpec((1, LANES), lambda i: (0, 0))],
        out_specs=[pl.BlockSpec((tm, LANES), lambda i: (i, 0))] * 2,
        out_shape=[jax.ShapeDtypeStruct((t, LANES), F32)] * 2,
        compiler_params=_cparams(("parallel",)),
        name="rope_tables",
    )(pos_col, freq)


def _rope(n, cos, sin_signed):
    w = n.shape[-1]
    lane = lax.broadcasted_iota(jnp.int32, n.shape, 1)
    first_half = (lane % 64) < 32
    swapped = jnp.where(first_half, pltpu.roll(n, w - 32, 1), pltpu.roll(n, 32, 1))
    return n * cos + swapped * sin_signed


def _mla_prep_kernel(p_ref, cos_ref, sin_ref, gq_ref, gkv_ref, wuq_ref, wukv_ref,
                     gqn_ref, gqr_ref, gkn_ref, gkr_ref, shift_ref, q_ref, k_ref, v_ref, *, q_scale):
    cq = p_ref[:, 0:MLA_Q_LORA].astype(F32)
    ckv = p_ref[:, MLA_Q_LORA:MLA_Q_LORA + MLA_KV_LORA].astype(F32)
    kr = p_ref[:, 768:896].astype(F32)
    cos = cos_ref[...]
    sin = sin_ref[...]

    q = jnp.dot(_rms(cq, gq_ref[...]).astype(BF16), wuq_ref[...], preferred_element_type=F32)
    kv = jnp.dot(_rms(ckv, gkv_ref[...]).astype(BF16), wukv_ref[...], preferred_element_type=F32)

    n_nope = MLA_HEADS * MLA_NOPE
    kr_ms = jnp.sum(kr * kr, axis=-1, keepdims=True) * (1.0 / MLA_ROPE)
    kr_n = kr * lax.rsqrt(kr_ms + EPS) * gkr_ref[...]
    kr_r = _rope(kr_n, cos, sin)
    kr_dup = kr_r + pltpu.roll(kr_r, 64, 1)

    lane = lax.broadcasted_iota(jnp.int32, (q.shape[0], LANES), 1)
    one_hot = jnp.where(lane % 64 == 0, 1.0, 0.0)
    for pair in range(MLA_HEADS // 2):
        qr = q[:, n_nope + pair * LANES:n_nope + (pair + 1) * LANES]
        qr_n = qr * lax.rsqrt(_ms_half_lanes(qr) + EPS) * gqr_ref[...]
        qr_r = _rope(qr_n, cos, sin) * q_scale
        for sub in range(2):
            h = 2 * pair + sub
            keep = (lane < 64) if sub == 0 else (lane >= 64)
            qn = _rms(q[:, h * MLA_NOPE:(h + 1) * MLA_NOPE], gqn_ref[...]) * q_scale
            q_ref[0, h, 0:LANES, :] = qn.T.astype(BF16)
            q_ref[0, h, LANES:2 * LANES, :] = jnp.where(keep, qr_r, shift_ref[...]).T.astype(BF16)
            kn = _rms(kv[:, h * MLA_NOPE:(h + 1) * MLA_NOPE], gkn_ref[...])
            k_ref[0, h, :, 0:LANES] = kn.astype(BF16)
            k_ref[0, h, :, LANES:2 * LANES] = jnp.where(keep, kr_dup, one_hot).astype(BF16)
            v_ref[0, h, :, :] = kv[:, n_nope + h * MLA_V:n_nope + (h + 1) * MLA_V].T.astype(BF16)


def _mla_prep(p, cos, sin, gq, gkv, wuq, wukv, gqn, gqr, gkn, gkr, shift_row, *, batch, seq, tm=512):
    nb = seq // tm
    full = lambda a: pl.BlockSpec(a.shape, lambda b, i: (0,) * a.ndim)
    row = lambda b, i: (b * nb + i, 0)
    hshape = lambda w: jax.ShapeDtypeStruct((batch, MLA_HEADS, seq, w), BF16)
    hspec = lambda w: pl.BlockSpec((1, MLA_HEADS, tm, w), lambda b, i: (b, 0, i, 0))
    tshape = lambda w: jax.ShapeDtypeStruct((batch, MLA_HEADS, w, seq), BF16)
    tspec = lambda w: pl.BlockSpec((1, MLA_HEADS, w, tm), lambda b, i: (b, 0, 0, i))
    return pl.pallas_call(
        functools.partial(_mla_prep_kernel, q_scale=MLA_Q_SCALE),
        grid=(batch, nb),
        in_specs=[pl.BlockSpec((tm, 1024), row),
                  pl.BlockSpec((tm, LANES), row), pl.BlockSpec((tm, LANES), row),
                  full(gq), full(gkv), full(wuq), full(wukv),
                  full(gqn), full(gqr), full(gkn), full(gkr), full(shift_row)],
        out_specs=[tspec(2 * LANES), hspec(2 * LANES), tspec(MLA_V)],
        out_shape=[tshape(2 * LANES), hshape(2 * LANES), tshape(MLA_V)],
        compiler_params=_cparams(("parallel", "parallel")),
        name="mla_prep",
    )(p, cos, sin, gq, gkv, wuq, wukv, gqn, gqr, gkn, gkr, shift_row)


BOUND_MARGIN = 1.02
MAX_SHIFT_SPAN = 100.0


def _norm_bound(n, gain):
    return math.sqrt(n) * jnp.max(jnp.abs(gain.astype(F32)))


def _score_shift(q_bound, k_bound):
    shift = (BOUND_MARGIN * q_bound * k_bound).astype(BF16).astype(F32)
    return shift, (2.0 * shift <= MAX_SHIFT_SPAN)


def _nt_dot(a, b):
    return lax.dot_general(a, b, (((1,), (1,)), ((), ())), preferred_element_type=F32)


def _lane_partial_sum(p):
    acc = p[:, 0:LANES]
    for c in range(1, p.shape[1] // LANES):
        acc = acc + p[:, c * LANES:(c + 1) * LANES]
    return acc


def _online_update(s, m, l):
    m_new = jnp.maximum(m, jnp.max(s, axis=-1, keepdims=True))
    alpha = jnp.exp2(m - m_new)
    p = jnp.exp2(s - m_new)
    l_new = alpha * l + jnp.sum(p, axis=-1, keepdims=True)
    return p, alpha, m_new, l_new


def _sublane_partial_sum(p):
    return jnp.sum(p.reshape(p.shape[0] // 8, 8, p.shape[1]), axis=0)


def _online_update_t(s, m, l):
    m_new = jnp.maximum(m, jnp.max(s, axis=0, keepdims=True))
    alpha = jnp.exp2(m - m_new)
    p = jnp.exp2(s - m_new)
    return p, alpha, m_new, alpha * l + jnp.sum(p, axis=0, keepdims=True)


def _mla_attn_kernel(flag_ref, qt_ref, k_ref, vt_ref, o_ref, *, tk, nk, unroll):
    tq = qt_ref.shape[3]
    qt = qt_ref[0, 0]

    def scores(k):
        return jnp.dot(k, qt, preferred_element_type=F32)

    def chunk(j):
        off = pl.multiple_of(j * tk, tk)
        return k_ref[0, 0, pl.ds(off, tk), :], vt_ref[0, 0, :, pl.ds(off, tk)]

    def bounded():
        def body(j, carry):
            l, acc = carry
            k, vt = chunk(j)
            p = jnp.exp2(scores(k))
            return (l + _sublane_partial_sum(p),
                    acc + jnp.dot(vt, p.astype(BF16), preferred_element_type=F32))

        init = (jnp.zeros((8, tq), F32), jnp.zeros((MLA_V, tq), F32))
        l, acc = lax.fori_loop(0, nk, body, init, unroll=unroll)
        return acc / jnp.sum(l, axis=0, keepdims=True)

    def online():
        def body(j, carry):
            m, l, acc = carry
            k, vt = chunk(j)
            p, alpha, m, l = _online_update_t(scores(k), m, l)
            return m, l, alpha * acc + jnp.dot(vt, p.astype(BF16), preferred_element_type=F32)

        init = (jnp.full((1, tq), NEG_BIG, F32), jnp.zeros((1, tq), F32), jnp.zeros((MLA_V, tq), F32))
        _, l, acc = lax.fori_loop(0, nk, body, init)
        return acc / l

    o_ref[0] = lax.cond(flag_ref[0] != 0, bounded, online).T.astype(BF16)


def _mla_attn(flag, qt, k, vt, *, tq=1024, tk=512, unroll=8):
    b, h, s, w = k.shape
    return pl.pallas_call(
        functools.partial(_mla_attn_kernel, tk=tk, nk=s // tk, unroll=unroll),
        grid=(b, h, s // tq),
        in_specs=[pl.BlockSpec(memory_space=pltpu.SMEM),
                  pl.BlockSpec((1, 1, w, tq), lambda bi, hi, qi: (bi, hi, 0, qi)),
                  pl.BlockSpec((1, 1, s, w), lambda bi, hi, qi: (bi, hi, 0, 0)),
                  pl.BlockSpec((1, 1, MLA_V, s), lambda bi, hi, qi: (bi, hi, 0, 0))],
        out_specs=pl.BlockSpec((1, tq, MLA_V), lambda bi, hi, qi: (bi, qi, hi)),
        out_shape=jax.ShapeDtypeStruct((b, s, h * MLA_V), BF16),
        compiler_params=_cparams(("parallel", "parallel", "arbitrary")),
        name="mla_attn",
    )(flag, qt, k, vt)


def _diff_prep_kernel(q_ref, k_ref, v_ref, gq_ref, gk_ref, qt_ref, kn_ref, vt_ref, *, q_scale):
    for c in range(DIFF_HEADS):
        sl = slice(c * LANES, (c + 1) * LANES)
        q = q_ref[:, sl].astype(F32)
        k = k_ref[:, sl].astype(F32)
        qt_ref[0, c] = (q * lax.rsqrt(_ms_half_lanes(q) + EPS) * (gq_ref[...] * q_scale)).T.astype(BF16)
        kn_ref[:, sl] = (k * lax.rsqrt(_ms_half_lanes(k) + EPS) * gk_ref[...]).astype(BF16)
        vt_ref[0, c] = v_ref[:, sl].astype(F32).T.astype(BF16)


def _diff_prep(p, gq, gk, *, batch, seq, tm=1024):
    t = p.shape[0]
    nb = seq // tm
    w = DIFF_HEADS * 2 * DIFF_HEAD_DIM
    row = lambda blk: (lambda b, i: (b * nb + i, blk))
    gain = pl.BlockSpec((1, LANES), lambda b, i: (0, 0))
    tshape = jax.ShapeDtypeStruct((batch, DIFF_HEADS, LANES, seq), BF16)
    tspec = pl.BlockSpec((1, DIFF_HEADS, LANES, tm), lambda b, i: (b, 0, 0, i))
    return pl.pallas_call(
        functools.partial(_diff_prep_kernel, q_scale=DIFF_Q_SCALE),
        grid=(batch, nb),
        in_specs=[pl.BlockSpec((tm, w), row(P_DQ_BLK)), pl.BlockSpec((tm, w), row(P_DK_BLK)),
                  pl.BlockSpec((tm, w), row(P_DV_BLK)), gain, gain],
        out_specs=[tspec, pl.BlockSpec((tm, w), row(0)), tspec],
        out_shape=[tshape, jax.ShapeDtypeStruct((t, w), BF16), tshape],
        compiler_params=_cparams(("parallel", "parallel")),
        name="diff_prep",
    )(p, p, p, gq, gk)


def _diff_attn_kernel(scal_ref, qt_ref, k_ref, vt_ref, pq_ref, pk_ref, lam_ref, g_ref, o_ref,
                      *, tk, nk, unroll, lambda_init):
    qt = qt_ref[0, 0]
    tq = qt.shape[1]
    row = lax.broadcasted_iota(jnp.int32, qt.shape, 0)
    zero = jnp.zeros_like(qt)
    q0t = jnp.where(row < DIFF_HEAD_DIM, qt, zero)
    q1t = jnp.where(row < DIFF_HEAD_DIM, zero, qt)
    pq = pq_ref[0]
    slope = scal_ref[pl.program_id(1)] * LOG2E
    shift = scal_ref[DIFF_HEADS]
    use_bounded = scal_ref[DIFF_HEADS + 1]

    spq = slope * pq

    def chunk(j, shifted):
        off = pl.multiple_of(j * tk, tk)
        k = k_ref[0, pl.ds(off, tk), :]
        vt = vt_ref[0, 0, :, pl.ds(off, tk)]
        spk = slope * pk_ref[0, pl.ds(off, tk), :]
        bias = jnp.abs(jnp.concatenate([spk] * (tq // LANES), axis=1) - spq)
        return k, vt, (bias + shift if shifted else bias)

    def scores(k, qt):
        return jnp.dot(k, qt, preferred_element_type=F32)

    def pv(vt, p):
        return jnp.dot(vt, p.astype(BF16), preferred_element_type=F32)

    def bounded():
        def body(j, carry):
            l0, a0, l1, a1 = carry
            k, vt, bias = chunk(j, False)
            p0 = jnp.exp2(scores(k, q0t) - bias)
            p1 = jnp.exp2(scores(k, q1t) - bias)
            return (l0 + _sublane_partial_sum(p0), a0 + pv(vt, p0),
                    l1 + _sublane_partial_sum(p1), a1 + pv(vt, p1))

        zl = jnp.zeros((8, tq), F32)
        za = jnp.zeros((LANES, tq), F32)
        l0, a0, l1, a1 = lax.fori_loop(0, nk, body, (zl, za, zl, za), unroll=unroll)
        return (a0 / jnp.sum(l0, axis=0, keepdims=True), a1 / jnp.sum(l1, axis=0, keepdims=True))

    def online():
        def body(j, carry):
            m0, l0, a0, m1, l1, a1 = carry
            k, vt, bias = chunk(j, True)
            p0, al0, m0, l0 = _online_update_t(scores(k, q0t) - bias, m0, l0)
            a0 = al0 * a0 + pv(vt, p0)
            p1, al1, m1, l1 = _online_update_t(scores(k, q1t) - bias, m1, l1)
            a1 = al1 * a1 + pv(vt, p1)
            return m0, l0, a0, m1, l1, a1

        neg = jnp.full((1, tq), NEG_BIG, F32)
        z1 = jnp.zeros((1, tq), F32)
        za = jnp.zeros((LANES, tq), F32)
        _, l0, a0, _, l1, a1 = lax.fori_loop(0, nk, body, (neg, z1, za, neg, z1, za))
        return a0 / l0, a1 / l1

    o0, o1 = lax.cond(use_bounded != 0.0, bounded, online)
    lam = (jnp.exp(jnp.sum(lam_ref[0:1, :] * lam_ref[1:2, :], axis=-1, keepdims=True))
           - jnp.exp(jnp.sum(lam_ref[2:3, :] * lam_ref[3:4, :], axis=-1, keepdims=True))
           + lambda_init)
    o = (o0 - lam * o1).T
    o_ref[0] = (_rms(o, g_ref[...]) * (1.0 - lambda_init)).astype(BF16)


def _diff_attn(scal, qt, kn, vt, pos_row, pos_col, lam4, subln_g, *, lambda_init, tq=512, tk=512, unroll=8):
    b, s, _ = kn.shape
    return pl.pallas_call(
        functools.partial(_diff_attn_kernel, tk=tk, nk=s // tk, unroll=unroll, lambda_init=lambda_init),
        grid=(b, DIFF_HEADS, s // tq),
        in_specs=[pl.BlockSpec(memory_space=pltpu.SMEM),
                  pl.BlockSpec((1, 1, LANES, tq), lambda bi, hi, qi: (bi, hi, 0, qi)),
                  pl.BlockSpec((1, s, LANES), lambda bi, hi, qi: (bi, 0, hi)),
                  pl.BlockSpec((1, 1, LANES, s), lambda bi, hi, qi: (bi, hi, 0, 0)),
                  pl.BlockSpec((1, 1, tq), lambda bi, hi, qi: (bi, 0, qi)),
                  pl.BlockSpec((1, s, LANES), lambda bi, hi, qi: (bi, 0, 0)),
                  pl.BlockSpec((4, DIFF_HEAD_DIM), lambda bi, hi, qi: (0, 0)),
                  pl.BlockSpec((1, LANES), lambda bi, hi, qi: (0, 0))],
        out_specs=pl.BlockSpec((1, tq, LANES), lambda bi, hi, qi: (bi, qi, hi)),
        out_shape=jax.ShapeDtypeStruct((b, s, DIFF_HEADS * LANES), BF16),
        compiler_params=_cparams(("parallel", "parallel", "arbitrary")),
        name="diff_attn",
    )(scal, qt, kn, vt, pos_row, pos_col, lam4, subln_g)


def _split_dot_f32(a, w_hi, w_lo):
    a_hi = a.astype(BF16)
    a_lo = (a - a_hi.astype(F32)).astype(BF16)
    return (jnp.dot(a_hi, w_hi, preferred_element_type=F32)
            + jnp.dot(a_lo, w_hi, preferred_element_type=F32)
            + jnp.dot(a_hi, w_lo, preferred_element_type=F32))


def _merge_kernel(x_ref, oa_ref, ob_ref, ga_ref, gb_ref, wa_ref, wb_ref, wo_ref, gf_ref, *rest,
                  with_router):
    if with_router:
        wr_hi_ref, wr_lo_ref, xo_ref, h_ref, route_ref = rest
    else:
        xo_ref, h_ref = rest
    ya = jnp.dot(oa_ref[...], wa_ref[...], preferred_element_type=F32)
    yb = jnp.dot(ob_ref[...], wb_ref[...], preferred_element_type=F32)
    merged = ga_ref[...].astype(F32) * ya + gb_ref[...].astype(F32) * yb
    x = x_ref[...] + jnp.dot(merged.astype(BF16), wo_ref[...], preferred_element_type=F32)
    xo_ref[...] = x
    h = _rms(x, gf_ref[...])
    h_ref[...] = h.astype(h_ref.dtype)
    if with_router:
        route_ref[...] = _top2_route(_split_dot_f32(h, wr_hi_ref[...], wr_lo_ref[...]))


def _merge(x, oa, ob, p, wa, wb, wo, gf, router=None, *, tm=512):
    t, d = x.shape
    ga_blk = P_GATE_COL0 // d
    row = lambda i: (i, 0)
    full = lambda a: pl.BlockSpec(a.shape, lambda i: (0,) * a.ndim)
    in_specs = [pl.BlockSpec((tm, d), row), pl.BlockSpec((tm, d), row), pl.BlockSpec((tm, d), row),
                pl.BlockSpec((tm, d), lambda i: (i, ga_blk)),
                pl.BlockSpec((tm, d), lambda i: (i, ga_blk + 1)),
                full(wa), full(wb), full(wo), full(gf)]
    args = [x, oa, ob, p, p, wa, wb, wo, gf]
    out_specs = [pl.BlockSpec((tm, d), row), pl.BlockSpec((tm, d), row)]
    h_dtype = BF16 if router is None else F32
    out_shape = [jax.ShapeDtypeStruct((t, d), F32), jax.ShapeDtypeStruct((t, d), h_dtype)]
    if router is not None:
        in_specs += [full(router[0]), full(router[1])]
        args += list(router)
        out_specs.append(pl.BlockSpec((tm, LANES), row))
        out_shape.append(jax.ShapeDtypeStruct((t, LANES), F32))
    return pl.pallas_call(
        functools.partial(_merge_kernel, with_router=router is not None),
        grid=(t // tm,),
        in_specs=in_specs,
        out_specs=out_specs,
        out_shape=out_shape,
        compiler_params=_cparams(("parallel",)),
        name="merge_router" if router is not None else "merge",
    )(*args)


def _swiglu_act(h, wg, wu):
    g = jnp.dot(h, wg, preferred_element_type=F32)
    u = jnp.dot(h, wu, preferred_element_type=F32)
    return (g * jax.nn.sigmoid(g) * u).astype(BF16)


def _ffn_kernel(x_ref, h_ref, wg_ref, wu_ref, wd_ref, o_ref, acc_ref):
    f = pl.program_id(1)
    y = jnp.dot(_swiglu_act(h_ref[...], wg_ref[...], wu_ref[...]), wd_ref[...],
                preferred_element_type=F32)

    @pl.when(f == 0)
    def _():
        acc_ref[...] = x_ref[...] + y

    @pl.when(f > 0)
    def _():
        acc_ref[...] += y

    @pl.when(f == pl.num_programs(1) - 1)
    def _():
        o_ref[...] = acc_ref[...]


def _ffn(x, h, wg, wu, wd, *, tm=1024, tf=1408):
    t, d = x.shape
    ff = wg.shape[1]
    return pl.pallas_call(
        _ffn_kernel,
        grid=(t // tm, ff // tf),
        in_specs=[pl.BlockSpec((tm, d), lambda i, f: (i, 0)),
                  pl.BlockSpec((tm, d), lambda i, f: (i, 0)),
                  pl.BlockSpec((d, tf), lambda i, f: (0, f)),
                  pl.BlockSpec((d, tf), lambda i, f: (0, f)),
                  pl.BlockSpec((tf, d), lambda i, f: (f, 0))],
        out_specs=pl.BlockSpec((tm, d), lambda i, f: (i, 0)),
        out_shape=jax.ShapeDtypeStruct((t, d), F32),
        scratch_shapes=[pltpu.VMEM((tm, d), F32)],
        compiler_params=_cparams(("parallel", "arbitrary")),
        name="ffn_dense",
    )(x, h, wg, wu, wd)


def _top2_route(logits):
    lane_i = lax.broadcasted_iota(jnp.int32, logits.shape, 1)
    lane = lane_i.astype(F32)
    neg = jnp.float32(-jnp.inf)
    z = jnp.where(lane_i < N_EXPERTS, logits, neg)
    m1 = jnp.max(z, axis=-1, keepdims=True)
    i1 = jnp.min(jnp.where(z == m1, lane, float(LANES)), axis=-1, keepdims=True)
    z2 = jnp.where(lane == i1, neg, z)
    m2 = jnp.max(z2, axis=-1, keepdims=True)
    i2 = jnp.min(jnp.where(z2 == m2, lane, float(LANES)), axis=-1, keepdims=True)
    e2 = jnp.exp(m2 - m1)
    w1 = 1.0 / (1.0 + e2)
    w2 = e2 / (1.0 + e2)
    return jnp.where(lane_i == 0, i1, jnp.where(lane_i == 1, i2,
                     jnp.where(lane_i == 2, w1, jnp.where(lane_i == 3, w2, 0.0))))


def _route_plan(route, tm):
    t = route.shape[0]
    e = route[:, 0:TOP_K].astype(jnp.int32)
    sel = (e[:, :, None] == jnp.arange(N_EXPERTS, dtype=jnp.int32)).astype(jnp.int32).sum(axis=1)
    csum = jnp.cumsum(sel, axis=0)
    padded = (csum[-1] + tm - 1) // tm * tm
    end = jnp.cumsum(padded)
    pos = (end - padded)[e] + jnp.take_along_axis(csum - sel, e, axis=1)
    n_rows = TOP_K * t + N_EXPERTS * tm
    flat = pos.reshape(-1)
    tok = jnp.zeros((n_rows,), jnp.int32).at[flat].set(
        jnp.repeat(jnp.arange(t, dtype=jnp.int32), TOP_K), unique_indices=True)
    n_tiles = n_rows // tm
    row0 = jnp.arange(n_tiles, dtype=jnp.int32) * tm
    valid = (row0 < end[-1]).astype(jnp.int32)
    tile_e = jnp.searchsorted(end, jnp.minimum(row0, end[-1] - 1), side="right").astype(jnp.int32)
    return tok.reshape(n_tiles, 1, tm), tile_e, valid, pos


def _moe_gmm_kernel(tile_e_ref, valid_ref, tok_ref, tok_next_ref, h_hbm, wg_ref, wu_ref, wd_ref,
                    o_ref, xg_ref, xb_ref, acc_ref, sem, *, nf, gather_steps):
    del tile_e_ref
    i = pl.program_id(0)
    f = pl.program_id(1)
    tm = xb_ref.shape[0]
    slot = i % 2

    def row_copy(tok, r, s):
        return pltpu.make_async_copy(h_hbm.at[pl.ds(tok[0, 0, r], 1), :],
                                     xg_ref.at[s, pl.ds(r, 1), :], sem.at[s])

    @pl.when((i == 0) & (f == 0))
    def _():
        def body(r, c):
            row_copy(tok_ref, r, 0).start()
            return c
        lax.fori_loop(0, tm, body, 0, unroll=8)

    valid = valid_ref[i] != 0
    prev_valid = valid_ref[jnp.maximum(i - 1, 0)] != 0

    @pl.when((f == 0) & ((i == 0) | prev_valid))
    def _():
        pltpu.make_async_copy(h_hbm.at[pl.ds(0, tm), :], xg_ref.at[slot], sem.at[slot]).wait()
        xb_ref[...] = xg_ref[slot].astype(BF16)

    def step(request_rows):
        if request_rows:
            rows = tm // gather_steps
            for r in range(rows):
                row_copy(tok_next_ref, f * rows + r, 1 - slot).start()
        y = jnp.dot(_swiglu_act(xb_ref[...], wg_ref[0], wu_ref[0]), wd_ref[0],
                    preferred_element_type=F32)

        @pl.when(f == 0)
        def _():
            acc_ref[...] = y

        @pl.when((f > 0) & (f < nf - 1))
        def _():
            acc_ref[...] += y

        @pl.when(f == nf - 1)
        def _():
            o_ref[...] = acc_ref[...] + y

    pl.when(valid & (f < gather_steps))(functools.partial(step, True))
    pl.when(valid & (f >= gather_steps))(functools.partial(step, False))

    @pl.when(jnp.logical_not(valid) & (f == nf - 1))
    def _():
        o_ref[...] = jnp.zeros_like(o_ref)


def _moe_gmm(tok, tile_e, valid, h, wg, wu, wd, *, tf=1792, gather_steps=2):
    n_tiles, _, tm = tok.shape
    d = h.shape[1]
    ff = wg.shape[2]
    nf = ff // tf
    assert 2 <= gather_steps <= nf and tm % gather_steps == 0
    tok_spec = lambda imap: pl.BlockSpec((1, 1, tm), imap, memory_space=pltpu.SMEM)
    grid_spec = pltpu.PrefetchScalarGridSpec(
        num_scalar_prefetch=2,
        grid=(n_tiles, nf),
        in_specs=[tok_spec(lambda i, f, te, va: (i, 0, 0)),
                  tok_spec(lambda i, f, te, va: (jnp.minimum(i + 1, n_tiles - 1), 0, 0)),
                  pl.BlockSpec(memory_space=pl.ANY),
                  pl.BlockSpec((1, d, tf), lambda i, f, te, va: (te[i], 0, f)),
                  pl.BlockSpec((1, d, tf), lambda i, f, te, va: (te[i], 0, f)),
                  pl.BlockSpec((1, tf, d), lambda i, f, te, va: (te[i], f, 0))],
        out_specs=pl.BlockSpec((tm, d), lambda i, f, te, va: (i, 0)),
        scratch_shapes=[pltpu.VMEM((2, tm, d), F32), pltpu.VMEM((tm, d), BF16), pltpu.VMEM((tm, d), F32),
                        pltpu.SemaphoreType.DMA((2,))],
    )
    return pl.pallas_call(
        functools.partial(_moe_gmm_kernel, nf=nf, gather_steps=gather_steps),
        grid_spec=grid_spec,
        out_shape=jax.ShapeDtypeStruct((n_tiles * tm, d), F32),
        compiler_params=_cparams(("arbitrary", "arbitrary")),
        name="moe_gmm",
    )(tile_e, valid, tok, tok, h, wg, wu, wd)


def _moe_combine_kernel(pos_ref, pos_next_ref, x_ref, route_ref, y_hbm, o_ref, buf_ref, sem):
    i = pl.program_id(0)
    tc = x_ref.shape[0]
    slot = i % 2

    def row_copy(pos, r, s):
        return pltpu.make_async_copy(y_hbm.at[pl.ds(pos[0, 0, r], 1), :],
                                     buf_ref.at[s, pl.ds(r, 1), :], sem.at[s])

    @pl.when(i == 0)
    def _():
        def body(r, c):
            row_copy(pos_ref, r, 0).start()
            return c
        lax.fori_loop(0, TOP_K * tc, body, 0, unroll=8)

    @pl.when(i + 1 < pl.num_programs(0))
    def _():
        for r in range(TOP_K * tc):
            row_copy(pos_next_ref, r, 1 - slot).start()

    pltpu.make_async_copy(y_hbm.at[pl.ds(0, TOP_K * tc), :], buf_ref.at[slot], sem.at[slot]).wait()
    y = buf_ref[slot]
    w = route_ref[...]
    o_ref[...] = (x_ref[...] + w[:, TOP_K:TOP_K + 1] * y[0:tc]
                  + w[:, TOP_K + 1:TOP_K + 2] * y[tc:2 * tc])


def _moe_combine(x, route, y_rows, pos, *, tc=512):
    t, d = x.shape
    n = t // tc
    pos_t = pos.reshape(n, tc, TOP_K).transpose(0, 2, 1).reshape(n, 1, TOP_K * tc)
    pos_spec = lambda imap: pl.BlockSpec((1, 1, TOP_K * tc), imap, memory_space=pltpu.SMEM)
    return pl.pallas_call(
        _moe_combine_kernel,
        grid=(n,),
        in_specs=[pos_spec(lambda i: (i, 0, 0)),
                  pos_spec(lambda i: (jnp.minimum(i + 1, n - 1), 0, 0)),
                  pl.BlockSpec((tc, d), lambda i: (i, 0)),
                  pl.BlockSpec((tc, LANES), lambda i: (i, 0)),
                  pl.BlockSpec(memory_space=pl.ANY)],
        out_specs=pl.BlockSpec((tc, d), lambda i: (i, 0)),
        out_shape=jax.ShapeDtypeStruct((t, d), F32),
        scratch_shapes=[pltpu.VMEM((2, TOP_K * tc, d), F32), pltpu.SemaphoreType.DMA((2,))],
        compiler_params=_cparams(("arbitrary",)),
        name="moe_combine",
    )(pos_t, pos_t, x, route, y_rows)


def _row(v, reps=1):
    return jnp.tile(v.astype(F32), reps).reshape(1, -1)


def kernel(x, positions, attn_norm_g, w_in, w_gate, mla_q_norm_g, mla_kv_norm_g, mla_w_uq, mla_w_ukv, mla_qn_nope_g, mla_qn_rope_g, mla_kn_nope_g, mla_kn_rope_g, diff_qn_g, diff_kn_g, diff_lam_q1, diff_lam_k1, diff_lam_q2, diff_lam_k2, diff_subln_g, w_branch_a, w_branch_b, w_out, ffn_norm_g, dense_w_gate, dense_w_up, dense_w_down, moe_w_router, moe_w_gate, moe_w_up, moe_w_down):
    batch, seq, d = x.shape
    t = batch * seq
    xf = x.reshape(t, d)
    pos_f = positions.astype(F32)
    cos, sin = _rope_tables(pos_f.reshape(t, 1))
    pos_col = jnp.broadcast_to(pos_f.reshape(batch, seq, 1), (batch, seq, LANES))
    pos_row = pos_f.reshape(batch, 1, seq)
    slopes = 2.0 ** (-8.0 * jnp.arange(1, DIFF_HEADS + 1, dtype=F32) / DIFF_HEADS)
    n_in_head = MLA_Q_LORA + MLA_KV_LORA + MLA_ROPE

    for layer in range(DEPTH):
        w1 = jnp.concatenate(
            [w_in[layer][:, :n_in_head], jnp.zeros((d, 1024 - n_in_head), F32),
             w_in[layer][:, n_in_head:], w_gate[layer]], axis=1).astype(BF16)
        wuq = mla_w_uq[layer].reshape(MLA_Q_LORA, MLA_HEADS, MLA_NOPE + MLA_ROPE)
        wuq = jnp.concatenate([wuq[:, :, :MLA_NOPE].reshape(MLA_Q_LORA, -1),
                               wuq[:, :, MLA_NOPE:].reshape(MLA_Q_LORA, -1)], axis=1).astype(BF16)
        wukv = mla_w_ukv[layer].reshape(MLA_KV_LORA, MLA_HEADS, MLA_NOPE + MLA_V)
        wukv = jnp.concatenate([wukv[:, :, :MLA_NOPE].reshape(MLA_KV_LORA, -1),
                                wukv[:, :, MLA_NOPE:].reshape(MLA_KV_LORA, -1)], axis=1).astype(BF16)
        gkr = jnp.concatenate([mla_kn_rope_g[layer].astype(F32), jnp.zeros((LANES - MLA_ROPE,), F32)]).reshape(1, -1)

        p = _norm_proj(xf, _row(attn_norm_g[layer]), w1)
        q_bound = MLA_Q_SCALE * jnp.sqrt(_norm_bound(MLA_NOPE, mla_qn_nope_g[layer]) ** 2
                                         + _norm_bound(MLA_ROPE, mla_qn_rope_g[layer]) ** 2)
        k_bound = jnp.sqrt(_norm_bound(MLA_NOPE, mla_kn_nope_g[layer]) ** 2
                           + _norm_bound(MLA_ROPE, mla_kn_rope_g[layer]) ** 2)
        shift, use_bounded = _score_shift(q_bound, k_bound)
        shift_row = jnp.where(jnp.arange(LANES) % 64 == 0, -shift, 0.0).astype(F32).reshape(1, LANES)
        q_m, k_m, v_m = _mla_prep(
            p, cos, sin, _row(mla_q_norm_g[layer]), _row(mla_kv_norm_g[layer]), wuq, wukv,
            _row(mla_qn_nope_g[layer]), _row(mla_qn_rope_g[layer], 2), _row(mla_kn_nope_g[layer]), gkr,
            shift_row, batch=batch, seq=seq)
        o_a = _mla_attn(use_bounded.astype(jnp.int32).reshape(1), q_m, k_m, v_m).reshape(t, -1)

        lambda_init = 0.8 - 0.6 * math.exp(-0.3 * layer)
        qt_d, kn, vt_d = _diff_prep(p, _row(diff_qn_g[layer], 2), _row(diff_kn_g[layer], 2),
                                  batch=batch, seq=seq)
        lam4 = jnp.stack([diff_lam_q1[layer], diff_lam_k1[layer],
                          diff_lam_q2[layer], diff_lam_k2[layer]]).astype(F32)
        d_shift, d_bounded = _score_shift(DIFF_Q_SCALE * _norm_bound(DIFF_HEAD_DIM, diff_qn_g[layer]),
                                          _norm_bound(DIFF_HEAD_DIM, diff_kn_g[layer]))
        scal = jnp.concatenate([slopes, d_shift.reshape(1), d_bounded.astype(F32).reshape(1)])
        o_b = _diff_attn(scal, qt_d, kn.reshape(batch, seq, -1), vt_d,
                         pos_row, pos_col, lam4, _row(diff_subln_g[layer]),
                         lambda_init=lambda_init).reshape(t, -1)

        j = layer // 2
        wa, wb, wo = (w.astype(BF16) for w in (w_branch_a[layer], w_branch_b[layer], w_out[layer]))
        if layer % 2 == 0:
            xf, h = _merge(xf, o_a, o_b, p, wa, wb, wo, _row(ffn_norm_g[layer]))
            xf = _ffn(xf, h, dense_w_gate[j].astype(BF16), dense_w_up[j].astype(BF16),
                      dense_w_down[j].astype(BF16))
        else:
            wr = jnp.pad(moe_w_router[j].astype(F32), ((0, 0), (0, LANES - N_EXPERTS)))
            wr_hi = wr.astype(BF16)
            wr_lo = (wr - wr_hi.astype(F32)).astype(BF16)
            xf, h, route = _merge(xf, o_a, o_b, p, wa, wb, wo, _row(ffn_norm_g[layer]),
                                  router=(wr_hi, wr_lo))
            tok, tile_e, valid, pos = _route_plan(route, MOE_TILE_ROWS)
            y_rows = _moe_gmm(tok, tile_e, valid, h, moe_w_gate[j].astype(BF16),
                              moe_w_up[j].astype(BF16), moe_w_down[j].astype(BF16))
            xf = _moe_combine(xf, route, y_rows, pos)
    return xf.reshape(batch, seq, d)
```

```python
import functools
import math

import jax
import jax.numpy as jnp
from jax import lax
from jax.experimental import pallas as pl
from jax.experimental.pallas import tpu as pltpu

D_MODEL = 1024
DEPTH = 2
MLA_HEADS = 8
MLA_Q_LORA = 512
MLA_KV_LORA = 256
MLA_NOPE = 128
MLA_ROPE = 64
MLA_V = 128
ROPE_THETA = 10000.0
DIFF_HEADS = 8
DIFF_HEAD_DIM = 64
D_FF_DENSE = 2816
N_EXPERTS = 8
TOP_K = 2
D_FF_EXPERT = 3584
EPS = 1e-6

LANES = 128
LOG2E = 1.4426950408889634
MLA_Q_SCALE = (MLA_NOPE + MLA_ROPE) ** -0.5 * LOG2E
DIFF_Q_SCALE = DIFF_HEAD_DIM ** -0.5 * LOG2E
NEG_BIG = -1e30
BF16 = jnp.bfloat16
F32 = jnp.float32

P_WIDTH = 6144
P_DQ_BLK = 1
P_DK_BLK = 2
P_DV_BLK = 3
P_GATE_COL0 = 4096
MOE_TILE_ROWS = 512
VMEM_LIMIT = 52 * 1024 * 1024


def _cparams(sem):
    return pltpu.CompilerParams(dimension_semantics=sem, vmem_limit_bytes=VMEM_LIMIT)


def _rms(x, gain):
    ms = jnp.mean(x * x, axis=-1, keepdims=True)
    return x * lax.rsqrt(ms + EPS) * gain


def _ms_half_lanes(xb):
    lane = lax.broadcasted_iota(jnp.int32, xb.shape, 1)
    lo = lane < 64
    sq = xb * xb
    s_lo = jnp.sum(jnp.where(lo, sq, 0.0), axis=-1, keepdims=True)
    s_hi = jnp.sum(jnp.where(lo, 0.0, sq), axis=-1, keepdims=True)
    return jnp.where(lo, s_lo, s_hi) * (1.0 / 64.0)


def _norm_proj_kernel(x_ref, g_ref, w_ref, o_ref, h_ref, *, gate_tile0):
    j = pl.program_id(1)

    @pl.when(j == 0)
    def _():
        h_ref[...] = _rms(x_ref[...], g_ref[...]).astype(BF16)

    y = jnp.dot(h_ref[...], w_ref[...], preferred_element_type=F32)

    @pl.when(j < gate_tile0)
    def _():
        o_ref[...] = y.astype(BF16)

    @pl.when(j >= gate_tile0)
    def _():
        o_ref[...] = jax.nn.sigmoid(y).astype(BF16)


def _norm_proj(x, gain, w, *, tm=1024, tn=2048):
    t, d = x.shape
    n = w.shape[1]
    return pl.pallas_call(
        functools.partial(_norm_proj_kernel, gate_tile0=P_GATE_COL0 // tn),
        grid=(t // tm, n // tn),
        in_specs=[
            pl.BlockSpec((tm, d), lambda i, j: (i, 0)),
            pl.BlockSpec((1, d), lambda i, j: (0, 0)),
            pl.BlockSpec((d, tn), lambda i, j: (0, j)),
        ],
        out_specs=pl.BlockSpec((tm, tn), lambda i, j: (i, j)),
        out_shape=jax.ShapeDtypeStruct((t, n), BF16),
        scratch_shapes=[pltpu.VMEM((tm, d), BF16)],
        compiler_params=_cparams(("parallel", "arbitrary")),
        name="norm_proj",
    )(x, gain, w)


def _rope_table_kernel(pos_ref, freq_ref, cos_ref, sin_ref):
    ang = pos_ref[...] * freq_ref[...]
    lane = lax.broadcasted_iota(jnp.int32, ang.shape, 1)
    first_half = (lane % 64) < 32
    cos_ref[...] = jnp.cos(ang)
    sin_ref[...] = jnp.where(first_half, -1.0, 1.0) * jnp.sin(ang)


def _rope_tables(pos_col, *, tm=2048):
    t = pos_col.shape[0]
    half = MLA_ROPE // 2
    inv_freq = ROPE_THETA ** (-jnp.arange(half, dtype=F32) / half)
    freq = jnp.tile(inv_freq, LANES // half).reshape(1, LANES)
    return pl.pallas_call(
        _rope_table_kernel,
        grid=(t // tm,),
        in_specs=[pl.BlockSpec((tm, 1), lambda i: (i, 0)),
                  pl.BlockSpec((1, LANES), lambda i: (0, 0))],
        out_specs=[pl.BlockSpec((tm, LANES), lambda i: (i, 0))] * 2,
        out_shape=[jax.ShapeDtypeStruct((t, LANES), F32)] * 2,
        compiler_params=_cparams(("parallel",)),
        name="rope_tables",
    )(pos_col, freq)


def _rope(n, cos, sin_signed):
    w = n.shape[-1]
    lane = lax.broadcasted_iota(jnp.int32, n.shape, 1)
    first_half = (lane % 64) < 32
    swapped = jnp.where(first_half, pltpu.roll(n, w - 32, 1), pltpu.roll(n, 32, 1))
    return n * cos + swapped * sin_signed


def _mla_prep_kernel(p_ref, cos_ref, sin_ref, gq_ref, gkv_ref, wuq_ref, wukv_ref,
                     gqn_ref, gqr_ref, gkn_ref, gkr_ref, shift_ref, q_ref, k_ref, v_ref, *, q_scale):
    cq = p_ref[:, 0:MLA_Q_LORA].astype(F32)
    ckv = p_ref[:, MLA_Q_LORA:MLA_Q_LORA + MLA_KV_LORA].astype(F32)
    kr = p_ref[:, 768:896].astype(F32)
    cos = cos_ref[...]
    sin = sin_ref[...]

    q = jnp.dot(_rms(cq, gq_ref[...]).astype(BF16), wuq_ref[...], preferred_element_type=F32)
    kv = jnp.dot(_rms(ckv, gkv_ref[...]).astype(BF16), wukv_ref[...], preferred_element_type=F32)

    n_nope = MLA_HEADS * MLA_NOPE
    kr_ms = jnp.sum(kr * kr, axis=-1, keepdims=True) * (1.0 / MLA_ROPE)
    kr_n = kr * lax.rsqrt(kr_ms + EPS) * gkr_ref[...]
    kr_r = _rope(kr_n, cos, sin)
    kr_dup = kr_r + pltpu.roll(kr_r, 64, 1)

    lane = lax.broadcasted_iota(jnp.int32, (q.shape[0], LANES), 1)
    one_hot = jnp.where(lane % 64 == 0, 1.0, 0.0)
    for pair in range(MLA_HEADS // 2):
        qr = q[:, n_nope + pair * LANES:n_nope + (pair + 1) * LANES]
        qr_n = qr * lax.rsqrt(_ms_half_lanes(qr) + EPS) * gqr_ref[...]
        qr_r = _rope(qr_n, cos, sin) * q_scale
        for sub in range(2):
            h = 2 * pair + sub
            keep = (lane < 64) if sub == 0 else (lane >= 64)
            qn = _rms(q[:, h * MLA_NOPE:(h + 1) * MLA_NOPE], gqn_ref[...]) * q_scale
            q_ref[0, h, 0:LANES, :] = qn.T.astype(BF16)
            q_ref[0, h, LANES:2 * LANES, :] = jnp.where(keep, qr_r, shift_ref[...]).T.astype(BF16)
            kn = _rms(kv[:, h * MLA_NOPE:(h + 1) * MLA_NOPE], gkn_ref[...])
            k_ref[0, h, :, 0:LANES] = kn.astype(BF16)
            k_ref[0, h, :, LANES:2 * LANES] = jnp.where(keep, kr_dup, one_hot).astype(BF16)
            v_ref[0, h, :, :] = kv[:, n_nope + h * MLA_V:n_nope + (h + 1) * MLA_V].T.astype(BF16)


def _mla_prep(p, cos, sin, gq, gkv, wuq, wukv, gqn, gqr, gkn, gkr, shift_row, *, batch, seq, tm=512):
    nb = seq // tm
    full = lambda a: pl.BlockSpec(a.shape, lambda b, i: (0,) * a.ndim)
    row = lambda b, i: (b * nb + i, 0)
    hshape = lambda w: jax.ShapeDtypeStruct((batch, MLA_HEADS, seq, w), BF16)
    hspec = lambda w: pl.BlockSpec((1, MLA_HEADS, tm, w), lambda b, i: (b, 0, i, 0))
    tshape = lambda w: jax.ShapeDtypeStruct((batch, MLA_HEADS, w, seq), BF16)
    tspec = lambda w: pl.BlockSpec((1, MLA_HEADS, w, tm), lambda b, i: (b, 0, 0, i))
    return pl.pallas_call(
        functools.partial(_mla_prep_kernel, q_scale=MLA_Q_SCALE),
        grid=(batch, nb),
        in_specs=[pl.BlockSpec((tm, 1024), row),
                  pl.BlockSpec((tm, LANES), row), pl.BlockSpec((tm, LANES), row),
                  full(gq), full(gkv), full(wuq), full(wukv),
                  full(gqn), full(gqr), full(gkn), full(gkr), full(shift_row)],
        out_specs=[tspec(2 * LANES), hspec(2 * LANES), tspec(MLA_V)],
        out_shape=[tshape(2 * LANES), hshape(2 * LANES), tshape(MLA_V)],
        compiler_params=_cparams(("parallel", "parallel")),
        name="mla_prep",
    )(p, cos, sin, gq, gkv, wuq, wukv, gqn, gqr, gkn, gkr, shift_row)


BOUND_MARGIN = 1.02
MAX_SHIFT_SPAN = 100.0


def _norm_bound(n, gain):
    return math.sqrt(n) * jnp.max(jnp.abs(gain.astype(F32)))


def _score_shift(q_bound, k_bound):
    shift = (BOUND_MARGIN * q_bound * k_bound).astype(BF16).astype(F32)
    return shift, (2.0 * shift <= MAX_SHIFT_SPAN)


def _nt_dot(a, b):
    return lax.dot_general(a, b, (((1,), (1,)), ((), ())), preferred_element_type=F32)


def _lane_partial_sum(p):
    acc = p[:, 0:LANES]
    for c in range(1, p.shape[1] // LANES):
        acc = acc + p[:, c * LANES:(c + 1) * LANES]
    return acc


def _online_update(s, m, l):
    m_new = jnp.maximum(m, jnp.max(s, axis=-1, keepdims=True))
    alpha = jnp.exp2(m - m_new)
    p = jnp.exp2(s - m_new)
    l_new = alpha * l + jnp.sum(p, axis=-1, keepdims=True)
    return p, alpha, m_new, l_new


def _sublane_partial_sum(p):
    return jnp.sum(p.reshape(p.shape[0] // 8, 8, p.shape[1]), axis=0)


def _online_update_t(s, m, l):
    m_new = jnp.maximum(m, jnp.max(s, axis=0, keepdims=True))
    alpha = jnp.exp2(m - m_new)
    p = jnp.exp2(s - m_new)
    return p, alpha, m_new, alpha * l + jnp.sum(p, axis=0, keepdims=True)


def _mla_attn_kernel(flag_ref, qt_ref, k_ref, vt_ref, o_ref, *, tk, nk, unroll):
    tq = qt_ref.shape[3]
    qt = qt_ref[0, 0]

    def scores(k):
        return jnp.dot(k, qt, preferred_element_type=F32)

    def chunk(j):
        off = pl.multiple_of(j * tk, tk)
        return k_ref[0, 0, pl.ds(off, tk), :], vt_ref[0, 0, :, pl.ds(off, tk)]

    def bounded():
        def body(j, carry):
            l, acc = carry
            k, vt = chunk(j)
            p = jnp.exp2(scores(k))
            return (l + _sublane_partial_sum(p),
                    acc + jnp.dot(vt, p.astype(BF16), preferred_element_type=F32))

        init = (jnp.zeros((8, tq), F32), jnp.zeros((MLA_V, tq), F32))
        l, acc = lax.fori_loop(0, nk, body, init, unroll=unroll)
        return acc / jnp.sum(l, axis=0, keepdims=True)

    def online():
        def body(j, carry):
            m, l, acc = carry
            k, vt = chunk(j)
            p, alpha, m, l = _online_update_t(scores(k), m, l)
            return m, l, alpha * acc + jnp.dot(vt, p.astype(BF16), preferred_element_type=F32)

        init = (jnp.full((1, tq), NEG_BIG, F32), jnp.zeros((1, tq), F32), jnp.zeros((MLA_V, tq), F32))
        _, l, acc = lax.fori_loop(0, nk, body, init)
        return acc / l

    o_ref[0] = lax.cond(flag_ref[0] != 0, bounded, online).T.astype(BF16)


def _mla_attn(flag, qt, k, vt, *, tq=1024, tk=512, unroll=16):
    b, h, s, w = k.shape
    return pl.pallas_call(
        functools.partial(_mla_attn_kernel, tk=tk, nk=s // tk, unroll=unroll),
        grid=(b, h, s // tq),
        in_specs=[pl.BlockSpec(memory_space=pltpu.SMEM),
                  pl.BlockSpec((1, 1, w, tq), lambda bi, hi, qi: (bi, hi, 0, qi)),
                  pl.BlockSpec((1, 1, s, w), lambda bi, hi, qi: (bi, hi, 0, 0)),
                  pl.BlockSpec((1, 1, MLA_V, s), lambda bi, hi, qi: (bi, hi, 0, 0))],
        out_specs=pl.BlockSpec((1, tq, MLA_V), lambda bi, hi, qi: (bi, qi, hi)),
        out_shape=jax.ShapeDtypeStruct((b, s, h * MLA_V), BF16),
        compiler_params=_cparams(("parallel", "parallel", "arbitrary")),
        name="mla_attn",
    )(flag, qt, k, vt)


def _diff_prep_kernel(q_ref, k_ref, v_ref, gq_ref, gk_ref, qt_ref, kn_ref, vt_ref, *, q_scale):
    for c in range(DIFF_HEADS):
        sl = slice(c * LANES, (c + 1) * LANES)
        q = q_ref[:, sl].astype(F32)
        k = k_ref[:, sl].astype(F32)
        qt_ref[0, c] = (q * lax.rsqrt(_ms_half_lanes(q) + EPS) * (gq_ref[...] * q_scale)).T.astype(BF16)
        kn_ref[:, sl] = (k * lax.rsqrt(_ms_half_lanes(k) + EPS) * gk_ref[...]).astype(BF16)
        vt_ref[0, c] = v_ref[:, sl].astype(F32).T.astype(BF16)


def _diff_prep(p, gq, gk, *, batch, seq, tm=1024):
    t = p.shape[0]
    nb = seq // tm
    w = DIFF_HEADS * 2 * DIFF_HEAD_DIM
    row = lambda blk: (lambda b, i: (b * nb + i, blk))
    gain = pl.BlockSpec((1, LANES), lambda b, i: (0, 0))
    tshape = jax.ShapeDtypeStruct((batch, DIFF_HEADS, LANES, seq), BF16)
    tspec = pl.BlockSpec((1, DIFF_HEADS, LANES, tm), lambda b, i: (b, 0, 0, i))
    return pl.pallas_call(
        functools.partial(_diff_prep_kernel, q_scale=DIFF_Q_SCALE),
        grid=(batch, nb),
        in_specs=[pl.BlockSpec((tm, w), row(P_DQ_BLK)), pl.BlockSpec((tm, w), row(P_DK_BLK)),
                  pl.BlockSpec((tm, w), row(P_DV_BLK)), gain, gain],
        out_specs=[tspec, pl.BlockSpec((tm, w), row(0)), tspec],
        out_shape=[tshape, jax.ShapeDtypeStruct((t, w), BF16), tshape],
        compiler_params=_cparams(("parallel", "parallel")),
        name="diff_prep",
    )(p, p, p, gq, gk)


def _diff_attn_kernel(scal_ref, qt_ref, k_ref, vt_ref, pq_ref, pk_ref, lam_ref, g_ref, o_ref,
                      *, tk, nk, unroll, lambda_init):
    qt = qt_ref[0, 0]
    tq = qt.shape[1]
    row = lax.broadcasted_iota(jnp.int32, qt.shape, 0)
    zero = jnp.zeros_like(qt)
    q0t = jnp.where(row < DIFF_HEAD_DIM, qt, zero)
    q1t = jnp.where(row < DIFF_HEAD_DIM, zero, qt)
    pq = pq_ref[0]
    slope = scal_ref[pl.program_id(1)] * LOG2E
    shift = scal_ref[DIFF_HEADS]
    use_bounded = scal_ref[DIFF_HEADS + 1]

    spq = slope * pq

    def chunk(j, shifted):
        off = pl.multiple_of(j * tk, tk)
        k = k_ref[0, pl.ds(off, tk), :]
        vt = vt_ref[0, 0, :, pl.ds(off, tk)]
        spk = slope * pk_ref[0, pl.ds(off, tk), :]
        bias = jnp.abs(jnp.concatenate([spk] * (tq // LANES), axis=1) - spq)
        return k, vt, (bias + shift if shifted else bias)

    def scores(k, qt):
        return jnp.dot(k, qt, preferred_element_type=F32)

    def pv(vt, p):
        return jnp.dot(vt, p.astype(BF16), preferred_element_type=F32)

    def bounded():
        def body(j, carry):
            l0, a0, l1, a1 = carry
            k, vt, bias = chunk(j, False)
            p0 = jnp.exp2(scores(k, q0t) - bias)
            p1 = jnp.exp2(scores(k, q1t) - bias)
            return (l0 + _sublane_partial_sum(p0), a0 + pv(vt, p0),
                    l1 + _sublane_partial_sum(p1), a1 + pv(vt, p1))

        zl = jnp.zeros((8, tq), F32)
        za = jnp.zeros((LANES, tq), F32)
        l0, a0, l1, a1 = lax.fori_loop(0, nk, body, (zl, za, zl, za), unroll=unroll)
        return (a0 / jnp.sum(l0, axis=0, keepdims=True), a1 / jnp.sum(l1, axis=0, keepdims=True))

    def online():
        def body(j, carry):
            m0, l0, a0, m1, l1, a1 = carry
            k, vt, bias = chunk(j, True)
            p0, al0, m0, l0 = _online_update_t(scores(k, q0t) - bias, m0, l0)
            a0 = al0 * a0 + pv(vt, p0)
            p1, al1, m1, l1 = _online_update_t(scores(k, q1t) - bias, m1, l1)
            a1 = al1 * a1 + pv(vt, p1)
            return m0, l0, a0, m1, l1, a1

        neg = jnp.full((1, tq), NEG_BIG, F32)
        z1 = jnp.zeros((1, tq), F32)
        za = jnp.zeros((LANES, tq), F32)
        _, l0, a0, _, l1, a1 = lax.fori_loop(0, nk, body, (neg, z1, za, neg, z1, za))
        return a0 / l0, a1 / l1

    o0, o1 = lax.cond(use_bounded != 0.0, bounded, online)
    lam = (jnp.exp(jnp.sum(lam_ref[0:1, :] * lam_ref[1:2, :], axis=-1, keepdims=True))
           - jnp.exp(jnp.sum(lam_ref[2:3, :] * lam_ref[3:4, :], axis=-1, keepdims=True))
           + lambda_init)
    o = (o0 - lam * o1).T
    o_ref[0] = (_rms(o, g_ref[...]) * (1.0 - lambda_init)).astype(BF16)


def _diff_attn(scal, qt, kn, vt, pos_row, pos_col, lam4, subln_g, *, lambda_init, tq=512, tk=512, unroll=16):
    b, s, _ = kn.shape
    return pl.pallas_call(
        functools.partial(_diff_attn_kernel, tk=tk, nk=s // tk, unroll=unroll, lambda_init=lambda_init),
        grid=(b, DIFF_HEADS, s // tq),
        in_specs=[pl.BlockSpec(memory_space=pltpu.SMEM),
                  pl.BlockSpec((1, 1, LANES, tq), lambda bi, hi, qi: (bi, hi, 0, qi)),
                  pl.BlockSpec((1, s, LANES), lambda bi, hi, qi: (bi, 0, hi)),
                  pl.BlockSpec((1, 1, LANES, s), lambda bi, hi, qi: (bi, hi, 0, 0)),
                  pl.BlockSpec((1, 1, tq), lambda bi, hi, qi: (bi, 0, qi)),
                  pl.BlockSpec((1, s, LANES), lambda bi, hi, qi: (bi, 0, 0)),
                  pl.BlockSpec((4, DIFF_HEAD_DIM), lambda bi, hi, qi: (0, 0)),
                  pl.BlockSpec((1, LANES), lambda bi, hi, qi: (0, 0))],
        out_specs=pl.BlockSpec((1, tq, LANES), lambda bi, hi, qi: (bi, qi, hi)),
        out_shape=jax.ShapeDtypeStruct((b, s, DIFF_HEADS * LANES), BF16),
        compiler_params=_cparams(("parallel", "parallel", "arbitrary")),
        name="diff_attn",
    )(scal, qt, kn, vt, pos_row, pos_col, lam4, subln_g)


def _split_dot_f32(a, w_hi, w_lo):
    a_hi = a.astype(BF16)
    a_lo = (a - a_hi.astype(F32)).astype(BF16)
    return (jnp.dot(a_hi, w_hi, preferred_element_type=F32)
            + jnp.dot(a_lo, w_hi, preferred_element_type=F32)
            + jnp.dot(a_hi, w_lo, preferred_element_type=F32))


def _merge_kernel(x_ref, oa_ref, ob_ref, ga_ref, gb_ref, wa_ref, wb_ref, wo_ref, gf_ref, *rest,
                  with_router):
    if with_router:
        wr_hi_ref, wr_lo_ref, xo_ref, h_ref, route_ref = rest
    else:
        xo_ref, h_ref = rest
    ya = jnp.dot(oa_ref[...], wa_ref[...], preferred_element_type=F32)
    yb = jnp.dot(ob_ref[...], wb_ref[...], preferred_element_type=F32)
    merged = ga_ref[...].astype(F32) * ya + gb_ref[...].astype(F32) * yb
    x = x_ref[...] + jnp.dot(merged.astype(BF16), wo_ref[...], preferred_element_type=F32)
    xo_ref[...] = x
    h = _rms(x, gf_ref[...])
    h_ref[...] = h.astype(h_ref.dtype)
    if with_router:
        route_ref[...] = _top2_route(_split_dot_f32(h, wr_hi_ref[...], wr_lo_ref[...]))


def _merge(x, oa, ob, p, wa, wb, wo, gf, router=None, *, tm=512):
    t, d = x.shape
    ga_blk = P_GATE_COL0 // d
    row = lambda i: (i, 0)
    full = lambda a: pl.BlockSpec(a.shape, lambda i: (0,) * a.ndim)
    in_specs = [pl.BlockSpec((tm, d), row), pl.BlockSpec((tm, d), row), pl.BlockSpec((tm, d), row),
                pl.BlockSpec((tm, d), lambda i: (i, ga_blk)),
                pl.BlockSpec((tm, d), lambda i: (i, ga_blk + 1)),
                full(wa), full(wb), full(wo), full(gf)]
    args = [x, oa, ob, p, p, wa, wb, wo, gf]
    out_specs = [pl.BlockSpec((tm, d), row), pl.BlockSpec((tm, d), row)]
    h_dtype = BF16 if router is None else F32
    out_shape = [jax.ShapeDtypeStruct((t, d), F32), jax.ShapeDtypeStruct((t, d), h_dtype)]
    if router is not None:
        in_specs += [full(router[0]), full(router[1])]
        args += list(router)
        out_specs.append(pl.BlockSpec((tm, LANES), row))
        out_shape.append(jax.ShapeDtypeStruct((t, LANES), F32))
    return pl.pallas_call(
        functools.partial(_merge_kernel, with_router=router is not None),
        grid=(t // tm,),
        in_specs=in_specs,
        out_specs=out_specs,
        out_shape=out_shape,
        compiler_params=_cparams(("parallel",)),
        name="merge_router" if router is not None else "merge",
    )(*args)


def _swiglu_act(h, wg, wu):
    g = jnp.dot(h, wg, preferred_element_type=F32)
    u = jnp.dot(h, wu, preferred_element_type=F32)
    return (g * jax.nn.sigmoid(g) * u).astype(BF16)


def _ffn_kernel(x_ref, h_ref, wg_ref, wu_ref, wd_ref, o_ref, acc_ref):
    f = pl.program_id(1)
    y = jnp.dot(_swiglu_act(h_ref[...], wg_ref[...], wu_ref[...]), wd_ref[...],
                preferred_element_type=F32)

    @pl.when(f == 0)
    def _():
        acc_ref[...] = x_ref[...] + y

    @pl.when(f > 0)
    def _():
        acc_ref[...] += y

    @pl.when(f == pl.num_programs(1) - 1)
    def _():
        o_ref[...] = acc_ref[...]


def _ffn(x, h, wg, wu, wd, *, tm=1024, tf=1408):
    t, d = x.shape
    ff = wg.shape[1]
    return pl.pallas_call(
        _ffn_kernel,
        grid=(t // tm, ff // tf),
        in_specs=[pl.BlockSpec((tm, d), lambda i, f: (i, 0)),
                  pl.BlockSpec((tm, d), lambda i, f: (i, 0)),
                  pl.BlockSpec((d, tf), lambda i, f: (0, f)),
                  pl.BlockSpec((d, tf), lambda i, f: (0, f)),
                  pl.BlockSpec((tf, d), lambda i, f: (f, 0))],
        out_specs=pl.BlockSpec((tm, d), lambda i, f: (i, 0)),
        out_shape=jax.ShapeDtypeStruct((t, d), F32),
        scratch_shapes=[pltpu.VMEM((tm, d), F32)],
        compiler_params=_cparams(("parallel", "arbitrary")),
        name="ffn_dense",
    )(x, h, wg, wu, wd)


def _top2_route(logits):
    lane_i = lax.broadcasted_iota(jnp.int32, logits.shape, 1)
    lane = lane_i.astype(F32)
    neg = jnp.float32(-jnp.inf)
    z = jnp.where(lane_i < N_EXPERTS, logits, neg)
    m1 = jnp.max(z, axis=-1, keepdims=True)
    i1 = jnp.min(jnp.where(z == m1, lane, float(LANES)), axis=-1, keepdims=True)
    z2 = jnp.where(lane == i1, neg, z)
    m2 = jnp.max(z2, axis=-1, keepdims=True)
    i2 = jnp.min(jnp.where(z2 == m2, lane, float(LANES)), axis=-1, keepdims=True)
    e2 = jnp.exp(m2 - m1)
    w1 = 1.0 / (1.0 + e2)
    w2 = e2 / (1.0 + e2)
    return jnp.where(lane_i == 0, i1, jnp.where(lane_i == 1, i2,
                     jnp.where(lane_i == 2, w1, jnp.where(lane_i == 3, w2, 0.0))))


def _route_plan(route, tm):
    t = route.shape[0]
    e = route[:, 0:TOP_K].astype(jnp.int32)
    sel = (e[:, :, None] == jnp.arange(N_EXPERTS, dtype=jnp.int32)).astype(jnp.int32).sum(axis=1)
    csum = jnp.cumsum(sel, axis=0)
    padded = (csum[-1] + tm - 1) // tm * tm
    end = jnp.cumsum(padded)
    pos = (end - padded)[e] + jnp.take_along_axis(csum - sel, e, axis=1)
    n_rows = TOP_K * t + N_EXPERTS * tm
    flat = pos.reshape(-1)
    tok = jnp.zeros((n_rows,), jnp.int32).at[flat].set(
        jnp.repeat(jnp.arange(t, dtype=jnp.int32), TOP_K), unique_indices=True)
    n_tiles = n_rows // tm
    row0 = jnp.arange(n_tiles, dtype=jnp.int32) * tm
    valid = (row0 < end[-1]).astype(jnp.int32)
    tile_e = jnp.searchsorted(end, jnp.minimum(row0, end[-1] - 1), side="right").astype(jnp.int32)
    return tok.reshape(n_tiles, 1, tm), tile_e, valid, pos


def _moe_gmm_kernel(tile_e_ref, valid_ref, tok_ref, tok_next_ref, h_hbm, wg_ref, wu_ref, wd_ref,
                    o_ref, xg_ref, xb_ref, acc_ref, sem, *, nf, gather_steps):
    del tile_e_ref
    i = pl.program_id(0)
    f = pl.program_id(1)
    tm = xb_ref.shape[0]
    slot = i % 2

    def row_copy(tok, r, s):
        return pltpu.make_async_copy(h_hbm.at[pl.ds(tok[0, 0, r], 1), :],
                                     xg_ref.at[s, pl.ds(r, 1), :], sem.at[s])

    @pl.when((i == 0) & (f == 0))
    def _():
        def body(r, c):
            row_copy(tok_ref, r, 0).start()
            return c
        lax.fori_loop(0, tm, body, 0, unroll=8)

    valid = valid_ref[i] != 0
    prev_valid = valid_ref[jnp.maximum(i - 1, 0)] != 0

    @pl.when((f == 0) & ((i == 0) | prev_valid))
    def _():
        pltpu.make_async_copy(h_hbm.at[pl.ds(0, tm), :], xg_ref.at[slot], sem.at[slot]).wait()
        xb_ref[...] = xg_ref[slot].astype(BF16)

    def step(request_rows):
        if request_rows:
            rows = tm // gather_steps
            for r in range(rows):
                row_copy(tok_next_ref, f * rows + r, 1 - slot).start()
        y = jnp.dot(_swiglu_act(xb_ref[...], wg_ref[0], wu_ref[0]), wd_ref[0],
                    preferred_element_type=F32)

        @pl.when(f == 0)
        def _():
            acc_ref[...] = y

        @pl.when((f > 0) & (f < nf - 1))
        def _():
            acc_ref[...] += y

        @pl.when(f == nf - 1)
        def _():
            o_ref[...] = acc_ref[...] + y

    pl.when(valid & (f < gather_steps))(functools.partial(step, True))
    pl.when(valid & (f >= gather_steps))(functools.partial(step, False))

    @pl.when(jnp.logical_not(valid) & (f == nf - 1))
    def _():
        o_ref[...] = jnp.zeros_like(o_ref)


def _moe_gmm(tok, tile_e, valid, h, wg, wu, wd, *, tf=1792, gather_steps=2):
    n_tiles, _, tm = tok.shape
    d = h.shape[1]
    ff = wg.shape[2]
    nf = ff // tf
    assert 2 <= gather_steps <= nf and tm % gather_steps == 0
    tok_spec = lambda imap: pl.BlockSpec((1, 1, tm), imap, memory_space=pltpu.SMEM)
    grid_spec = pltpu.PrefetchScalarGridSpec(
        num_scalar_prefetch=2,
        grid=(n_tiles, nf),
        in_specs=[tok_spec(lambda i, f, te, va: (i, 0, 0)),
                  tok_spec(lambda i, f, te, va: (jnp.minimum(i + 1, n_tiles - 1), 0, 0)),
                  pl.BlockSpec(memory_space=pl.ANY),
                  pl.BlockSpec((1, d, tf), lambda i, f, te, va: (te[i], 0, f)),
                  pl.BlockSpec((1, d, tf), lambda i, f, te, va: (te[i], 0, f)),
                  pl.BlockSpec((1, tf, d), lambda i, f, te, va: (te[i], f, 0))],
        out_specs=pl.BlockSpec((tm, d), lambda i, f, te, va: (i, 0)),
        scratch_shapes=[pltpu.VMEM((2, tm, d), F32), pltpu.VMEM((tm, d), BF16), pltpu.VMEM((tm, d), F32),
                        pltpu.SemaphoreType.DMA((2,))],
    )
    return pl.pallas_call(
        functools.partial(_moe_gmm_kernel, nf=nf, gather_steps=gather_steps),
        grid_spec=grid_spec,
        out_shape=jax.ShapeDtypeStruct((n_tiles * tm, d), F32),
        compiler_params=_cparams(("arbitrary", "arbitrary")),
        name="moe_gmm",
    )(tile_e, valid, tok, tok, h, wg, wu, wd)


def _moe_combine_kernel(pos_ref, pos_next_ref, x_ref, route_ref, y_hbm, o_ref, buf_ref, sem):
    i = pl.program_id(0)
    tc = x_ref.shape[0]
    slot = i % 2

    def row_copy(pos, r, s):
        return pltpu.make_async_copy(y_hbm.at[pl.ds(pos[0, 0, r], 1), :],
                                     buf_ref.at[s, pl.ds(r, 1), :], sem.at[s])

    @pl.when(i == 0)
    def _():
        def body(r, c):
            row_copy(pos_ref, r, 0).start()
            return c
        lax.fori_loop(0, TOP_K * tc, body, 0, unroll=8)

    @pl.when(i + 1 < pl.num_programs(0))
    def _():
        for r in range(TOP_K * tc):
            row_copy(pos_next_ref, r, 1 - slot).start()

    pltpu.make_async_copy(y_hbm.at[pl.ds(0, TOP_K * tc), :], buf_ref.at[slot], sem.at[slot]).wait()
    y = buf_ref[slot]
    w = route_ref[...]
    o_ref[...] = (x_ref[...] + w[:, TOP_K:TOP_K + 1] * y[0:tc]
                  + w[:, TOP_K + 1:TOP_K + 2] * y[tc:2 * tc])


def _moe_combine(x, route, y_rows, pos, *, tc=512):
    t, d = x.shape
    n = t // tc
    pos_t = pos.reshape(n, tc, TOP_K).transpose(0, 2, 1).reshape(n, 1, TOP_K * tc)
    pos_spec = lambda imap: pl.BlockSpec((1, 1, TOP_K * tc), imap, memory_space=pltpu.SMEM)
    return pl.pallas_call(
        _moe_combine_kernel,
        grid=(n,),
        in_specs=[pos_spec(lambda i: (i, 0, 0)),
                  pos_spec(lambda i: (jnp.minimum(i + 1, n - 1), 0, 0)),
                  pl.BlockSpec((tc, d), lambda i: (i, 0)),
                  pl.BlockSpec((tc, LANES), lambda i: (i, 0)),
                  pl.BlockSpec(memory_space=pl.ANY)],
        out_specs=pl.BlockSpec((tc, d), lambda i: (i, 0)),
        out_shape=jax.ShapeDtypeStruct((t, d), F32),
        scratch_shapes=[pltpu.VMEM((2, TOP_K * tc, d), F32), pltpu.SemaphoreType.DMA((2,))],
        compiler_params=_cparams(("arbitrary",)),
        name="moe_combine",
    )(pos_t, pos_t, x, route, y_rows)


def _row(v, reps=1):
    return jnp.tile(v.astype(F32), reps).reshape(1, -1)


def kernel(x, positions, attn_norm_g, w_in, w_gate, mla_q_norm_g, mla_kv_norm_g, mla_w_uq, mla_w_ukv, mla_qn_nope_g, mla_qn_rope_g, mla_kn_nope_g, mla_kn_rope_g, diff_qn_g, diff_kn_g, diff_lam_q1, diff_lam_k1, diff_lam_q2, diff_lam_k2, diff_subln_g, w_branch_a, w_branch_b, w_out, ffn_norm_g, dense_w_gate, dense_w_up, dense_w_down, moe_w_router, moe_w_gate, moe_w_up, moe_w_down):
    batch, seq, d = x.shape
    t = batch * seq
    xf = x.reshape(t, d)
    pos_f = positions.astype(F32)
    cos, sin = _rope_tables(pos_f.reshape(t, 1))
    pos_col = jnp.broadcast_to(pos_f.reshape(batch, seq, 1), (batch, seq, LANES))
    pos_row = pos_f.reshape(batch, 1, seq)
    slopes = 2.0 ** (-8.0 * jnp.arange(1, DIFF_HEADS + 1, dtype=F32) / DIFF_HEADS)
    n_in_head = MLA_Q_LORA + MLA_KV_LORA + MLA_ROPE

    for layer in range(DEPTH):
        w1 = jnp.concatenate(
            [w_in[layer][:, :n_in_head], jnp.zeros((d, 1024 - n_in_head), F32),
             w_in[layer][:, n_in_head:], w_gate[layer]], axis=1).astype(BF16)
        wuq = mla_w_uq[layer].reshape(MLA_Q_LORA, MLA_HEADS, MLA_NOPE + MLA_ROPE)
        wuq = jnp.concatenate([wuq[:, :, :MLA_NOPE].reshape(MLA_Q_LORA, -1),
                               wuq[:, :, MLA_NOPE:].reshape(MLA_Q_LORA, -1)], axis=1).astype(BF16)
        wukv = mla_w_ukv[layer].reshape(MLA_KV_LORA, MLA_HEADS, MLA_NOPE + MLA_V)
        wukv = jnp.concatenate([wukv[:, :, :MLA_NOPE].reshape(MLA_KV_LORA, -1),
                                wukv[:, :, MLA_NOPE:].reshape(MLA_KV_LORA, -1)], axis=1).astype(BF16)
        gkr = jnp.concatenate([mla_kn_rope_g[layer].astype(F32), jnp.zeros((LANES - MLA_ROPE,), F32)]).reshape(1, -1)

        p = _norm_proj(xf, _row(attn_norm_g[layer]), w1)
        q_bound = MLA_Q_SCALE * jnp.sqrt(_norm_bound(MLA_NOPE, mla_qn_nope_g[layer]) ** 2
                                         + _norm_bound(MLA_ROPE, mla_qn_rope_g[layer]) ** 2)
        k_bound = jnp.sqrt(_norm_bound(MLA_NOPE, mla_kn_nope_g[layer]) ** 2
                           + _norm_bound(MLA_ROPE, mla_kn_rope_g[layer]) ** 2)
        shift, use_bounded = _score_shift(q_bound, k_bound)
        shift_row = jnp.where(jnp.arange(LANES) % 64 == 0, -shift, 0.0).astype(F32).reshape(1, LANES)
        q_m, k_m, v_m = _mla_prep(
            p, cos, sin, _row(mla_q_norm_g[layer]), _row(mla_kv_norm_g[layer]), wuq, wukv,
            _row(mla_qn_nope_g[layer]), _row(mla_qn_rope_g[layer], 2), _row(mla_kn_nope_g[layer]), gkr,
            shift_row, batch=batch, seq=seq)
        o_a = _mla_attn(use_bounded.astype(jnp.int32).reshape(1), q_m, k_m, v_m).reshape(t, -1)

        lambda_init = 0.8 - 0.6 * math.exp(-0.3 * layer)
        qt_d, kn, vt_d = _diff_prep(p, _row(diff_qn_g[layer], 2), _row(diff_kn_g[layer], 2),
                                  batch=batch, seq=seq)
        lam4 = jnp.stack([diff_lam_q1[layer], diff_lam_k1[layer],
                          diff_lam_q2[layer], diff_lam_k2[layer]]).astype(F32)
        d_shift, d_bounded = _score_shift(DIFF_Q_SCALE * _norm_bound(DIFF_HEAD_DIM, diff_qn_g[layer]),
                                          _norm_bound(DIFF_HEAD_DIM, diff_kn_g[layer]))
        scal = jnp.concatenate([slopes, d_shift.reshape(1), d_bounded.astype(F32).reshape(1)])
        o_b = _diff_attn(scal, qt_d, kn.reshape(batch, seq, -1), vt_d,
                         pos_row, pos_col, lam4, _row(diff_subln_g[layer]),
                         lambda_init=lambda_init).reshape(t, -1)

        j = layer // 2
        wa, wb, wo = (w.astype(BF16) for w in (w_branch_a[layer], w_branch_b[layer], w_out[layer]))
        if layer % 2 == 0:
            xf, h = _merge(xf, o_a, o_b, p, wa, wb, wo, _row(ffn_norm_g[layer]))
            xf = _ffn(xf, h, dense_w_gate[j].astype(BF16), dense_w_up[j].astype(BF16),
                      dense_w_down[j].astype(BF16))
        else:
            wr = jnp.pad(moe_w_router[j].astype(F32), ((0, 0), (0, LANES - N_EXPERTS)))
            wr_hi = wr.astype(BF16)
            wr_lo = (wr - wr_hi.astype(F32)).astype(BF16)
            xf, h, route = _merge(xf, o_a, o_b, p, wa, wb, wo, _row(ffn_norm_g[layer]),
                                  router=(wr_hi, wr_lo))
            tok, tile_e, valid, pos = _route_plan(route, MOE_TILE_ROWS)
            y_rows = _moe_gmm(tok, tile_e, valid, h, moe_w_gate[j].astype(BF16),
                              moe_w_up[j].astype(BF16), moe_w_down[j].astype(BF16))
            xf = _moe_combine(xf, route, y_rows, pos)
    return xf.reshape(batch, seq, d)
```

```python
import functools
import math

import jax
import jax.numpy as jnp
from jax import lax
from jax.experimental import pallas as pl
from jax.experimental.pallas import tpu as pltpu

D_MODEL = 1024
DEPTH = 2
MLA_HEADS = 8
MLA_Q_LORA = 512
MLA_KV_LORA = 256
MLA_NOPE = 128
MLA_ROPE = 64
MLA_V = 128
ROPE_THETA = 10000.0
DIFF_HEADS = 8
DIFF_HEAD_DIM = 64
D_FF_DENSE = 2816
N_EXPERTS = 8
TOP_K = 2
D_FF_EXPERT = 3584
EPS = 1e-6

LANES = 128
LOG2E = 1.4426950408889634
MLA_Q_SCALE = (MLA_NOPE + MLA_ROPE) ** -0.5 * LOG2E
DIFF_Q_SCALE = DIFF_HEAD_DIM ** -0.5 * LOG2E
NEG_BIG = -1e30
BF16 = jnp.bfloat16
F32 = jnp.float32

P_BLK = DIFF_HEADS * 2 * DIFF_HEAD_DIM
P_KR_COL0 = MLA_Q_LORA + MLA_KV_LORA
P_DQ_BLK = 1
P_DK_BLK = 2
P_DV_BLK = 3
P_GATE_COL0 = 4 * P_BLK
MOE_TILE_ROWS = 512
VMEM_LIMIT = 52 * 1024 * 1024


def _cparams(sem):
    return pltpu.CompilerParams(dimension_semantics=sem, vmem_limit_bytes=VMEM_LIMIT)


def _rms(x, gain):
    ms = jnp.mean(x * x, axis=-1, keepdims=True)
    return x * lax.rsqrt(ms + EPS) * gain


def _ms_half_lanes(xb):
    lane = lax.broadcasted_iota(jnp.int32, xb.shape, 1)
    lo = lane < 64
    sq = xb * xb
    s_lo = jnp.sum(jnp.where(lo, sq, 0.0), axis=-1, keepdims=True)
    s_hi = jnp.sum(jnp.where(lo, 0.0, sq), axis=-1, keepdims=True)
    return jnp.where(lo, s_lo, s_hi) * (1.0 / 64.0)


def _norm_proj_kernel(x_ref, g_ref, w_ref, o_ref, h_ref, *, gate_tile0):
    j = pl.program_id(1)

    @pl.when(j == 0)
    def _():
        h_ref[...] = _rms(x_ref[...], g_ref[...]).astype(BF16)

    y = jnp.dot(h_ref[...], w_ref[...], preferred_element_type=F32)

    @pl.when(j < gate_tile0)
    def _():
        o_ref[...] = y.astype(BF16)

    @pl.when(j >= gate_tile0)
    def _():
        o_ref[...] = jax.nn.sigmoid(y).astype(BF16)


def _norm_proj(x, gain, w, *, tm=1024, tn=2048):
    t, d = x.shape
    n = w.shape[1]
    return pl.pallas_call(
        functools.partial(_norm_proj_kernel, gate_tile0=P_GATE_COL0 // tn),
        grid=(t // tm, n // tn),
        in_specs=[
            pl.BlockSpec((tm, d), lambda i, j: (i, 0)),
            pl.BlockSpec((1, d), lambda i, j: (0, 0)),
            pl.BlockSpec((d, tn), lambda i, j: (0, j)),
        ],
        out_specs=pl.BlockSpec((tm, tn), lambda i, j: (i, j)),
        out_shape=jax.ShapeDtypeStruct((t, n), BF16),
        scratch_shapes=[pltpu.VMEM((tm, d), BF16)],
        compiler_params=_cparams(("parallel", "arbitrary")),
        name="norm_proj",
    )(x, gain, w)


def _rope_table_kernel(pos_ref, freq_ref, cos_ref, sin_ref):
    ang = pos_ref[...] * freq_ref[...]
    lane = lax.broadcasted_iota(jnp.int32, ang.shape, 1)
    first_half = (lane % 64) < 32
    cos_ref[...] = jnp.cos(ang)
    sin_ref[...] = jnp.where(first_half, -1.0, 1.0) * jnp.sin(ang)


def _rope_tables(pos_col, *, tm=2048):
    t = pos_col.shape[0]
    half = MLA_ROPE // 2
    inv_freq = ROPE_THETA ** (-jnp.arange(half, dtype=F32) / half)
    freq = jnp.tile(inv_freq, LANES // half).reshape(1, LANES)
    return pl.pallas_call(
        _rope_table_kernel,
        grid=(t // tm,),
        in_specs=[pl.BlockSpec((tm, 1), lambda i: (i, 0)),
                  pl.BlockSpec((1, LANES), lambda i: (0, 0))],
        out_specs=[pl.BlockSpec((tm, LANES), lambda i: (i, 0))] * 2,
        out_shape=[jax.ShapeDtypeStruct((t, LANES), F32)] * 2,
        compiler_params=_cparams(("parallel",)),
        name="rope_tables",
    )(pos_col, freq)


def _rope(n, cos, sin_signed):
    w = n.shape[-1]
    lane = lax.broadcasted_iota(jnp.int32, n.shape, 1)
    first_half = (lane % 64) < 32
    swapped = jnp.where(first_half, pltpu.roll(n, w - 32, 1), pltpu.roll(n, 32, 1))
    return n * cos + swapped * sin_signed


def _mla_prep_kernel(p_ref, cos_ref, sin_ref, gq_ref, gkv_ref, wuq_ref, wukv_ref,
                     gqn_ref, gqr_ref, gkn_ref, gkr_ref, shift_ref, q_ref, k_ref, v_ref, *, q_scale):
    cq = p_ref[:, 0:MLA_Q_LORA].astype(F32)
    ckv = p_ref[:, MLA_Q_LORA:MLA_Q_LORA + MLA_KV_LORA].astype(F32)
    kr = p_ref[:, P_KR_COL0:P_KR_COL0 + LANES].astype(F32)
    cos = cos_ref[...]
    sin = sin_ref[...]

    q = jnp.dot(_rms(cq, gq_ref[...]).astype(BF16), wuq_ref[...], preferred_element_type=F32)
    kv = jnp.dot(_rms(ckv, gkv_ref[...]).astype(BF16), wukv_ref[...], preferred_element_type=F32)

    n_nope = MLA_HEADS * MLA_NOPE
    kr_ms = jnp.sum(kr * kr, axis=-1, keepdims=True) * (1.0 / MLA_ROPE)
    kr_n = kr * lax.rsqrt(kr_ms + EPS) * gkr_ref[...]
    kr_r = _rope(kr_n, cos, sin)
    kr_dup = kr_r + pltpu.roll(kr_r, 64, 1)

    lane = lax.broadcasted_iota(jnp.int32, (q.shape[0], LANES), 1)
    one_hot = jnp.where(lane % 64 == 0, 1.0, 0.0)
    for pair in range(MLA_HEADS // 2):
        qr = q[:, n_nope + pair * LANES:n_nope + (pair + 1) * LANES]
        qr_n = qr * lax.rsqrt(_ms_half_lanes(qr) + EPS) * gqr_ref[...]
        qr_r = _rope(qr_n, cos, sin) * q_scale
        for sub in range(2):
            h = 2 * pair + sub
            keep = (lane < 64) if sub == 0 else (lane >= 64)
            qn = _rms(q[:, h * MLA_NOPE:(h + 1) * MLA_NOPE], gqn_ref[...]) * q_scale
            q_ref[0, h, 0:LANES, :] = qn.T.astype(BF16)
            q_ref[0, h, LANES:2 * LANES, :] = jnp.where(keep, qr_r, shift_ref[...]).T.astype(BF16)
            kn = _rms(kv[:, h * MLA_NOPE:(h + 1) * MLA_NOPE], gkn_ref[...])
            k_ref[0, h, :, 0:LANES] = kn.astype(BF16)
            k_ref[0, h, :, LANES:2 * LANES] = jnp.where(keep, kr_dup, one_hot).astype(BF16)
            v_ref[0, h, :, :] = kv[:, n_nope + h * MLA_V:n_nope + (h + 1) * MLA_V].T.astype(BF16)


def _mla_prep(p, cos, sin, gq, gkv, wuq, wukv, gqn, gqr, gkn, gkr, shift_row, *, batch, seq, tm=512):
    nb = seq // tm
    full = lambda a: pl.BlockSpec(a.shape, lambda b, i: (0,) * a.ndim)
    row = lambda b, i: (b * nb + i, 0)
    hshape = lambda w: jax.ShapeDtypeStruct((batch, MLA_HEADS, seq, w), BF16)
    hspec = lambda w: pl.BlockSpec((1, MLA_HEADS, tm, w), lambda b, i: (b, 0, i, 0))
    tshape = lambda w: jax.ShapeDtypeStruct((batch, MLA_HEADS, w, seq), BF16)
    tspec = lambda w: pl.BlockSpec((1, MLA_HEADS, w, tm), lambda b, i: (b, 0, 0, i))
    return pl.pallas_call(
        functools.partial(_mla_prep_kernel, q_scale=MLA_Q_SCALE),
        grid=(batch, nb),
        in_specs=[pl.BlockSpec((tm, P_BLK), row),
                  pl.BlockSpec((tm, LANES), row), pl.BlockSpec((tm, LANES), row),
                  full(gq), full(gkv), full(wuq), full(wukv),
                  full(gqn), full(gqr), full(gkn), full(gkr), full(shift_row)],
        out_specs=[tspec(2 * LANES), hspec(2 * LANES), tspec(MLA_V)],
        out_shape=[tshape(2 * LANES), hshape(2 * LANES), tshape(MLA_V)],
        compiler_params=_cparams(("parallel", "parallel")),
        name="mla_prep",
    )(p, cos, sin, gq, gkv, wuq, wukv, gqn, gqr, gkn, gkr, shift_row)


BOUND_MARGIN = 1.02
MAX_SHIFT_SPAN = 100.0


def _norm_bound(n, gain):
    return math.sqrt(n) * jnp.max(jnp.abs(gain.astype(F32)))


def _score_shift(q_bound, k_bound):
    shift = (BOUND_MARGIN * q_bound * k_bound).astype(BF16).astype(F32)
    return shift, (2.0 * shift <= MAX_SHIFT_SPAN)


def _sublane_partial_sum(p):
    return jnp.sum(p.reshape(p.shape[0] // 8, 8, p.shape[1]), axis=0)


def _online_update_t(s, m, l):
    m_new = jnp.maximum(m, jnp.max(s, axis=0, keepdims=True))
    alpha = jnp.exp2(m - m_new)
    p = jnp.exp2(s - m_new)
    return p, alpha, m_new, alpha * l + jnp.sum(p, axis=0, keepdims=True)


def _mla_attn_kernel(flag_ref, qt_ref, k_ref, vt_ref, o_ref, *, tk, nk, unroll):
    tq = qt_ref.shape[3]
    qt = qt_ref[0, 0]

    def scores(k):
        return jnp.dot(k, qt, preferred_element_type=F32)

    def chunk(j):
        off = pl.multiple_of(j * tk, tk)
        return k_ref[0, 0, pl.ds(off, tk), :], vt_ref[0, 0, :, pl.ds(off, tk)]

    def bounded():
        def body(j, carry):
            l, acc = carry
            k, vt = chunk(j)
            p = jnp.exp2(scores(k))
            return (l + _sublane_partial_sum(p),
                    acc + jnp.dot(vt, p.astype(BF16), preferred_element_type=F32))

        init = (jnp.zeros((8, tq), F32), jnp.zeros((MLA_V, tq), F32))
        l, acc = lax.fori_loop(0, nk, body, init, unroll=unroll)
        return acc / jnp.sum(l, axis=0, keepdims=True)

    def online():
        def body(j, carry):
            m, l, acc = carry
            k, vt = chunk(j)
            p, alpha, m, l = _online_update_t(scores(k), m, l)
            return m, l, alpha * acc + jnp.dot(vt, p.astype(BF16), preferred_element_type=F32)

        init = (jnp.full((1, tq), NEG_BIG, F32), jnp.zeros((1, tq), F32), jnp.zeros((MLA_V, tq), F32))
        _, l, acc = lax.fori_loop(0, nk, body, init)
        return acc / l

    o_ref[0] = lax.cond(flag_ref[0] != 0, bounded, online).T.astype(BF16)


def _mla_attn(flag, qt, k, vt, *, tq=1024, tk=1024, unroll=8):
    b, h, s, w = k.shape
    return pl.pallas_call(
        functools.partial(_mla_attn_kernel, tk=tk, nk=s // tk, unroll=unroll),
        grid=(b, h, s // tq),
        in_specs=[pl.BlockSpec(memory_space=pltpu.SMEM),
                  pl.BlockSpec((1, 1, w, tq), lambda bi, hi, qi: (bi, hi, 0, qi)),
                  pl.BlockSpec((1, 1, s, w), lambda bi, hi, qi: (bi, hi, 0, 0)),
                  pl.BlockSpec((1, 1, MLA_V, s), lambda bi, hi, qi: (bi, hi, 0, 0))],
        out_specs=pl.BlockSpec((1, tq, MLA_V), lambda bi, hi, qi: (bi, qi, hi)),
        out_shape=jax.ShapeDtypeStruct((b, s, h * MLA_V), BF16),
        compiler_params=_cparams(("parallel", "parallel", "arbitrary")),
        name="mla_attn",
    )(flag, qt, k, vt)


def _diff_prep_kernel(q_ref, k_ref, v_ref, gq_ref, gk_ref, qt_ref, kn_ref, vt_ref, *, q_scale):
    for c in range(DIFF_HEADS):
        sl = slice(c * LANES, (c + 1) * LANES)
        q = q_ref[:, sl].astype(F32)
        k = k_ref[:, sl].astype(F32)
        qt_ref[0, c] = (q * lax.rsqrt(_ms_half_lanes(q) + EPS) * (gq_ref[...] * q_scale)).T.astype(BF16)
        kn_ref[:, sl] = (k * lax.rsqrt(_ms_half_lanes(k) + EPS) * gk_ref[...]).astype(BF16)
        vt_ref[0, c] = v_ref[:, sl].astype(F32).T.astype(BF16)


def _diff_prep(p, gq, gk, *, batch, seq, tm=1024):
    t = p.shape[0]
    nb = seq // tm
    w = DIFF_HEADS * 2 * DIFF_HEAD_DIM
    row = lambda blk: (lambda b, i: (b * nb + i, blk))
    gain = pl.BlockSpec((1, LANES), lambda b, i: (0, 0))
    tshape = jax.ShapeDtypeStruct((batch, DIFF_HEADS, LANES, seq), BF16)
    tspec = pl.BlockSpec((1, DIFF_HEADS, LANES, tm), lambda b, i: (b, 0, 0, i))
    return pl.pallas_call(
        functools.partial(_diff_prep_kernel, q_scale=DIFF_Q_SCALE),
        grid=(batch, nb),
        in_specs=[pl.BlockSpec((tm, w), row(P_DQ_BLK)), pl.BlockSpec((tm, w), row(P_DK_BLK)),
                  pl.BlockSpec((tm, w), row(P_DV_BLK)), gain, gain],
        out_specs=[tspec, pl.BlockSpec((tm, w), row(0)), tspec],
        out_shape=[tshape, jax.ShapeDtypeStruct((t, w), BF16), tshape],
        compiler_params=_cparams(("parallel", "parallel")),
        name="diff_prep",
    )(p, p, p, gq, gk)


def _diff_attn_kernel(scal_ref, qt_ref, k_ref, vt_ref, pq_ref, pk_ref, lam_ref, g_ref, o_ref,
                      *, tk, nk, unroll, lambda_init):
    qt = qt_ref[0, 0]
    tq = qt.shape[1]
    row = lax.broadcasted_iota(jnp.int32, qt.shape, 0)
    zero = jnp.zeros_like(qt)
    q0t = jnp.where(row < DIFF_HEAD_DIM, qt, zero)
    q1t = jnp.where(row < DIFF_HEAD_DIM, zero, qt)
    pq = pq_ref[0]
    slope = scal_ref[pl.program_id(1)] * LOG2E
    shift = scal_ref[DIFF_HEADS]
    use_bounded = scal_ref[DIFF_HEADS + 1]

    spq = slope * pq

    def chunk(j, shifted):
        off = pl.multiple_of(j * tk, tk)
        k = k_ref[0, pl.ds(off, tk), :]
        vt = vt_ref[0, 0, :, pl.ds(off, tk)]
        spk = slope * pk_ref[0, pl.ds(off, tk), :]
        bias = jnp.abs(jnp.concatenate([spk] * (tq // LANES), axis=1) - spq)
        return k, vt, (bias + shift if shifted else bias)

    def scores(k, qt):
        return jnp.dot(k, qt, preferred_element_type=F32)

    def pv(vt, p):
        return jnp.dot(vt, p.astype(BF16), preferred_element_type=F32)

    def bounded():
        def body(j, carry):
            l0, a0, l1, a1 = carry
            k, vt, bias = chunk(j, False)
            p0 = jnp.exp2(scores(k, q0t) - bias)
            p1 = jnp.exp2(scores(k, q1t) - bias)
            return (l0 + _sublane_partial_sum(p0), a0 + pv(vt, p0),
                    l1 + _sublane_partial_sum(p1), a1 + pv(vt, p1))

        zl = jnp.zeros((8, tq), F32)
        za = jnp.zeros((LANES, tq), F32)
        l0, a0, l1, a1 = lax.fori_loop(0, nk, body, (zl, za, zl, za), unroll=unroll)
        return (a0 / jnp.sum(l0, axis=0, keepdims=True), a1 / jnp.sum(l1, axis=0, keepdims=True))

    def online():
        def body(j, carry):
            m0, l0, a0, m1, l1, a1 = carry
            k, vt, bias = chunk(j, True)
            p0, al0, m0, l0 = _online_update_t(scores(k, q0t) - bias, m0, l0)
            a0 = al0 * a0 + pv(vt, p0)
            p1, al1, m1, l1 = _online_update_t(scores(k, q1t) - bias, m1, l1)
            a1 = al1 * a1 + pv(vt, p1)
            return m0, l0, a0, m1, l1, a1

        neg = jnp.full((1, tq), NEG_BIG, F32)
        z1 = jnp.zeros((1, tq), F32)
        za = jnp.zeros((LANES, tq), F32)
        _, l0, a0, _, l1, a1 = lax.fori_loop(0, nk, body, (neg, z1, za, neg, z1, za))
        return a0 / l0, a1 / l1

    o0, o1 = lax.cond(use_bounded != 0.0, bounded, online)
    lam = (jnp.exp(jnp.sum(lam_ref[0:1, :] * lam_ref[1:2, :], axis=-1, keepdims=True))
           - jnp.exp(jnp.sum(lam_ref[2:3, :] * lam_ref[3:4, :], axis=-1, keepdims=True))
           + lambda_init)
    o = (o0 - lam * o1).T
    o_ref[0] = (_rms(o, g_ref[...]) * (1.0 - lambda_init)).astype(BF16)


def _diff_attn(scal, qt, kn, vt, pos_row, pos_col, lam4, subln_g, *, lambda_init, tq=512, tk=512, unroll=16):
    b, s, _ = kn.shape
    return pl.pallas_call(
        functools.partial(_diff_attn_kernel, tk=tk, nk=s // tk, unroll=unroll, lambda_init=lambda_init),
        grid=(b, DIFF_HEADS, s // tq),
        in_specs=[pl.BlockSpec(memory_space=pltpu.SMEM),
                  pl.BlockSpec((1, 1, LANES, tq), lambda bi, hi, qi: (bi, hi, 0, qi)),
                  pl.BlockSpec((1, s, LANES), lambda bi, hi, qi: (bi, 0, hi)),
                  pl.BlockSpec((1, 1, LANES, s), lambda bi, hi, qi: (bi, hi, 0, 0)),
                  pl.BlockSpec((1, 1, tq), lambda bi, hi, qi: (bi, 0, qi)),
                  pl.BlockSpec((1, s, LANES), lambda bi, hi, qi: (bi, 0, 0)),
                  pl.BlockSpec((4, DIFF_HEAD_DIM), lambda bi, hi, qi: (0, 0)),
                  pl.BlockSpec((1, LANES), lambda bi, hi, qi: (0, 0))],
        out_specs=pl.BlockSpec((1, tq, LANES), lambda bi, hi, qi: (bi, qi, hi)),
        out_shape=jax.ShapeDtypeStruct((b, s, DIFF_HEADS * LANES), BF16),
        compiler_params=_cparams(("parallel", "parallel", "arbitrary")),
        name="diff_attn",
    )(scal, qt, kn, vt, pos_row, pos_col, lam4, subln_g)


def _split_dot_f32(a, w_hi, w_lo):
    a_hi = a.astype(BF16)
    a_lo = (a - a_hi.astype(F32)).astype(BF16)
    return (jnp.dot(a_hi, w_hi, preferred_element_type=F32)
            + jnp.dot(a_lo, w_hi, preferred_element_type=F32)
            + jnp.dot(a_hi, w_lo, preferred_element_type=F32))


def _merge_kernel(x_ref, oa_ref, ob_ref, ga_ref, gb_ref, wa_ref, wb_ref, wo_ref, gf_ref, *rest,
                  with_router):
    if with_router:
        wr_hi_ref, wr_lo_ref, xo_ref, h_ref, route_ref = rest
    else:
        xo_ref, h_ref = rest
    ya = jnp.dot(oa_ref[...], wa_ref[...], preferred_element_type=F32)
    yb = jnp.dot(ob_ref[...], wb_ref[...], preferred_element_type=F32)
    merged = ga_ref[...].astype(F32) * ya + gb_ref[...].astype(F32) * yb
    x = x_ref[...] + jnp.dot(merged.astype(BF16), wo_ref[...], preferred_element_type=F32)
    xo_ref[...] = x
    h = _rms(x, gf_ref[...])
    h_ref[...] = h.astype(h_ref.dtype)
    if with_router:
        route_ref[...] = _top2_route(_split_dot_f32(h, wr_hi_ref[...], wr_lo_ref[...]))


def _merge(x, oa, ob, p, wa, wb, wo, gf, router=None, *, tm=512):
    t, d = x.shape
    ga_blk = P_GATE_COL0 // d
    row = lambda i: (i, 0)
    full = lambda a: pl.BlockSpec(a.shape, lambda i: (0,) * a.ndim)
    in_specs = [pl.BlockSpec((tm, d), row), pl.BlockSpec((tm, d), row), pl.BlockSpec((tm, d), row),
                pl.BlockSpec((tm, d), lambda i: (i, ga_blk)),
                pl.BlockSpec((tm, d), lambda i: (i, ga_blk + 1)),
                full(wa), full(wb), full(wo), full(gf)]
    args = [x, oa, ob, p, p, wa, wb, wo, gf]
    out_specs = [pl.BlockSpec((tm, d), row), pl.BlockSpec((tm, d), row)]
    h_dtype = BF16 if router is None else F32
    out_shape = [jax.ShapeDtypeStruct((t, d), F32), jax.ShapeDtypeStruct((t, d), h_dtype)]
    if router is not None:
        in_specs += [full(router[0]), full(router[1])]
        args += list(router)
        out_specs.append(pl.BlockSpec((tm, LANES), row))
        out_shape.append(jax.ShapeDtypeStruct((t, LANES), F32))
    return pl.pallas_call(
        functools.partial(_merge_kernel, with_router=router is not None),
        grid=(t // tm,),
        in_specs=in_specs,
        out_specs=out_specs,
        out_shape=out_shape,
        compiler_params=_cparams(("parallel",)),
        name="merge_router" if router is not None else "merge",
    )(*args)


def _swiglu_act(h, wg, wu):
    g = jnp.dot(h, wg, preferred_element_type=F32)
    u = jnp.dot(h, wu, preferred_element_type=F32)
    return (g * jax.nn.sigmoid(g) * u).astype(BF16)


def _ffn_kernel(x_ref, h_ref, wg_ref, wu_ref, wd_ref, o_ref, acc_ref):
    f = pl.program_id(1)
    y = jnp.dot(_swiglu_act(h_ref[...], wg_ref[...], wu_ref[...]), wd_ref[...],
                preferred_element_type=F32)

    @pl.when(f == 0)
    def _():
        acc_ref[...] = x_ref[...] + y

    @pl.when(f > 0)
    def _():
        acc_ref[...] += y

    @pl.when(f == pl.num_programs(1) - 1)
    def _():
        o_ref[...] = acc_ref[...]


def _ffn(x, h, wg, wu, wd, *, tm=1024, tf=1408):
    t, d = x.shape
    ff = wg.shape[1]
    return pl.pallas_call(
        _ffn_kernel,
        grid=(t // tm, ff // tf),
        in_specs=[pl.BlockSpec((tm, d), lambda i, f: (i, 0)),
                  pl.BlockSpec((tm, d), lambda i, f: (i, 0)),
                  pl.BlockSpec((d, tf), lambda i, f: (0, f)),
                  pl.BlockSpec((d, tf), lambda i, f: (0, f)),
                  pl.BlockSpec((tf, d), lambda i, f: (f, 0))],
        out_specs=pl.BlockSpec((tm, d), lambda i, f: (i, 0)),
        out_shape=jax.ShapeDtypeStruct((t, d), F32),
        scratch_shapes=[pltpu.VMEM((tm, d), F32)],
        compiler_params=_cparams(("parallel", "arbitrary")),
        name="ffn_dense",
    )(x, h, wg, wu, wd)


def _top2_route(logits):
    lane_i = lax.broadcasted_iota(jnp.int32, logits.shape, 1)
    lane = lane_i.astype(F32)
    neg = jnp.float32(-jnp.inf)
    z = jnp.where(lane_i < N_EXPERTS, logits, neg)
    m1 = jnp.max(z, axis=-1, keepdims=True)
    i1 = jnp.min(jnp.where(z == m1, lane, float(LANES)), axis=-1, keepdims=True)
    z2 = jnp.where(lane == i1, neg, z)
    m2 = jnp.max(z2, axis=-1, keepdims=True)
    i2 = jnp.min(jnp.where(z2 == m2, lane, float(LANES)), axis=-1, keepdims=True)
    e2 = jnp.exp(m2 - m1)
    w1 = 1.0 / (1.0 + e2)
    w2 = e2 / (1.0 + e2)
    return jnp.where(lane_i == 0, i1, jnp.where(lane_i == 1, i2,
                     jnp.where(lane_i == 2, w1, jnp.where(lane_i == 3, w2, 0.0))))


def _route_plan(route, tm):
    t = route.shape[0]
    e = route[:, 0:TOP_K].astype(jnp.int32)
    sel = (e[:, :, None] == jnp.arange(N_EXPERTS, dtype=jnp.int32)).astype(jnp.int32).sum(axis=1)
    csum = jnp.cumsum(sel, axis=0)
    padded = (csum[-1] + tm - 1) // tm * tm
    end = jnp.cumsum(padded)
    pos = (end - padded)[e] + jnp.take_along_axis(csum - sel, e, axis=1)
    n_rows = TOP_K * t + N_EXPERTS * tm
    flat = pos.reshape(-1)
    tok = jnp.zeros((n_rows,), jnp.int32).at[flat].set(
        jnp.repeat(jnp.arange(t, dtype=jnp.int32), TOP_K), unique_indices=True)
    n_tiles = n_rows // tm
    row0 = jnp.arange(n_tiles, dtype=jnp.int32) * tm
    valid = (row0 < end[-1]).astype(jnp.int32)
    tile_e = jnp.searchsorted(end, jnp.minimum(row0, end[-1] - 1), side="right").astype(jnp.int32)
    return tok.reshape(n_tiles, 1, tm), tile_e, valid, pos


def _moe_gmm_kernel(tile_e_ref, valid_ref, tok_ref, tok_next_ref, h_hbm, wg_ref, wu_ref, wd_ref,
                    o_ref, xg_ref, xb_ref, acc_ref, sem, *, nf, gather_steps):
    del tile_e_ref
    i = pl.program_id(0)
    f = pl.program_id(1)
    tm = xb_ref.shape[0]
    slot = i % 2

    def row_copy(tok, r, s):
        return pltpu.make_async_copy(h_hbm.at[pl.ds(tok[0, 0, r], 1), :],
                                     xg_ref.at[s, pl.ds(r, 1), :], sem.at[s])

    @pl.when((i == 0) & (f == 0))
    def _():
        def body(r, c):
            row_copy(tok_ref, r, 0).start()
            return c
        lax.fori_loop(0, tm, body, 0, unroll=8)

    valid = valid_ref[i] != 0
    prev_valid = valid_ref[jnp.maximum(i - 1, 0)] != 0

    @pl.when((f == 0) & ((i == 0) | prev_valid))
    def _():
        pltpu.make_async_copy(h_hbm.at[pl.ds(0, tm), :], xg_ref.at[slot], sem.at[slot]).wait()
        xb_ref[...] = xg_ref[slot].astype(BF16)

    def step(request_rows):
        if request_rows:
            rows = tm // gather_steps
            for r in range(rows):
                row_copy(tok_next_ref, f * rows + r, 1 - slot).start()
        y = jnp.dot(_swiglu_act(xb_ref[...], wg_ref[0], wu_ref[0]), wd_ref[0],
                    preferred_element_type=F32)

        @pl.when(f == 0)
        def _():
            acc_ref[...] = y

        @pl.when((f > 0) & (f < nf - 1))
        def _():
            acc_ref[...] += y

        @pl.when(f == nf - 1)
        def _():
            o_ref[...] = acc_ref[...] + y

    pl.when(valid & (f < gather_steps))(functools.partial(step, True))
    pl.when(valid & (f >= gather_steps))(functools.partial(step, False))

    @pl.when(jnp.logical_not(valid) & (f == nf - 1))
    def _():
        o_ref[...] = jnp.zeros_like(o_ref)


def _moe_gmm(tok, tile_e, valid, h, wg, wu, wd, *, tf=1792, gather_steps=2):
    n_tiles, _, tm = tok.shape
    d = h.shape[1]
    ff = wg.shape[2]
    nf = ff // tf
    assert 2 <= gather_steps <= nf and tm % gather_steps == 0
    tok_spec = lambda imap: pl.BlockSpec((1, 1, tm), imap, memory_space=pltpu.SMEM)
    grid_spec = pltpu.PrefetchScalarGridSpec(
        num_scalar_prefetch=2,
        grid=(n_tiles, nf),
        in_specs=[tok_spec(lambda i, f, te, va: (i, 0, 0)),
                  tok_spec(lambda i, f, te, va: (jnp.minimum(i + 1, n_tiles - 1), 0, 0)),
                  pl.BlockSpec(memory_space=pl.ANY),
                  pl.BlockSpec((1, d, tf), lambda i, f, te, va: (te[i], 0, f)),
                  pl.BlockSpec((1, d, tf), lambda i, f, te, va: (te[i], 0, f)),
                  pl.BlockSpec((1, tf, d), lambda i, f, te, va: (te[i], f, 0))],
        out_specs=pl.BlockSpec((tm, d), lambda i, f, te, va: (i, 0)),
        scratch_shapes=[pltpu.VMEM((2, tm, d), F32), pltpu.VMEM((tm, d), BF16), pltpu.VMEM((tm, d), F32),
                        pltpu.SemaphoreType.DMA((2,))],
    )
    return pl.pallas_call(
        functools.partial(_moe_gmm_kernel, nf=nf, gather_steps=gather_steps),
        grid_spec=grid_spec,
        out_shape=jax.ShapeDtypeStruct((n_tiles * tm, d), F32),
        compiler_params=_cparams(("arbitrary", "arbitrary")),
        name="moe_gmm",
    )(tile_e, valid, tok, tok, h, wg, wu, wd)


def _moe_combine_kernel(pos_ref, pos_next_ref, x_ref, route_ref, y_hbm, o_ref, buf_ref, sem):
    i = pl.program_id(0)
    tc = x_ref.shape[0]
    slot = i % 2

    def row_copy(pos, r, s):
        return pltpu.make_async_copy(y_hbm.at[pl.ds(pos[0, 0, r], 1), :],
                                     buf_ref.at[s, pl.ds(r, 1), :], sem.at[s])

    @pl.when(i == 0)
    def _():
        def body(r, c):
            row_copy(pos_ref, r, 0).start()
            return c
        lax.fori_loop(0, TOP_K * tc, body, 0, unroll=8)

    @pl.when(i + 1 < pl.num_programs(0))
    def _():
        for r in range(TOP_K * tc):
            row_copy(pos_next_ref, r, 1 - slot).start()

    pltpu.make_async_copy(y_hbm.at[pl.ds(0, TOP_K * tc), :], buf_ref.at[slot], sem.at[slot]).wait()
    y = buf_ref[slot]
    w = route_ref[...]
    o_ref[...] = (x_ref[...] + w[:, TOP_K:TOP_K + 1] * y[0:tc]
                  + w[:, TOP_K + 1:TOP_K + 2] * y[tc:2 * tc])


def _moe_combine(x, route, y_rows, pos, *, tc=512):
    t, d = x.shape
    n = t // tc
    pos_t = pos.reshape(n, tc, TOP_K).transpose(0, 2, 1).reshape(n, 1, TOP_K * tc)
    pos_spec = lambda imap: pl.BlockSpec((1, 1, TOP_K * tc), imap, memory_space=pltpu.SMEM)
    return pl.pallas_call(
        _moe_combine_kernel,
        grid=(n,),
        in_specs=[pos_spec(lambda i: (i, 0, 0)),
                  pos_spec(lambda i: (jnp.minimum(i + 1, n - 1), 0, 0)),
                  pl.BlockSpec((tc, d), lambda i: (i, 0)),
                  pl.BlockSpec((tc, LANES), lambda i: (i, 0)),
                  pl.BlockSpec(memory_space=pl.ANY)],
        out_specs=pl.BlockSpec((tc, d), lambda i: (i, 0)),
        out_shape=jax.ShapeDtypeStruct((t, d), F32),
        scratch_shapes=[pltpu.VMEM((2, TOP_K * tc, d), F32), pltpu.SemaphoreType.DMA((2,))],
        compiler_params=_cparams(("arbitrary",)),
        name="moe_combine",
    )(pos_t, pos_t, x, route, y_rows)


def _row(v, reps=1):
    return jnp.tile(v.astype(F32), reps).reshape(1, -1)


def kernel(x, positions, attn_norm_g, w_in, w_gate, mla_q_norm_g, mla_kv_norm_g, mla_w_uq, mla_w_ukv, mla_qn_nope_g, mla_qn_rope_g, mla_kn_nope_g, mla_kn_rope_g, diff_qn_g, diff_kn_g, diff_lam_q1, diff_lam_k1, diff_lam_q2, diff_lam_k2, diff_subln_g, w_branch_a, w_branch_b, w_out, ffn_norm_g, dense_w_gate, dense_w_up, dense_w_down, moe_w_router, moe_w_gate, moe_w_up, moe_w_down):
    batch, seq, d = x.shape
    t = batch * seq
    xf = x.reshape(t, d)
    pos_f = positions.astype(F32)
    cos, sin = _rope_tables(pos_f.reshape(t, 1))
    pos_col = jnp.broadcast_to(pos_f.reshape(batch, seq, 1), (batch, seq, LANES))
    pos_row = pos_f.reshape(batch, 1, seq)
    slopes = 2.0 ** (-8.0 * jnp.arange(1, DIFF_HEADS + 1, dtype=F32) / DIFF_HEADS)
    n_in_head = MLA_Q_LORA + MLA_KV_LORA + MLA_ROPE

    for layer in range(DEPTH):
        w1 = jnp.concatenate(
            [w_in[layer][:, :n_in_head], jnp.zeros((d, P_BLK - n_in_head), F32),
             w_in[layer][:, n_in_head:], w_gate[layer]], axis=1).astype(BF16)
        wuq = mla_w_uq[layer].reshape(MLA_Q_LORA, MLA_HEADS, MLA_NOPE + MLA_ROPE)
        wuq = jnp.concatenate([wuq[:, :, :MLA_NOPE].reshape(MLA_Q_LORA, -1),
                               wuq[:, :, MLA_NOPE:].reshape(MLA_Q_LORA, -1)], axis=1).astype(BF16)
        wukv = mla_w_ukv[layer].reshape(MLA_KV_LORA, MLA_HEADS, MLA_NOPE + MLA_V)
        wukv = jnp.concatenate([wukv[:, :, :MLA_NOPE].reshape(MLA_KV_LORA, -1),
                                wukv[:, :, MLA_NOPE:].reshape(MLA_KV_LORA, -1)], axis=1).astype(BF16)
        gkr = jnp.concatenate([mla_kn_rope_g[layer].astype(F32), jnp.zeros((LANES - MLA_ROPE,), F32)]).reshape(1, -1)

        p = _norm_proj(xf, _row(attn_norm_g[layer]), w1)
        q_bound = MLA_Q_SCALE * jnp.sqrt(_norm_bound(MLA_NOPE, mla_qn_nope_g[layer]) ** 2
                                         + _norm_bound(MLA_ROPE, mla_qn_rope_g[layer]) ** 2)
        k_bound = jnp.sqrt(_norm_bound(MLA_NOPE, mla_kn_nope_g[layer]) ** 2
                           + _norm_bound(MLA_ROPE, mla_kn_rope_g[layer]) ** 2)
        shift, use_bounded = _score_shift(q_bound, k_bound)
        shift_row = jnp.where(jnp.arange(LANES) % 64 == 0, -shift, 0.0).astype(F32).reshape(1, LANES)
        q_m, k_m, v_m = _mla_prep(
            p, cos, sin, _row(mla_q_norm_g[layer]), _row(mla_kv_norm_g[layer]), wuq, wukv,
            _row(mla_qn_nope_g[layer]), _row(mla_qn_rope_g[layer], 2), _row(mla_kn_nope_g[layer]), gkr,
            shift_row, batch=batch, seq=seq)
        o_a = _mla_attn(use_bounded.astype(jnp.int32).reshape(1), q_m, k_m, v_m).reshape(t, -1)

        lambda_init = 0.8 - 0.6 * math.exp(-0.3 * layer)
        qt_d, kn, vt_d = _diff_prep(p, _row(diff_qn_g[layer], 2), _row(diff_kn_g[layer], 2),
                                  batch=batch, seq=seq)
        lam4 = jnp.stack([diff_lam_q1[layer], diff_lam_k1[layer],
                          diff_lam_q2[layer], diff_lam_k2[layer]]).astype(F32)
        d_shift, d_bounded = _score_shift(DIFF_Q_SCALE * _norm_bound(DIFF_HEAD_DIM, diff_qn_g[layer]),
                                          _norm_bound(DIFF_HEAD_DIM, diff_kn_g[layer]))
        scal = jnp.concatenate([slopes, d_shift.reshape(1), d_bounded.astype(F32).reshape(1)])
        o_b = _diff_attn(scal, qt_d, kn.reshape(batch, seq, -1), vt_d,
                         pos_row, pos_col, lam4, _row(diff_subln_g[layer]),
                         lambda_init=lambda_init).reshape(t, -1)

        j = layer // 2
        wa, wb, wo = (w.astype(BF16) for w in (w_branch_a[layer], w_branch_b[layer], w_out[layer]))
        if layer % 2 == 0:
            xf, h = _merge(xf, o_a, o_b, p, wa, wb, wo, _row(ffn_norm_g[layer]))
            xf = _ffn(xf, h, dense_w_gate[j].astype(BF16), dense_w_up[j].astype(BF16),
                      dense_w_down[j].astype(BF16))
        else:
            wr = jnp.pad(moe_w_router[j].astype(F32), ((0, 0), (0, LANES - N_EXPERTS)))
            wr_hi = wr.astype(BF16)
            wr_lo = (wr - wr_hi.astype(F32)).astype(BF16)
            xf, h, route = _merge(xf, o_a, o_b, p, wa, wb, wo, _row(ffn_norm_g[layer]),
                                  router=(wr_hi, wr_lo))
            tok, tile_e, valid, pos = _route_plan(route, MOE_TILE_ROWS)
            y_rows = _moe_gmm(tok, tile_e, valid, h, moe_w_gate[j].astype(BF16),
                              moe_w_up[j].astype(BF16), moe_w_down[j].astype(BF16))
            xf = _moe_combine(xf, route, y_rows, pos)
    return xf.reshape(batch, seq, d)
```

```python
import functools
import math

import jax
import jax.numpy as jnp
from jax import lax
from jax.experimental import pallas as pl
from jax.experimental.pallas import tpu as pltpu

D_MODEL = 1024
DEPTH = 2
MLA_HEADS = 8
MLA_Q_LORA = 512
MLA_KV_LORA = 256
MLA_NOPE = 128
MLA_ROPE = 64
MLA_V = 128
ROPE_THETA = 10000.0
DIFF_HEADS = 8
DIFF_HEAD_DIM = 64
D_FF_DENSE = 2816
N_EXPERTS = 8
TOP_K = 2
D_FF_EXPERT = 3584
EPS = 1e-6

LANES = 128
LOG2E = 1.4426950408889634
MLA_Q_SCALE = (MLA_NOPE + MLA_ROPE) ** -0.5 * LOG2E
DIFF_Q_SCALE = DIFF_HEAD_DIM ** -0.5 * LOG2E
NEG_BIG = -1e30
BF16 = jnp.bfloat16
F32 = jnp.float32

P_BLK = DIFF_HEADS * 2 * DIFF_HEAD_DIM
P_KR_COL0 = MLA_Q_LORA + MLA_KV_LORA
P_DQ_BLK = 1
P_DK_BLK = 2
P_DV_BLK = 3
P_GATE_COL0 = 4 * P_BLK
MOE_TILE_ROWS = 512
VMEM_LIMIT = 52 * 1024 * 1024


def _cparams(sem):
    return pltpu.CompilerParams(dimension_semantics=sem, vmem_limit_bytes=VMEM_LIMIT)


def _rms(x, gain):
    ms = jnp.mean(x * x, axis=-1, keepdims=True)
    return x * lax.rsqrt(ms + EPS) * gain


def _ms_half_lanes(xb):
    lane = lax.broadcasted_iota(jnp.int32, xb.shape, 1)
    lo = lane < 64
    sq = xb * xb
    s_lo = jnp.sum(jnp.where(lo, sq, 0.0), axis=-1, keepdims=True)
    s_hi = jnp.sum(jnp.where(lo, 0.0, sq), axis=-1, keepdims=True)
    return jnp.where(lo, s_lo, s_hi) * (1.0 / 64.0)


def _norm_proj_kernel(x_ref, g_ref, w_ref, o_ref, h_ref, *, gate_tile0):
    j = pl.program_id(1)

    @pl.when(j == 0)
    def _():
        h_ref[...] = _rms(x_ref[...], g_ref[...]).astype(BF16)

    y = jnp.dot(h_ref[...], w_ref[...], preferred_element_type=F32)

    @pl.when(j < gate_tile0)
    def _():
        o_ref[...] = y.astype(BF16)

    @pl.when(j >= gate_tile0)
    def _():
        o_ref[...] = jax.nn.sigmoid(y).astype(BF16)


def _norm_proj(x, gain, w, *, tm=1024, tn=2048):
    t, d = x.shape
    n = w.shape[1]
    return pl.pallas_call(
        functools.partial(_norm_proj_kernel, gate_tile0=P_GATE_COL0 // tn),
        grid=(t // tm, n // tn),
        in_specs=[
            pl.BlockSpec((tm, d), lambda i, j: (i, 0)),
            pl.BlockSpec((1, d), lambda i, j: (0, 0)),
            pl.BlockSpec((d, tn), lambda i, j: (0, j)),
        ],
        out_specs=pl.BlockSpec((tm, tn), lambda i, j: (i, j)),
        out_shape=jax.ShapeDtypeStruct((t, n), BF16),
        scratch_shapes=[pltpu.VMEM((tm, d), BF16)],
        compiler_params=_cparams(("parallel", "arbitrary")),
        name="norm_proj",
    )(x, gain, w)


def _rope_table_kernel(pos_ref, freq_ref, cos_ref, sin_ref):
    ang = pos_ref[...] * freq_ref[...]
    lane = lax.broadcasted_iota(jnp.int32, ang.shape, 1)
    first_half = (lane % 64) < 32
    cos_ref[...] = jnp.cos(ang)
    sin_ref[...] = jnp.where(first_half, -1.0, 1.0) * jnp.sin(ang)


def _rope_tables(pos_col, *, tm=2048):
    t = pos_col.shape[0]
    half = MLA_ROPE // 2
    inv_freq = ROPE_THETA ** (-jnp.arange(half, dtype=F32) / half)
    freq = jnp.tile(inv_freq, LANES // half).reshape(1, LANES)
    return pl.pallas_call(
        _rope_table_kernel,
        grid=(t // tm,),
        in_specs=[pl.BlockSpec((tm, 1), lambda i: (i, 0)),
                  pl.BlockSpec((1, LANES), lambda i: (0, 0))],
        out_specs=[pl.BlockSpec((tm, LANES), lambda i: (i, 0))] * 2,
        out_shape=[jax.ShapeDtypeStruct((t, LANES), F32)] * 2,
        compiler_params=_cparams(("parallel",)),
        name="rope_tables",
    )(pos_col, freq)


def _rope(n, cos, sin_signed):
    w = n.shape[-1]
    lane = lax.broadcasted_iota(jnp.int32, n.shape, 1)
    first_half = (lane % 64) < 32
    swapped = jnp.where(first_half, pltpu.roll(n, w - 32, 1), pltpu.roll(n, 32, 1))
    return n * cos + swapped * sin_signed


def _mla_prep_kernel(p_ref, cos_ref, sin_ref, gq_ref, gkv_ref, wuq_ref, wukv_ref,
                     gqn_ref, gqr_ref, gkn_ref, gkr_ref, shift_ref, q_ref, k_ref, v_ref, *, q_scale):
    cq = p_ref[:, 0:MLA_Q_LORA].astype(F32)
    ckv = p_ref[:, MLA_Q_LORA:MLA_Q_LORA + MLA_KV_LORA].astype(F32)
    kr = p_ref[:, P_KR_COL0:P_KR_COL0 + LANES].astype(F32)
    cos = cos_ref[...]
    sin = sin_ref[...]

    q = jnp.dot(_rms(cq, gq_ref[...]).astype(BF16), wuq_ref[...], preferred_element_type=F32)
    kv = jnp.dot(_rms(ckv, gkv_ref[...]).astype(BF16), wukv_ref[...], preferred_element_type=F32)

    n_nope = MLA_HEADS * MLA_NOPE
    kr_ms = jnp.sum(kr * kr, axis=-1, keepdims=True) * (1.0 / MLA_ROPE)
    kr_n = kr * lax.rsqrt(kr_ms + EPS) * gkr_ref[...]
    kr_r = _rope(kr_n, cos, sin)
    kr_dup = kr_r + pltpu.roll(kr_r, 64, 1)

    lane = lax.broadcasted_iota(jnp.int32, (q.shape[0], LANES), 1)
    one_hot = jnp.where(lane % 64 == 0, 1.0, 0.0)
    for pair in range(MLA_HEADS // 2):
        qr = q[:, n_nope + pair * LANES:n_nope + (pair + 1) * LANES]
        qr_n = qr * lax.rsqrt(_ms_half_lanes(qr) + EPS) * gqr_ref[...]
        qr_r = _rope(qr_n, cos, sin) * q_scale
        for sub in range(2):
            h = 2 * pair + sub
            keep = (lane < 64) if sub == 0 else (lane >= 64)
            qn = _rms(q[:, h * MLA_NOPE:(h + 1) * MLA_NOPE], gqn_ref[...]) * q_scale
            q_ref[0, h, 0:LANES, :] = qn.T.astype(BF16)
            q_ref[0, h, LANES:2 * LANES, :] = jnp.where(keep, qr_r, shift_ref[...]).T.astype(BF16)
            kn = _rms(kv[:, h * MLA_NOPE:(h + 1) * MLA_NOPE], gkn_ref[...])
            k_ref[0, h, :, 0:LANES] = kn.astype(BF16)
            k_ref[0, h, :, LANES:2 * LANES] = jnp.where(keep, kr_dup, one_hot).astype(BF16)
            v_ref[0, h, :, :] = kv[:, n_nope + h * MLA_V:n_nope + (h + 1) * MLA_V].T.astype(BF16)


def _mla_prep(p, cos, sin, gq, gkv, wuq, wukv, gqn, gqr, gkn, gkr, shift_row, *, batch, seq, tm=512):
    nb = seq // tm
    full = lambda a: pl.BlockSpec(a.shape, lambda b, i: (0,) * a.ndim)
    row = lambda b, i: (b * nb + i, 0)
    hshape = lambda w: jax.ShapeDtypeStruct((batch, MLA_HEADS, seq, w), BF16)
    hspec = lambda w: pl.BlockSpec((1, MLA_HEADS, tm, w), lambda b, i: (b, 0, i, 0))
    tshape = lambda w: jax.ShapeDtypeStruct((batch, MLA_HEADS, w, seq), BF16)
    tspec = lambda w: pl.BlockSpec((1, MLA_HEADS, w, tm), lambda b, i: (b, 0, 0, i))
    return pl.pallas_call(
        functools.partial(_mla_prep_kernel, q_scale=MLA_Q_SCALE),
        grid=(batch, nb),
        in_specs=[pl.BlockSpec((tm, P_BLK), row),
                  pl.BlockSpec((tm, LANES), row), pl.BlockSpec((tm, LANES), row),
                  full(gq), full(gkv), full(wuq), full(wukv),
                  full(gqn), full(gqr), full(gkn), full(gkr), full(shift_row)],
        out_specs=[tspec(2 * LANES), hspec(2 * LANES), tspec(MLA_V)],
        out_shape=[tshape(2 * LANES), hshape(2 * LANES), tshape(MLA_V)],
        compiler_params=_cparams(("parallel", "parallel")),
        name="mla_prep",
    )(p, cos, sin, gq, gkv, wuq, wukv, gqn, gqr, gkn, gkr, shift_row)


BOUND_MARGIN = 1.02
MAX_SHIFT_SPAN = 100.0


def _norm_bound(n, gain):
    return math.sqrt(n) * jnp.max(jnp.abs(gain.astype(F32)))


def _score_shift(q_bound, k_bound):
    shift = (BOUND_MARGIN * q_bound * k_bound).astype(BF16).astype(F32)
    return shift, (2.0 * shift <= MAX_SHIFT_SPAN)


def _sublane_partial_sum(p):
    return jnp.sum(p.reshape(p.shape[0] // 8, 8, p.shape[1]), axis=0)


def _online_update_t(s, m, l):
    m_new = jnp.maximum(m, jnp.max(s, axis=0, keepdims=True))
    alpha = jnp.exp2(m - m_new)
    p = jnp.exp2(s - m_new)
    return p, alpha, m_new, alpha * l + jnp.sum(p, axis=0, keepdims=True)


def _mla_attn_kernel(flag_ref, qt_ref, k_ref, vt_ref, o_ref, *, tk, nk, unroll):
    tq = qt_ref.shape[3]
    qt = qt_ref[0, 0]

    def scores(k):
        return jnp.dot(k, qt, preferred_element_type=F32)

    def chunk(j):
        off = pl.multiple_of(j * tk, tk)
        return k_ref[0, 0, pl.ds(off, tk), :], vt_ref[0, 0, :, pl.ds(off, tk)]

    def bounded():
        def body(j, carry):
            l, acc = carry
            k, vt = chunk(j)
            p = jnp.exp2(scores(k))
            return (l + _sublane_partial_sum(p),
                    acc + jnp.dot(vt, p.astype(BF16), preferred_element_type=F32))

        init = (jnp.zeros((8, tq), F32), jnp.zeros((MLA_V, tq), F32))
        l, acc = lax.fori_loop(0, nk, body, init, unroll=unroll)
        return acc / jnp.sum(l, axis=0, keepdims=True)

    def online():
        def body(j, carry):
            m, l, acc = carry
            k, vt = chunk(j)
            p, alpha, m, l = _online_update_t(scores(k), m, l)
            return m, l, alpha * acc + jnp.dot(vt, p.astype(BF16), preferred_element_type=F32)

        init = (jnp.full((1, tq), NEG_BIG, F32), jnp.zeros((1, tq), F32), jnp.zeros((MLA_V, tq), F32))
        _, l, acc = lax.fori_loop(0, nk, body, init)
        return acc / l

    o_ref[0] = lax.cond(flag_ref[0] != 0, bounded, online).T.astype(BF16)


def _mla_attn(flag, qt, k, vt, *, tq=1024, tk=1024, unroll=8):
    b, h, s, w = k.shape
    return pl.pallas_call(
        functools.partial(_mla_attn_kernel, tk=tk, nk=s // tk, unroll=unroll),
        grid=(b, h, s // tq),
        in_specs=[pl.BlockSpec(memory_space=pltpu.SMEM),
                  pl.BlockSpec((1, 1, w, tq), lambda bi, hi, qi: (bi, hi, 0, qi)),
                  pl.BlockSpec((1, 1, s, w), lambda bi, hi, qi: (bi, hi, 0, 0)),
                  pl.BlockSpec((1, 1, MLA_V, s), lambda bi, hi, qi: (bi, hi, 0, 0))],
        out_specs=pl.BlockSpec((1, tq, MLA_V), lambda bi, hi, qi: (bi, qi, hi)),
        out_shape=jax.ShapeDtypeStruct((b, s, h * MLA_V), BF16),
        compiler_params=_cparams(("parallel", "parallel", "arbitrary")),
        name="mla_attn",
    )(flag, qt, k, vt)


def _diff_prep_kernel(q_ref, k_ref, v_ref, gq_ref, gk_ref, qt_ref, kn_ref, vt_ref, *, q_scale):
    for c in range(DIFF_HEADS):
        sl = slice(c * LANES, (c + 1) * LANES)
        q = q_ref[:, sl].astype(F32)
        k = k_ref[:, sl].astype(F32)
        qt_ref[0, c] = (q * lax.rsqrt(_ms_half_lanes(q) + EPS) * (gq_ref[...] * q_scale)).T.astype(BF16)
        kn_ref[:, sl] = (k * lax.rsqrt(_ms_half_lanes(k) + EPS) * gk_ref[...]).astype(BF16)
        vt_ref[0, c] = v_ref[:, sl].astype(F32).T.astype(BF16)


def _diff_prep(p, gq, gk, *, batch, seq, tm=1024):
    t = p.shape[0]
    nb = seq // tm
    w = DIFF_HEADS * 2 * DIFF_HEAD_DIM
    row = lambda blk: (lambda b, i: (b * nb + i, blk))
    gain = pl.BlockSpec((1, LANES), lambda b, i: (0, 0))
    tshape = jax.ShapeDtypeStruct((batch, DIFF_HEADS, LANES, seq), BF16)
    tspec = pl.BlockSpec((1, DIFF_HEADS, LANES, tm), lambda b, i: (b, 0, 0, i))
    return pl.pallas_call(
        functools.partial(_diff_prep_kernel, q_scale=DIFF_Q_SCALE),
        grid=(batch, nb),
        in_specs=[pl.BlockSpec((tm, w), row(P_DQ_BLK)), pl.BlockSpec((tm, w), row(P_DK_BLK)),
                  pl.BlockSpec((tm, w), row(P_DV_BLK)), gain, gain],
        out_specs=[tspec, pl.BlockSpec((tm, w), row(0)), tspec],
        out_shape=[tshape, jax.ShapeDtypeStruct((t, w), BF16), tshape],
        compiler_params=_cparams(("parallel", "parallel")),
        name="diff_prep",
    )(p, p, p, gq, gk)


def _diff_attn_kernel(scal_ref, qt_ref, k_ref, vt_ref, pq_ref, pk_ref, lam_ref, g_ref, o_ref,
                      *, tk, nk, unroll, lambda_init):
    qt = qt_ref[0, 0]
    tq = qt.shape[1]
    row = lax.broadcasted_iota(jnp.int32, qt.shape, 0)
    zero = jnp.zeros_like(qt)
    q0t = jnp.where(row < DIFF_HEAD_DIM, qt, zero)
    q1t = jnp.where(row < DIFF_HEAD_DIM, zero, qt)
    pq = pq_ref[0]
    slope = scal_ref[pl.program_id(1)] * LOG2E
    shift = scal_ref[DIFF_HEADS]
    use_bounded = scal_ref[DIFF_HEADS + 1]

    spq = slope * pq

    def chunk(j, shifted):
        off = pl.multiple_of(j * tk, tk)
        k = k_ref[0, pl.ds(off, tk), :]
        vt = vt_ref[0, 0, :, pl.ds(off, tk)]
        spk = slope * pk_ref[0, pl.ds(off, tk), :]
        bias = jnp.abs(jnp.concatenate([spk] * (tq // LANES), axis=1) - spq)
        return k, vt, (bias + shift if shifted else bias)

    def scores(k, qt):
        return jnp.dot(k, qt, preferred_element_type=F32)

    def pv(vt, p):
        return jnp.dot(vt, p.astype(BF16), preferred_element_type=F32)

    def bounded():
        def body(j, carry):
            l0, a0, l1, a1 = carry
            k, vt, bias = chunk(j, False)
            p0 = jnp.exp2(scores(k, q0t) - bias)
            p1 = jnp.exp2(scores(k, q1t) - bias)
            return (l0 + _sublane_partial_sum(p0), a0 + pv(vt, p0),
                    l1 + _sublane_partial_sum(p1), a1 + pv(vt, p1))

        zl = jnp.zeros((8, tq), F32)
        za = jnp.zeros((LANES, tq), F32)
        l0, a0, l1, a1 = lax.fori_loop(0, nk, body, (zl, za, zl, za), unroll=unroll)
        return (a0 / jnp.sum(l0, axis=0, keepdims=True), a1 / jnp.sum(l1, axis=0, keepdims=True))

    def online():
        def body(j, carry):
            m0, l0, a0, m1, l1, a1 = carry
            k, vt, bias = chunk(j, True)
            p0, al0, m0, l0 = _online_update_t(scores(k, q0t) - bias, m0, l0)
            a0 = al0 * a0 + pv(vt, p0)
            p1, al1, m1, l1 = _online_update_t(scores(k, q1t) - bias, m1, l1)
            a1 = al1 * a1 + pv(vt, p1)
            return m0, l0, a0, m1, l1, a1

        neg = jnp.full((1, tq), NEG_BIG, F32)
        z1 = jnp.zeros((1, tq), F32)
        za = jnp.zeros((LANES, tq), F32)
        _, l0, a0, _, l1, a1 = lax.fori_loop(0, nk, body, (neg, z1, za, neg, z1, za))
        return a0 / l0, a1 / l1

    o0, o1 = lax.cond(use_bounded != 0.0, bounded, online)
    lam = (jnp.exp(jnp.sum(lam_ref[0:1, :] * lam_ref[1:2, :], axis=-1, keepdims=True))
           - jnp.exp(jnp.sum(lam_ref[2:3, :] * lam_ref[3:4, :], axis=-1, keepdims=True))
           + lambda_init)
    o = (o0 - lam * o1).T
    o_ref[0] = (_rms(o, g_ref[...]) * (1.0 - lambda_init)).astype(BF16)


def _diff_attn(scal, qt, kn, vt, pos_row, pos_col, lam4, subln_g, *, lambda_init, tq=256, tk=1024, unroll=8):
    b, s, _ = kn.shape
    return pl.pallas_call(
        functools.partial(_diff_attn_kernel, tk=tk, nk=s // tk, unroll=unroll, lambda_init=lambda_init),
        grid=(b, DIFF_HEADS, s // tq),
        in_specs=[pl.BlockSpec(memory_space=pltpu.SMEM),
                  pl.BlockSpec((1, 1, LANES, tq), lambda bi, hi, qi: (bi, hi, 0, qi)),
                  pl.BlockSpec((1, s, LANES), lambda bi, hi, qi: (bi, 0, hi)),
                  pl.BlockSpec((1, 1, LANES, s), lambda bi, hi, qi: (bi, hi, 0, 0)),
                  pl.BlockSpec((1, 1, tq), lambda bi, hi, qi: (bi, 0, qi)),
                  pl.BlockSpec((1, s, LANES), lambda bi, hi, qi: (bi, 0, 0)),
                  pl.BlockSpec((4, DIFF_HEAD_DIM), lambda bi, hi, qi: (0, 0)),
                  pl.BlockSpec((1, LANES), lambda bi, hi, qi: (0, 0))],
        out_specs=pl.BlockSpec((1, tq, LANES), lambda bi, hi, qi: (bi, qi, hi)),
        out_shape=jax.ShapeDtypeStruct((b, s, DIFF_HEADS * LANES), BF16),
        compiler_params=_cparams(("parallel", "parallel", "arbitrary")),
        name="diff_attn",
    )(scal, qt, kn, vt, pos_row, pos_col, lam4, subln_g)


def _split_dot_f32(a, w_hi, w_lo):
    a_hi = a.astype(BF16)
    a_lo = (a - a_hi.astype(F32)).astype(BF16)
    return (jnp.dot(a_hi, w_hi, preferred_element_type=F32)
            + jnp.dot(a_lo, w_hi, preferred_element_type=F32)
            + jnp.dot(a_hi, w_lo, preferred_element_type=F32))


def _merge_kernel(x_ref, oa_ref, ob_ref, ga_ref, gb_ref, wa_ref, wb_ref, wo_ref, gf_ref, *rest,
                  with_router):
    if with_router:
        wr_hi_ref, wr_lo_ref, xo_ref, h_ref, route_ref = rest
    else:
        xo_ref, h_ref = rest
    ya = jnp.dot(oa_ref[...], wa_ref[...], preferred_element_type=F32)
    yb = jnp.dot(ob_ref[...], wb_ref[...], preferred_element_type=F32)
    merged = ga_ref[...].astype(F32) * ya + gb_ref[...].astype(F32) * yb
    x = x_ref[...] + jnp.dot(merged.astype(BF16), wo_ref[...], preferred_element_type=F32)
    xo_ref[...] = x
    h = _rms(x, gf_ref[...])
    h_ref[...] = h.astype(h_ref.dtype)
    if with_router:
        route_ref[...] = _top2_route(_split_dot_f32(h, wr_hi_ref[...], wr_lo_ref[...]))


def _merge(x, oa, ob, p, wa, wb, wo, gf, router=None, *, tm=512):
    t, d = x.shape
    ga_blk = P_GATE_COL0 // d
    row = lambda i: (i, 0)
    full = lambda a: pl.BlockSpec(a.shape, lambda i: (0,) * a.ndim)
    in_specs = [pl.BlockSpec((tm, d), row), pl.BlockSpec((tm, d), row), pl.BlockSpec((tm, d), row),
                pl.BlockSpec((tm, d), lambda i: (i, ga_blk)),
                pl.BlockSpec((tm, d), lambda i: (i, ga_blk + 1)),
                full(wa), full(wb), full(wo), full(gf)]
    args = [x, oa, ob, p, p, wa, wb, wo, gf]
    out_specs = [pl.BlockSpec((tm, d), row), pl.BlockSpec((tm, d), row)]
    h_dtype = BF16 if router is None else F32
    out_shape = [jax.ShapeDtypeStruct((t, d), F32), jax.ShapeDtypeStruct((t, d), h_dtype)]
    if router is not None:
        in_specs += [full(router[0]), full(router[1])]
        args += list(router)
        out_specs.append(pl.BlockSpec((tm, LANES), row))
        out_shape.append(jax.ShapeDtypeStruct((t, LANES), F32))
    return pl.pallas_call(
        functools.partial(_merge_kernel, with_router=router is not None),
        grid=(t // tm,),
        in_specs=in_specs,
        out_specs=out_specs,
        out_shape=out_shape,
        compiler_params=_cparams(("parallel",)),
        name="merge_router" if router is not None else "merge",
    )(*args)


def _swiglu_act(h, wg, wu):
    g = jnp.dot(h, wg, preferred_element_type=F32)
    u = jnp.dot(h, wu, preferred_element_type=F32)
    return (g * jax.nn.sigmoid(g) * u).astype(BF16)


def _ffn_kernel(x_ref, h_ref, wg_ref, wu_ref, wd_ref, o_ref, acc_ref):
    f = pl.program_id(1)
    y = jnp.dot(_swiglu_act(h_ref[...], wg_ref[...], wu_ref[...]), wd_ref[...],
                preferred_element_type=F32)

    @pl.when(f == 0)
    def _():
        acc_ref[...] = x_ref[...] + y

    @pl.when(f > 0)
    def _():
        acc_ref[...] += y

    @pl.when(f == pl.num_programs(1) - 1)
    def _():
        o_ref[...] = acc_ref[...]


def _ffn(x, h, wg, wu, wd, *, tm=1024, tf=1408):
    t, d = x.shape
    ff = wg.shape[1]
    return pl.pallas_call(
        _ffn_kernel,
        grid=(t // tm, ff // tf),
        in_specs=[pl.BlockSpec((tm, d), lambda i, f: (i, 0)),
                  pl.BlockSpec((tm, d), lambda i, f: (i, 0)),
                  pl.BlockSpec((d, tf), lambda i, f: (0, f)),
                  pl.BlockSpec((d, tf), lambda i, f: (0, f)),
                  pl.BlockSpec((tf, d), lambda i, f: (f, 0))],
        out_specs=pl.BlockSpec((tm, d), lambda i, f: (i, 0)),
        out_shape=jax.ShapeDtypeStruct((t, d), F32),
        scratch_shapes=[pltpu.VMEM((tm, d), F32)],
        compiler_params=_cparams(("parallel", "arbitrary")),
        name="ffn_dense",
    )(x, h, wg, wu, wd)


def _top2_route(logits):
    lane_i = lax.broadcasted_iota(jnp.int32, logits.shape, 1)
    lane = lane_i.astype(F32)
    neg = jnp.float32(-jnp.inf)
    z = jnp.where(lane_i < N_EXPERTS, logits, neg)
    m1 = jnp.max(z, axis=-1, keepdims=True)
    i1 = jnp.min(jnp.where(z == m1, lane, float(LANES)), axis=-1, keepdims=True)
    z2 = jnp.where(lane == i1, neg, z)
    m2 = jnp.max(z2, axis=-1, keepdims=True)
    i2 = jnp.min(jnp.where(z2 == m2, lane, float(LANES)), axis=-1, keepdims=True)
    e2 = jnp.exp(m2 - m1)
    w1 = 1.0 / (1.0 + e2)
    w2 = e2 / (1.0 + e2)
    return jnp.where(lane_i == 0, i1, jnp.where(lane_i == 1, i2,
                     jnp.where(lane_i == 2, w1, jnp.where(lane_i == 3, w2, 0.0))))


def _route_plan(route, tm):
    t = route.shape[0]
    e = route[:, 0:TOP_K].astype(jnp.int32)
    sel = (e[:, :, None] == jnp.arange(N_EXPERTS, dtype=jnp.int32)).astype(jnp.int32).sum(axis=1)
    csum = jnp.cumsum(sel, axis=0)
    padded = (csum[-1] + tm - 1) // tm * tm
    end = jnp.cumsum(padded)
    pos = (end - padded)[e] + jnp.take_along_axis(csum - sel, e, axis=1)
    n_rows = TOP_K * t + N_EXPERTS * tm
    flat = pos.reshape(-1)
    tok = jnp.zeros((n_rows,), jnp.int32).at[flat].set(
        jnp.repeat(jnp.arange(t, dtype=jnp.int32), TOP_K), unique_indices=True)
    n_tiles = n_rows // tm
    row0 = jnp.arange(n_tiles, dtype=jnp.int32) * tm
    valid = (row0 < end[-1]).astype(jnp.int32)
    tile_e = jnp.searchsorted(end, jnp.minimum(row0, end[-1] - 1), side="right").astype(jnp.int32)
    return tok.reshape(n_tiles, 1, tm), tile_e, valid, pos


def _moe_gmm_kernel(tile_e_ref, valid_ref, tok_ref, tok_next_ref, h_hbm, wg_ref, wu_ref, wd_ref,
                    o_ref, xg_ref, xb_ref, acc_ref, sem, *, nf, gather_steps):
    del tile_e_ref
    i = pl.program_id(0)
    f = pl.program_id(1)
    tm = xb_ref.shape[0]
    slot = i % 2

    def row_copy(tok, r, s):
        return pltpu.make_async_copy(h_hbm.at[pl.ds(tok[0, 0, r], 1), :],
                                     xg_ref.at[s, pl.ds(r, 1), :], sem.at[s])

    @pl.when((i == 0) & (f == 0))
    def _():
        def body(r, c):
            row_copy(tok_ref, r, 0).start()
            return c
        lax.fori_loop(0, tm, body, 0, unroll=8)

    valid = valid_ref[i] != 0
    prev_valid = valid_ref[jnp.maximum(i - 1, 0)] != 0

    @pl.when((f == 0) & ((i == 0) | prev_valid))
    def _():
        pltpu.make_async_copy(h_hbm.at[pl.ds(0, tm), :], xg_ref.at[slot], sem.at[slot]).wait()
        xb_ref[...] = xg_ref[slot].astype(BF16)

    def step(request_rows):
        if request_rows:
            rows = tm // gather_steps
            for r in range(rows):
                row_copy(tok_next_ref, f * rows + r, 1 - slot).start()
        y = jnp.dot(_swiglu_act(xb_ref[...], wg_ref[0], wu_ref[0]), wd_ref[0],
                    preferred_element_type=F32)

        @pl.when(f == 0)
        def _():
            acc_ref[...] = y

        @pl.when((f > 0) & (f < nf - 1))
        def _():
            acc_ref[...] += y

        @pl.when(f == nf - 1)
        def _():
            o_ref[...] = acc_ref[...] + y

    pl.when(valid & (f < gather_steps))(functools.partial(step, True))
    pl.when(valid & (f >= gather_steps))(functools.partial(step, False))

    @pl.when(jnp.logical_not(valid) & (f == nf - 1))
    def _():
        o_ref[...] = jnp.zeros_like(o_ref)


def _moe_gmm(tok, tile_e, valid, h, wg, wu, wd, *, tf=1792, gather_steps=2):
    n_tiles, _, tm = tok.shape
    d = h.shape[1]
    ff = wg.shape[2]
    nf = ff // tf
    assert 2 <= gather_steps <= nf and tm % gather_steps == 0
    tok_spec = lambda imap: pl.BlockSpec((1, 1, tm), imap, memory_space=pltpu.SMEM)
    grid_spec = pltpu.PrefetchScalarGridSpec(
        num_scalar_prefetch=2,
        grid=(n_tiles, nf),
        in_specs=[tok_spec(lambda i, f, te, va: (i, 0, 0)),
                  tok_spec(lambda i, f, te, va: (jnp.minimum(i + 1, n_tiles - 1), 0, 0)),
                  pl.BlockSpec(memory_space=pl.ANY),
                  pl.BlockSpec((1, d, tf), lambda i, f, te, va: (te[i], 0, f)),
                  pl.BlockSpec((1, d, tf), lambda i, f, te, va: (te[i], 0, f)),
                  pl.BlockSpec((1, tf, d), lambda i, f, te, va: (te[i], f, 0))],
        out_specs=pl.BlockSpec((tm, d), lambda i, f, te, va: (i, 0)),
        scratch_shapes=[pltpu.VMEM((2, tm, d), F32), pltpu.VMEM((tm, d), BF16), pltpu.VMEM((tm, d), F32),
                        pltpu.SemaphoreType.DMA((2,))],
    )
    return pl.pallas_call(
        functools.partial(_moe_gmm_kernel, nf=nf, gather_steps=gather_steps),
        grid_spec=grid_spec,
        out_shape=jax.ShapeDtypeStruct((n_tiles * tm, d), F32),
        compiler_params=_cparams(("arbitrary", "arbitrary")),
        name="moe_gmm",
    )(tile_e, valid, tok, tok, h, wg, wu, wd)


def _moe_combine_kernel(pos_ref, pos_next_ref, x_ref, route_ref, y_hbm, o_ref, buf_ref, sem):
    i = pl.program_id(0)
    tc = x_ref.shape[0]
    slot = i % 2

    def row_copy(pos, r, s):
        return pltpu.make_async_copy(y_hbm.at[pl.ds(pos[0, 0, r], 1), :],
                                     buf_ref.at[s, pl.ds(r, 1), :], sem.at[s])

    @pl.when(i == 0)
    def _():
        def body(r, c):
            row_copy(pos_ref, r, 0).start()
            return c
        lax.fori_loop(0, TOP_K * tc, body, 0, unroll=8)

    @pl.when(i + 1 < pl.num_programs(0))
    def _():
        for r in range(TOP_K * tc):
            row_copy(pos_next_ref, r, 1 - slot).start()

    pltpu.make_async_copy(y_hbm.at[pl.ds(0, TOP_K * tc), :], buf_ref.at[slot], sem.at[slot]).wait()
    y = buf_ref[slot]
    w = route_ref[...]
    o_ref[...] = (x_ref[...] + w[:, TOP_K:TOP_K + 1] * y[0:tc]
                  + w[:, TOP_K + 1:TOP_K + 2] * y[tc:2 * tc])


def _moe_combine(x, route, y_rows, pos, *, tc=512):
    t, d = x.shape
    n = t // tc
    pos_t = pos.reshape(n, tc, TOP_K).transpose(0, 2, 1).reshape(n, 1, TOP_K * tc)
    pos_spec = lambda imap: pl.BlockSpec((1, 1, TOP_K * tc), imap, memory_space=pltpu.SMEM)
    return pl.pallas_call(
        _moe_combine_kernel,
        grid=(n,),
        in_specs=[pos_spec(lambda i: (i, 0, 0)),
                  pos_spec(lambda i: (jnp.minimum(i + 1, n - 1), 0, 0)),
                  pl.BlockSpec((tc, d), lambda i: (i, 0)),
                  pl.BlockSpec((tc, LANES), lambda i: (i, 0)),
                  pl.BlockSpec(memory_space=pl.ANY)],
        out_specs=pl.BlockSpec((tc, d), lambda i: (i, 0)),
        out_shape=jax.ShapeDtypeStruct((t, d), F32),
        scratch_shapes=[pltpu.VMEM((2, TOP_K * tc, d), F32), pltpu.SemaphoreType.DMA((2,))],
        compiler_params=_cparams(("arbitrary",)),
        name="moe_combine",
    )(pos_t, pos_t, x, route, y_rows)


def _row(v, reps=1):
    return jnp.tile(v.astype(F32), reps).reshape(1, -1)


def kernel(x, positions, attn_norm_g, w_in, w_gate, mla_q_norm_g, mla_kv_norm_g, mla_w_uq, mla_w_ukv, mla_qn_nope_g, mla_qn_rope_g, mla_kn_nope_g, mla_kn_rope_g, diff_qn_g, diff_kn_g, diff_lam_q1, diff_lam_k1, diff_lam_q2, diff_lam_k2, diff_subln_g, w_branch_a, w_branch_b, w_out, ffn_norm_g, dense_w_gate, dense_w_up, dense_w_down, moe_w_router, moe_w_gate, moe_w_up, moe_w_down):
    batch, seq, d = x.shape
    t = batch * seq
    xf = x.reshape(t, d)
    pos_f = positions.astype(F32)
    cos, sin = _rope_tables(pos_f.reshape(t, 1))
    pos_col = jnp.broadcast_to(pos_f.reshape(batch, seq, 1), (batch, seq, LANES))
    pos_row = pos_f.reshape(batch, 1, seq)
    slopes = 2.0 ** (-8.0 * jnp.arange(1, DIFF_HEADS + 1, dtype=F32) / DIFF_HEADS)
    n_in_head = MLA_Q_LORA + MLA_KV_LORA + MLA_ROPE

    for layer in range(DEPTH):
        w1 = jnp.concatenate(
            [w_in[layer][:, :n_in_head], jnp.zeros((d, P_BLK - n_in_head), F32),
             w_in[layer][:, n_in_head:], w_gate[layer]], axis=1).astype(BF16)
        wuq = mla_w_uq[layer].reshape(MLA_Q_LORA, MLA_HEADS, MLA_NOPE + MLA_ROPE)
        wuq = jnp.concatenate([wuq[:, :, :MLA_NOPE].reshape(MLA_Q_LORA, -1),
                               wuq[:, :, MLA_NOPE:].reshape(MLA_Q_LORA, -1)], axis=1).astype(BF16)
        wukv = mla_w_ukv[layer].reshape(MLA_KV_LORA, MLA_HEADS, MLA_NOPE + MLA_V)
        wukv = jnp.concatenate([wukv[:, :, :MLA_NOPE].reshape(MLA_KV_LORA, -1),
                                wukv[:, :, MLA_NOPE:].reshape(MLA_KV_LORA, -1)], axis=1).astype(BF16)
        gkr = jnp.concatenate([mla_kn_rope_g[layer].astype(F32), jnp.zeros((LANES - MLA_ROPE,), F32)]).reshape(1, -1)

        p = _norm_proj(xf, _row(attn_norm_g[layer]), w1)
        q_bound = MLA_Q_SCALE * jnp.sqrt(_norm_bound(MLA_NOPE, mla_qn_nope_g[layer]) ** 2
                                         + _norm_bound(MLA_ROPE, mla_qn_rope_g[layer]) ** 2)
        k_bound = jnp.sqrt(_norm_bound(MLA_NOPE, mla_kn_nope_g[layer]) ** 2
                           + _norm_bound(MLA_ROPE, mla_kn_rope_g[layer]) ** 2)
        shift, use_bounded = _score_shift(q_bound, k_bound)
        shift_row = jnp.where(jnp.arange(LANES) % 64 == 0, -shift, 0.0).astype(F32).reshape(1, LANES)
        q_m, k_m, v_m = _mla_prep(
            p, cos, sin, _row(mla_q_norm_g[layer]), _row(mla_kv_norm_g[layer]), wuq, wukv,
            _row(mla_qn_nope_g[layer]), _row(mla_qn_rope_g[layer], 2), _row(mla_kn_nope_g[layer]), gkr,
            shift_row, batch=batch, seq=seq)
        o_a = _mla_attn(use_bounded.astype(jnp.int32).reshape(1), q_m, k_m, v_m).reshape(t, -1)

        lambda_init = 0.8 - 0.6 * math.exp(-0.3 * layer)
        qt_d, kn, vt_d = _diff_prep(p, _row(diff_qn_g[layer], 2), _row(diff_kn_g[layer], 2),
                                  batch=batch, seq=seq)
        lam4 = jnp.stack([diff_lam_q1[layer], diff_lam_k1[layer],
                          diff_lam_q2[layer], diff_lam_k2[layer]]).astype(F32)
        d_shift, d_bounded = _score_shift(DIFF_Q_SCALE * _norm_bound(DIFF_HEAD_DIM, diff_qn_g[layer]),
                                          _norm_bound(DIFF_HEAD_DIM, diff_kn_g[layer]))
        scal = jnp.concatenate([slopes, d_shift.reshape(1), d_bounded.astype(F32).reshape(1)])
        o_b = _diff_attn(scal, qt_d, kn.reshape(batch, seq, -1), vt_d,
                         pos_row, pos_col, lam4, _row(diff_subln_g[layer]),
                         lambda_init=lambda_init).reshape(t, -1)

        j = layer // 2
        wa, wb, wo = (w.astype(BF16) for w in (w_branch_a[layer], w_branch_b[layer], w_out[layer]))
        if layer % 2 == 0:
            xf, h = _merge(xf, o_a, o_b, p, wa, wb, wo, _row(ffn_norm_g[layer]))
            xf = _ffn(xf, h, dense_w_gate[j].astype(BF16), dense_w_up[j].astype(BF16),
                      dense_w_down[j].astype(BF16))
        else:
            wr = jnp.pad(moe_w_router[j].astype(F32), ((0, 0), (0, LANES - N_EXPERTS)))
            wr_hi = wr.astype(BF16)
            wr_lo = (wr - wr_hi.astype(F32)).astype(BF16)
            xf, h, route = _merge(xf, o_a, o_b, p, wa, wb, wo, _row(ffn_norm_g[layer]),
                                  router=(wr_hi, wr_lo))
            tok, tile_e, valid, pos = _route_plan(route, MOE_TILE_ROWS)
            y_rows = _moe_gmm(tok, tile_e, valid, h, moe_w_gate[j].astype(BF16),
                              moe_w_up[j].astype(BF16), moe_w_down[j].astype(BF16))
            xf = _moe_combine(xf, route, y_rows, pos)
    return xf.reshape(batch, seq, d)
```

```python
import functools
import math

import jax
import jax.numpy as jnp
from jax import lax
from jax.experimental import pallas as pl
from jax.experimental.pallas import tpu as pltpu

D_MODEL = 1024
DEPTH = 2
MLA_HEADS = 8
MLA_Q_LORA = 512
MLA_KV_LORA = 256
MLA_NOPE = 128
MLA_ROPE = 64
MLA_V = 128
ROPE_THETA = 10000.0
DIFF_HEADS = 8
DIFF_HEAD_DIM = 64
D_FF_DENSE = 2816
N_EXPERTS = 8
TOP_K = 2
D_FF_EXPERT = 3584
EPS = 1e-6

LANES = 128
LOG2E = 1.4426950408889634
MLA_Q_SCALE = (MLA_NOPE + MLA_ROPE) ** -0.5 * LOG2E
DIFF_Q_SCALE = DIFF_HEAD_DIM ** -0.5 * LOG2E
NEG_BIG = -1e30
BF16 = jnp.bfloat16
F32 = jnp.float32

P_BLK = DIFF_HEADS * 2 * DIFF_HEAD_DIM
P_KR_COL0 = MLA_Q_LORA + MLA_KV_LORA
P_DQ_BLK = 1
P_DK_BLK = 2
P_DV_BLK = 3
P_GATE_COL0 = 4 * P_BLK
MOE_TILE_ROWS = 512
VMEM_LIMIT = 52 * 1024 * 1024


def _cparams(sem):
    return pltpu.CompilerParams(dimension_semantics=sem, vmem_limit_bytes=VMEM_LIMIT)


def _rms(x, gain):
    ms = jnp.mean(x * x, axis=-1, keepdims=True)
    return x * lax.rsqrt(ms + EPS) * gain


def _ms_half_lanes(xb):
    lane = lax.broadcasted_iota(jnp.int32, xb.shape, 1)
    lo = lane < 64
    sq = xb * xb
    s_lo = jnp.sum(jnp.where(lo, sq, 0.0), axis=-1, keepdims=True)
    s_hi = jnp.sum(jnp.where(lo, 0.0, sq), axis=-1, keepdims=True)
    return jnp.where(lo, s_lo, s_hi) * (1.0 / 64.0)


def _norm_proj_kernel(x_ref, g_ref, w_ref, o_ref, h_ref, *, gate_tile0):
    j = pl.program_id(1)

    @pl.when(j == 0)
    def _():
        h_ref[...] = _rms(x_ref[...], g_ref[...]).astype(BF16)

    y = jnp.dot(h_ref[...], w_ref[...], preferred_element_type=F32)

    @pl.when(j < gate_tile0)
    def _():
        o_ref[...] = y.astype(BF16)

    @pl.when(j >= gate_tile0)
    def _():
        o_ref[...] = jax.nn.sigmoid(y).astype(BF16)


def _norm_proj(x, gain, w, *, tm=1024, tn=2048):
    t, d = x.shape
    n = w.shape[1]
    return pl.pallas_call(
        functools.partial(_norm_proj_kernel, gate_tile0=P_GATE_COL0 // tn),
        grid=(t // tm, n // tn),
        in_specs=[
            pl.BlockSpec((tm, d), lambda i, j: (i, 0)),
            pl.BlockSpec((1, d), lambda i, j: (0, 0)),
            pl.BlockSpec((d, tn), lambda i, j: (0, j)),
        ],
        out_specs=pl.BlockSpec((tm, tn), lambda i, j: (i, j)),
        out_shape=jax.ShapeDtypeStruct((t, n), BF16),
        scratch_shapes=[pltpu.VMEM((tm, d), BF16)],
        compiler_params=_cparams(("parallel", "arbitrary")),
        name="norm_proj",
    )(x, gain, w)


def _rope_table_kernel(pos_ref, freq_ref, cos_ref, sin_ref):
    ang = pos_ref[...] * freq_ref[...]
    lane = lax.broadcasted_iota(jnp.int32, ang.shape, 1)
    first_half = (lane % 64) < 32
    cos_ref[...] = jnp.cos(ang)
    sin_ref[...] = jnp.where(first_half, -1.0, 1.0) * jnp.sin(ang)


def _rope_tables(pos_col, *, tm=2048):
    t = pos_col.shape[0]
    half = MLA_ROPE // 2
    inv_freq = ROPE_THETA ** (-jnp.arange(half, dtype=F32) / half)
    freq = jnp.tile(inv_freq, LANES // half).reshape(1, LANES)
    return pl.pallas_call(
        _rope_table_kernel,
        grid=(t // tm,),
        in_specs=[pl.BlockSpec((tm, 1), lambda i: (i, 0)),
                  pl.BlockSpec((1, LANES), lambda i: (0, 0))],
        out_specs=[pl.BlockSpec((tm, LANES), lambda i: (i, 0))] * 2,
        out_shape=[jax.ShapeDtypeStruct((t, LANES), F32)] * 2,
        compiler_params=_cparams(("parallel",)),
        name="rope_tables",
    )(pos_col, freq)


def _rope(n, cos, sin_signed):
    w = n.shape[-1]
    lane = lax.broadcasted_iota(jnp.int32, n.shape, 1)
    first_half = (lane % 64) < 32
    swapped = jnp.where(first_half, pltpu.roll(n, w - 32, 1), pltpu.roll(n, 32, 1))
    return n * cos + swapped * sin_signed


def _mla_prep_kernel(p_ref, cos_ref, sin_ref, gq_ref, gkv_ref, wuq_ref, wukv_ref,
                     gqn_ref, gqr_ref, gkn_ref, gkr_ref, shift_ref, q_ref, k_ref, v_ref, *, q_scale):
    cq = p_ref[:, 0:MLA_Q_LORA].astype(F32)
    ckv = p_ref[:, MLA_Q_LORA:MLA_Q_LORA + MLA_KV_LORA].astype(F32)
    kr = p_ref[:, P_KR_COL0:P_KR_COL0 + LANES].astype(F32)
    cos = cos_ref[...]
    sin = sin_ref[...]

    q = jnp.dot(_rms(cq, gq_ref[...]).astype(BF16), wuq_ref[...], preferred_element_type=F32)
    kv = jnp.dot(_rms(ckv, gkv_ref[...]).astype(BF16), wukv_ref[...], preferred_element_type=F32)

    n_nope = MLA_HEADS * MLA_NOPE
    kr_ms = jnp.sum(kr * kr, axis=-1, keepdims=True) * (1.0 / MLA_ROPE)
    kr_n = kr * lax.rsqrt(kr_ms + EPS) * gkr_ref[...]
    kr_r = _rope(kr_n, cos, sin)
    kr_dup = kr_r + pltpu.roll(kr_r, 64, 1)

    lane = lax.broadcasted_iota(jnp.int32, (q.shape[0], LANES), 1)
    one_hot = jnp.where(lane % 64 == 0, 1.0, 0.0)
    for pair in range(MLA_HEADS // 2):
        qr = q[:, n_nope + pair * LANES:n_nope + (pair + 1) * LANES]
        qr_n = qr * lax.rsqrt(_ms_half_lanes(qr) + EPS) * gqr_ref[...]
        qr_r = _rope(qr_n, cos, sin) * q_scale
        for sub in range(2):
            h = 2 * pair + sub
            keep = (lane < 64) if sub == 0 else (lane >= 64)
            qn = _rms(q[:, h * MLA_NOPE:(h + 1) * MLA_NOPE], gqn_ref[...]) * q_scale
            q_ref[0, h, 0:LANES, :] = qn.T.astype(BF16)
            q_ref[0, h, LANES:2 * LANES, :] = jnp.where(keep, qr_r, shift_ref[...]).T.astype(BF16)
            kn = _rms(kv[:, h * MLA_NOPE:(h + 1) * MLA_NOPE], gkn_ref[...])
            k_ref[0, h, :, 0:LANES] = kn.astype(BF16)
            k_ref[0, h, :, LANES:2 * LANES] = jnp.where(keep, kr_dup, one_hot).astype(BF16)
            v_ref[0, h, :, :] = kv[:, n_nope + h * MLA_V:n_nope + (h + 1) * MLA_V].T.astype(BF16)


def _mla_prep(p, cos, sin, gq, gkv, wuq, wukv, gqn, gqr, gkn, gkr, shift_row, *, batch, seq, tm=512):
    nb = seq // tm
    full = lambda a: pl.BlockSpec(a.shape, lambda b, i: (0,) * a.ndim)
    row = lambda b, i: (b * nb + i, 0)
    hshape = lambda w: jax.ShapeDtypeStruct((batch, MLA_HEADS, seq, w), BF16)
    hspec = lambda w: pl.BlockSpec((1, MLA_HEADS, tm, w), lambda b, i: (b, 0, i, 0))
    tshape = lambda w: jax.ShapeDtypeStruct((batch, MLA_HEADS, w, seq), BF16)
    tspec = lambda w: pl.BlockSpec((1, MLA_HEADS, w, tm), lambda b, i: (b, 0, 0, i))
    return pl.pallas_call(
        functools.partial(_mla_prep_kernel, q_scale=MLA_Q_SCALE),
        grid=(batch, nb),
        in_specs=[pl.BlockSpec((tm, P_BLK), row),
                  pl.BlockSpec((tm, LANES), row), pl.BlockSpec((tm, LANES), row),
                  full(gq), full(gkv), full(wuq), full(wukv),
                  full(gqn), full(gqr), full(gkn), full(gkr), full(shift_row)],
        out_specs=[tspec(2 * LANES), hspec(2 * LANES), tspec(MLA_V)],
        out_shape=[tshape(2 * LANES), hshape(2 * LANES), tshape(MLA_V)],
        compiler_params=_cparams(("parallel", "parallel")),
        name="mla_prep",
    )(p, cos, sin, gq, gkv, wuq, wukv, gqn, gqr, gkn, gkr, shift_row)


BOUND_MARGIN = 1.02
MAX_SHIFT_SPAN = 100.0


def _norm_bound(n, gain):
    return math.sqrt(n) * jnp.max(jnp.abs(gain.astype(F32)))


def _score_shift(q_bound, k_bound):
    shift = (BOUND_MARGIN * q_bound * k_bound).astype(BF16).astype(F32)
    return shift, (2.0 * shift <= MAX_SHIFT_SPAN)


def _sublane_partial_sum(p):
    return jnp.sum(p.reshape(p.shape[0] // 8, 8, p.shape[1]), axis=0)


def _online_update_t(s, m, l):
    m_new = jnp.maximum(m, jnp.max(s, axis=0, keepdims=True))
    alpha = jnp.exp2(m - m_new)
    p = jnp.exp2(s - m_new)
    return p, alpha, m_new, alpha * l + jnp.sum(p, axis=0, keepdims=True)


def _mla_attn_kernel(flag_ref, qt_ref, k_ref, vt_ref, o_ref, *, tk, nk, unroll):
    tq = qt_ref.shape[3]
    qt = qt_ref[0, 0]

    def scores(k):
        return jnp.dot(k, qt, preferred_element_type=F32)

    def chunk(j):
        off = pl.multiple_of(j * tk, tk)
        return k_ref[0, 0, pl.ds(off, tk), :], vt_ref[0, 0, :, pl.ds(off, tk)]

    def bounded():
        def body(j, carry):
            l, acc = carry
            k, vt = chunk(j)
            p = jnp.exp2(scores(k))
            return (l + _sublane_partial_sum(p),
                    acc + jnp.dot(vt, p.astype(BF16), preferred_element_type=F32))

        init = (jnp.zeros((8, tq), F32), jnp.zeros((MLA_V, tq), F32))
        l, acc = lax.fori_loop(0, nk, body, init, unroll=unroll)
        return acc / jnp.sum(l, axis=0, keepdims=True)

    def online():
        def body(j, carry):
            m, l, acc = carry
            k, vt = chunk(j)
            p, alpha, m, l = _online_update_t(scores(k), m, l)
            return m, l, alpha * acc + jnp.dot(vt, p.astype(BF16), preferred_element_type=F32)

        init = (jnp.full((1, tq), NEG_BIG, F32), jnp.zeros((1, tq), F32), jnp.zeros((MLA_V, tq), F32))
        _, l, acc = lax.fori_loop(0, nk, body, init)
        return acc / l

    o_ref[0] = lax.cond(flag_ref[0] != 0, bounded, online).T.astype(BF16)


def _mla_attn(flag, qt, k, vt, *, tq=1024, tk=1024, unroll=8):
    b, h, s, w = k.shape
    return pl.pallas_call(
        functools.partial(_mla_attn_kernel, tk=tk, nk=s // tk, unroll=unroll),
        grid=(b, h, s // tq),
        in_specs=[pl.BlockSpec(memory_space=pltpu.SMEM),
                  pl.BlockSpec((1, 1, w, tq), lambda bi, hi, qi: (bi, hi, 0, qi)),
                  pl.BlockSpec((1, 1, s, w), lambda bi, hi, qi: (bi, hi, 0, 0)),
                  pl.BlockSpec((1, 1, MLA_V, s), lambda bi, hi, qi: (bi, hi, 0, 0))],
        out_specs=pl.BlockSpec((1, tq, MLA_V), lambda bi, hi, qi: (bi, qi, hi)),
        out_shape=jax.ShapeDtypeStruct((b, s, h * MLA_V), BF16),
        compiler_params=_cparams(("parallel", "parallel", "arbitrary")),
        name="mla_attn",
    )(flag, qt, k, vt)


def _diff_prep_kernel(q_ref, k_ref, v_ref, gq_ref, gk_ref, qt_ref, kn_ref, vt_ref, *, q_scale):
    for c in range(DIFF_HEADS):
        sl = slice(c * LANES, (c + 1) * LANES)
        q = q_ref[:, sl].astype(F32)
        k = k_ref[:, sl].astype(F32)
        qt_ref[0, c] = (q * lax.rsqrt(_ms_half_lanes(q) + EPS) * (gq_ref[...] * q_scale)).T.astype(BF16)
        kn_ref[:, sl] = (k * lax.rsqrt(_ms_half_lanes(k) + EPS) * gk_ref[...]).astype(BF16)
        vt_ref[0, c] = v_ref[:, sl].astype(F32).T.astype(BF16)


def _diff_prep(p, gq, gk, *, batch, seq, tm=1024):
    t = p.shape[0]
    nb = seq // tm
    w = DIFF_HEADS * 2 * DIFF_HEAD_DIM
    row = lambda blk: (lambda b, i: (b * nb + i, blk))
    gain = pl.BlockSpec((1, LANES), lambda b, i: (0, 0))
    tshape = jax.ShapeDtypeStruct((batch, DIFF_HEADS, LANES, seq), BF16)
    tspec = pl.BlockSpec((1, DIFF_HEADS, LANES, tm), lambda b, i: (b, 0, 0, i))
    return pl.pallas_call(
        functools.partial(_diff_prep_kernel, q_scale=DIFF_Q_SCALE),
        grid=(batch, nb),
        in_specs=[pl.BlockSpec((tm, w), row(P_DQ_BLK)), pl.BlockSpec((tm, w), row(P_DK_BLK)),
                  pl.BlockSpec((tm, w), row(P_DV_BLK)), gain, gain],
        out_specs=[tspec, pl.BlockSpec((tm, w), row(0)), tspec],
        out_shape=[tshape, jax.ShapeDtypeStruct((t, w), BF16), tshape],
        compiler_params=_cparams(("parallel", "parallel")),
        name="diff_prep",
    )(p, p, p, gq, gk)


def _diff_attn_kernel(scal_ref, qt_ref, k_ref, vt_ref, pq_ref, pk_ref, lam_ref, g_ref, o_ref,
                      *, tk, nk, unroll, lambda_init):
    qt = qt_ref[0, 0]
    tq = qt.shape[1]
    row = lax.broadcasted_iota(jnp.int32, qt.shape, 0)
    zero = jnp.zeros_like(qt)
    q0t = jnp.where(row < DIFF_HEAD_DIM, qt, zero)
    q1t = jnp.where(row < DIFF_HEAD_DIM, zero, qt)
    pq = pq_ref[0]
    slope = scal_ref[pl.program_id(1)] * LOG2E
    shift = scal_ref[DIFF_HEADS]
    use_bounded = scal_ref[DIFF_HEADS + 1]

    spq = slope * pq

    def chunk(j, shifted):
        off = pl.multiple_of(j * tk, tk)
        k = k_ref[0, pl.ds(off, tk), :]
        vt = vt_ref[0, 0, :, pl.ds(off, tk)]
        spk = slope * pk_ref[0, pl.ds(off, tk), :]
        bias = jnp.abs(jnp.concatenate([spk] * (tq // LANES), axis=1) - spq)
        return k, vt, (bias + shift if shifted else bias)

    def scores(k, qt):
        return jnp.dot(k, qt, preferred_element_type=F32)

    def pv(vt, p):
        return jnp.dot(vt, p.astype(BF16), preferred_element_type=F32)

    def bounded():
        def body(j, carry):
            l0, a0, l1, a1 = carry
            k, vt, bias = chunk(j, False)
            p0 = jnp.exp2(scores(k, q0t) - bias)
            p1 = jnp.exp2(scores(k, q1t) - bias)
            return (l0 + _sublane_partial_sum(p0), a0 + pv(vt, p0),
                    l1 + _sublane_partial_sum(p1), a1 + pv(vt, p1))

        zl = jnp.zeros((8, tq), F32)
        za = jnp.zeros((LANES, tq), F32)
        l0, a0, l1, a1 = lax.fori_loop(0, nk, body, (zl, za, zl, za), unroll=unroll)
        return (a0 / jnp.sum(l0, axis=0, keepdims=True), a1 / jnp.sum(l1, axis=0, keepdims=True))

    def online():
        def body(j, carry):
            m0, l0, a0, m1, l1, a1 = carry
            k, vt, bias = chunk(j, True)
            p0, al0, m0, l0 = _online_update_t(scores(k, q0t) - bias, m0, l0)
            a0 = al0 * a0 + pv(vt, p0)
            p1, al1, m1, l1 = _online_update_t(scores(k, q1t) - bias, m1, l1)
            a1 = al1 * a1 + pv(vt, p1)
            return m0, l0, a0, m1, l1, a1

        neg = jnp.full((1, tq), NEG_BIG, F32)
        z1 = jnp.zeros((1, tq), F32)
        za = jnp.zeros((LANES, tq), F32)
        _, l0, a0, _, l1, a1 = lax.fori_loop(0, nk, body, (neg, z1, za, neg, z1, za))
        return a0 / l0, a1 / l1

    o0, o1 = lax.cond(use_bounded != 0.0, bounded, online)
    lam = (jnp.exp(jnp.sum(lam_ref[0:1, :] * lam_ref[1:2, :], axis=-1, keepdims=True))
           - jnp.exp(jnp.sum(lam_ref[2:3, :] * lam_ref[3:4, :], axis=-1, keepdims=True))
           + lambda_init)
    o = (o0 - lam * o1).T
    o_ref[0] = (_rms(o, g_ref[...]) * (1.0 - lambda_init)).astype(BF16)


def _diff_attn(scal, qt, kn, vt, pos_row, pos_col, lam4, subln_g, *, lambda_init, tq=512, tk=512, unroll=16):
    b, s, _ = kn.shape
    return pl.pallas_call(
        functools.partial(_diff_attn_kernel, tk=tk, nk=s // tk, unroll=unroll, lambda_init=lambda_init),
        grid=(b, DIFF_HEADS, s // tq),
        in_specs=[pl.BlockSpec(memory_space=pltpu.SMEM),
                  pl.BlockSpec((1, 1, LANES, tq), lambda bi, hi, qi: (bi, hi, 0, qi)),
                  pl.BlockSpec((1, s, LANES), lambda bi, hi, qi: (bi, 0, hi)),
                  pl.BlockSpec((1, 1, LANES, s), lambda bi, hi, qi: (bi, hi, 0, 0)),
                  pl.BlockSpec((1, 1, tq), lambda bi, hi, qi: (bi, 0, qi)),
                  pl.BlockSpec((1, s, LANES), lambda bi, hi, qi: (bi, 0, 0)),
                  pl.BlockSpec((4, DIFF_HEAD_DIM), lambda bi, hi, qi: (0, 0)),
                  pl.BlockSpec((1, LANES), lambda bi, hi, qi: (0, 0))],
        out_specs=pl.BlockSpec((1, tq, LANES), lambda bi, hi, qi: (bi, qi, hi)),
        out_shape=jax.ShapeDtypeStruct((b, s, DIFF_HEADS * LANES), BF16),
        compiler_params=_cparams(("parallel", "parallel", "arbitrary")),
        name="diff_attn",
    )(scal, qt, kn, vt, pos_row, pos_col, lam4, subln_g)


def _split_dot_f32(a, w_hi, w_lo):
    a_hi = a.astype(BF16)
    a_lo = (a - a_hi.astype(F32)).astype(BF16)
    return (jnp.dot(a_hi, w_hi, preferred_element_type=F32)
            + jnp.dot(a_lo, w_hi, preferred_element_type=F32)
            + jnp.dot(a_hi, w_lo, preferred_element_type=F32))


def _merge_kernel(x_ref, oa_ref, ob_ref, ga_ref, gb_ref, wa_ref, wb_ref, wo_ref, gf_ref, *rest,
                  with_router):
    if with_router:
        wr_hi_ref, wr_lo_ref, xo_ref, h_ref, route_ref = rest
    else:
        xo_ref, h_ref = rest
    ya = jnp.dot(oa_ref[...], wa_ref[...], preferred_element_type=F32)
    yb = jnp.dot(ob_ref[...], wb_ref[...], preferred_element_type=F32)
    merged = ga_ref[...].astype(F32) * ya + gb_ref[...].astype(F32) * yb
    x = x_ref[...] + jnp.dot(merged.astype(BF16), wo_ref[...], preferred_element_type=F32)
    xo_ref[...] = x
    h = _rms(x, gf_ref[...])
    h_ref[...] = h.astype(h_ref.dtype)
    if with_router:
        route_ref[...] = _top2_route(_split_dot_f32(h, wr_hi_ref[...], wr_lo_ref[...]))


def _merge(x, oa, ob, p, wa, wb, wo, gf, router=None, *, tm=512):
    t, d = x.shape
    ga_blk = P_GATE_COL0 // d
    row = lambda i: (i, 0)
    full = lambda a: pl.BlockSpec(a.shape, lambda i: (0,) * a.ndim)
    in_specs = [pl.BlockSpec((tm, d), row), pl.BlockSpec((tm, d), row), pl.BlockSpec((tm, d), row),
                pl.BlockSpec((tm, d), lambda i: (i, ga_blk)),
                pl.BlockSpec((tm, d), lambda i: (i, ga_blk + 1)),
                full(wa), full(wb), full(wo), full(gf)]
    args = [x, oa, ob, p, p, wa, wb, wo, gf]
    out_specs = [pl.BlockSpec((tm, d), row), pl.BlockSpec((tm, d), row)]
    h_dtype = BF16 if router is None else F32
    out_shape = [jax.ShapeDtypeStruct((t, d), F32), jax.ShapeDtypeStruct((t, d), h_dtype)]
    if router is not None:
        in_specs += [full(router[0]), full(router[1])]
        args += list(router)
        out_specs.append(pl.BlockSpec((tm, LANES), row))
        out_shape.append(jax.ShapeDtypeStruct((t, LANES), F32))
    return pl.pallas_call(
        functools.partial(_merge_kernel, with_router=router is not None),
        grid=(t // tm,),
        in_specs=in_specs,
        out_specs=out_specs,
        out_shape=out_shape,
        compiler_params=_cparams(("parallel",)),
        name="merge_router" if router is not None else "merge",
    )(*args)


def _swiglu_act(h, wg, wu):
    g = jnp.dot(h, wg, preferred_element_type=F32)
    u = jnp.dot(h, wu, preferred_element_type=F32)
    return (g * jax.nn.sigmoid(g) * u).astype(BF16)


def _ffn_kernel(x_ref, h_ref, wg_ref, wu_ref, wd_ref, o_ref, acc_ref):
    f = pl.program_id(1)
    y = jnp.dot(_swiglu_act(h_ref[...], wg_ref[...], wu_ref[...]), wd_ref[...],
                preferred_element_type=F32)

    @pl.when(f == 0)
    def _():
        acc_ref[...] = x_ref[...] + y

    @pl.when(f > 0)
    def _():
        acc_ref[...] += y

    @pl.when(f == pl.num_programs(1) - 1)
    def _():
        o_ref[...] = acc_ref[...]


def _ffn(x, h, wg, wu, wd, *, tm=1024, tf=1408):
    t, d = x.shape
    ff = wg.shape[1]
    return pl.pallas_call(
        _ffn_kernel,
        grid=(t // tm, ff // tf),
        in_specs=[pl.BlockSpec((tm, d), lambda i, f: (i, 0)),
                  pl.BlockSpec((tm, d), lambda i, f: (i, 0)),
                  pl.BlockSpec((d, tf), lambda i, f: (0, f)),
                  pl.BlockSpec((d, tf), lambda i, f: (0, f)),
                  pl.BlockSpec((tf, d), lambda i, f: (f, 0))],
        out_specs=pl.BlockSpec((tm, d), lambda i, f: (i, 0)),
        out_shape=jax.ShapeDtypeStruct((t, d), F32),
        scratch_shapes=[pltpu.VMEM((tm, d), F32)],
        compiler_params=_cparams(("parallel", "arbitrary")),
        name="ffn_dense",
    )(x, h, wg, wu, wd)


def _top2_route(logits):
    lane_i = lax.broadcasted_iota(jnp.int32, logits.shape, 1)
    lane = lane_i.astype(F32)
    neg = jnp.float32(-jnp.inf)
    z = jnp.where(lane_i < N_EXPERTS, logits, neg)
    m1 = jnp.max(z, axis=-1, keepdims=True)
    i1 = jnp.min(jnp.where(z == m1, lane, float(LANES)), axis=-1, keepdims=True)
    z2 = jnp.where(lane == i1, neg, z)
    m2 = jnp.max(z2, axis=-1, keepdims=True)
    i2 = jnp.min(jnp.where(z2 == m2, lane, float(LANES)), axis=-1, keepdims=True)
    e2 = jnp.exp(m2 - m1)
    w1 = 1.0 / (1.0 + e2)
    w2 = e2 / (1.0 + e2)
    return jnp.where(lane_i == 0, i1, jnp.where(lane_i == 1, i2,
                     jnp.where(lane_i == 2, w1, jnp.where(lane_i == 3, w2, 0.0))))


def _route_plan(route, tm):
    t = route.shape[0]
    e = route[:, 0:TOP_K].astype(jnp.int32)
    sel = (e[:, :, None] == jnp.arange(N_EXPERTS, dtype=jnp.int32)).astype(jnp.int32).sum(axis=1)
    csum = jnp.cumsum(sel, axis=0)
    padded = (csum[-1] + tm - 1) // tm * tm
    end = jnp.cumsum(padded)
    pos = (end - padded)[e] + jnp.take_along_axis(csum - sel, e, axis=1)
    assert (TOP_K * t) % tm == 0
    n_rows = TOP_K * t + N_EXPERTS * tm
    flat = pos.reshape(-1)
    tok = jnp.zeros((n_rows,), jnp.int32).at[flat].set(
        jnp.repeat(jnp.arange(t, dtype=jnp.int32), TOP_K), unique_indices=True)
    n_tiles = n_rows // tm
    row0 = jnp.arange(n_tiles, dtype=jnp.int32) * tm
    valid = (row0 < end[-1]).astype(jnp.int32)
    first = jnp.minimum(row0, end[-1] - 1)
    tile_e = jnp.sum((first[:, None] >= end[None, :]).astype(jnp.int32), axis=1)
    return tok.reshape(n_tiles, 1, tm), tile_e, valid, pos


def _moe_gmm_kernel(tile_e_ref, valid_ref, tok_ref, tok_next_ref, h_hbm, wg_ref, wu_ref, wd_ref,
                    o_ref, xg_ref, xb_ref, acc_ref, sem, *, nf, gather_steps):
    del tile_e_ref
    i = pl.program_id(0)
    f = pl.program_id(1)
    tm = xb_ref.shape[0]
    slot = i % 2

    def row_copy(tok, r, s):
        return pltpu.make_async_copy(h_hbm.at[pl.ds(tok[0, 0, r], 1), :],
                                     xg_ref.at[s, pl.ds(r, 1), :], sem.at[s])

    @pl.when((i == 0) & (f == 0))
    def _():
        def body(r, c):
            row_copy(tok_ref, r, 0).start()
            return c
        lax.fori_loop(0, tm, body, 0, unroll=8)

    valid = valid_ref[i] != 0
    prev_valid = valid_ref[jnp.maximum(i - 1, 0)] != 0

    @pl.when((f == 0) & ((i == 0) | prev_valid))
    def _():
        pltpu.make_async_copy(h_hbm.at[pl.ds(0, tm), :], xg_ref.at[slot], sem.at[slot]).wait()
        xb_ref[...] = xg_ref[slot].astype(BF16)

    def step(request_rows):
        if request_rows:
            rows = tm // gather_steps
            for r in range(rows):
                row_copy(tok_next_ref, f * rows + r, 1 - slot).start()
        y = jnp.dot(_swiglu_act(xb_ref[...], wg_ref[0], wu_ref[0]), wd_ref[0],
                    preferred_element_type=F32)

        @pl.when(f == 0)
        def _():
            acc_ref[...] = y

        @pl.when((f > 0) & (f < nf - 1))
        def _():
            acc_ref[...] += y

        @pl.when(f == nf - 1)
        def _():
            o_ref[...] = acc_ref[...] + y

    pl.when(valid & (f < gather_steps))(functools.partial(step, True))
    pl.when(valid & (f >= gather_steps))(functools.partial(step, False))

    @pl.when(jnp.logical_not(valid) & (f == nf - 1))
    def _():
        o_ref[...] = jnp.zeros_like(o_ref)


def _moe_gmm(tok, tile_e, valid, h, wg, wu, wd, *, tf=1792, gather_steps=2):
    n_tiles, _, tm = tok.shape
    d = h.shape[1]
    ff = wg.shape[2]
    nf = ff // tf
    assert 2 <= gather_steps <= nf and tm % gather_steps == 0
    tok_spec = lambda imap: pl.BlockSpec((1, 1, tm), imap, memory_space=pltpu.SMEM)
    grid_spec = pltpu.PrefetchScalarGridSpec(
        num_scalar_prefetch=2,
        grid=(n_tiles, nf),
        in_specs=[tok_spec(lambda i, f, te, va: (i, 0, 0)),
                  tok_spec(lambda i, f, te, va: (jnp.minimum(i + 1, n_tiles - 1), 0, 0)),
                  pl.BlockSpec(memory_space=pl.ANY),
                  pl.BlockSpec((1, d, tf), lambda i, f, te, va: (te[i], 0, f)),
                  pl.BlockSpec((1, d, tf), lambda i, f, te, va: (te[i], 0, f)),
                  pl.BlockSpec((1, tf, d), lambda i, f, te, va: (te[i], f, 0))],
        out_specs=pl.BlockSpec((tm, d), lambda i, f, te, va: (i, 0)),
        scratch_shapes=[pltpu.VMEM((2, tm, d), F32), pltpu.VMEM((tm, d), BF16), pltpu.VMEM((tm, d), F32),
                        pltpu.SemaphoreType.DMA((2,))],
    )
    return pl.pallas_call(
        functools.partial(_moe_gmm_kernel, nf=nf, gather_steps=gather_steps),
        grid_spec=grid_spec,
        out_shape=jax.ShapeDtypeStruct((n_tiles * tm, d), F32),
        compiler_params=_cparams(("arbitrary", "arbitrary")),
        name="moe_gmm",
    )(tile_e, valid, tok, tok, h, wg, wu, wd)


def _moe_combine_kernel(pos_ref, pos_next_ref, x_ref, route_ref, y_hbm, o_ref, buf_ref, sem):
    i = pl.program_id(0)
    tc = x_ref.shape[0]
    slot = i % 2

    def row_copy(pos, r, s):
        return pltpu.make_async_copy(y_hbm.at[pl.ds(pos[0, 0, r], 1), :],
                                     buf_ref.at[s, pl.ds(r, 1), :], sem.at[s])

    @pl.when(i == 0)
    def _():
        def body(r, c):
            row_copy(pos_ref, r, 0).start()
            return c
        lax.fori_loop(0, TOP_K * tc, body, 0, unroll=8)

    @pl.when(i + 1 < pl.num_programs(0))
    def _():
        for r in range(TOP_K * tc):
            row_copy(pos_next_ref, r, 1 - slot).start()

    pltpu.make_async_copy(y_hbm.at[pl.ds(0, TOP_K * tc), :], buf_ref.at[slot], sem.at[slot]).wait()
    y = buf_ref[slot]
    w = route_ref[...]
    o_ref[...] = (x_ref[...] + w[:, TOP_K:TOP_K + 1] * y[0:tc]
                  + w[:, TOP_K + 1:TOP_K + 2] * y[tc:2 * tc])


def _moe_combine(x, route, y_rows, pos, *, tc=512):
    t, d = x.shape
    n = t // tc
    pos_t = pos.reshape(n, tc, TOP_K).transpose(0, 2, 1).reshape(n, 1, TOP_K * tc)
    pos_spec = lambda imap: pl.BlockSpec((1, 1, TOP_K * tc), imap, memory_space=pltpu.SMEM)
    return pl.pallas_call(
        _moe_combine_kernel,
        grid=(n,),
        in_specs=[pos_spec(lambda i: (i, 0, 0)),
                  pos_spec(lambda i: (jnp.minimum(i + 1, n - 1), 0, 0)),
                  pl.BlockSpec((tc, d), lambda i: (i, 0)),
                  pl.BlockSpec((tc, LANES), lambda i: (i, 0)),
                  pl.BlockSpec(memory_space=pl.ANY)],
        out_specs=pl.BlockSpec((tc, d), lambda i: (i, 0)),
        out_shape=jax.ShapeDtypeStruct((t, d), F32),
        scratch_shapes=[pltpu.VMEM((2, TOP_K * tc, d), F32), pltpu.SemaphoreType.DMA((2,))],
        compiler_params=_cparams(("arbitrary",)),
        name="moe_combine",
    )(pos_t, pos_t, x, route, y_rows)


def _row(v, reps=1):
    return jnp.tile(v.astype(F32), reps).reshape(1, -1)


def kernel(x, positions, attn_norm_g, w_in, w_gate, mla_q_norm_g, mla_kv_norm_g, mla_w_uq, mla_w_ukv, mla_qn_nope_g, mla_qn_rope_g, mla_kn_nope_g, mla_kn_rope_g, diff_qn_g, diff_kn_g, diff_lam_q1, diff_lam_k1, diff_lam_q2, diff_lam_k2, diff_subln_g, w_branch_a, w_branch_b, w_out, ffn_norm_g, dense_w_gate, dense_w_up, dense_w_down, moe_w_router, moe_w_gate, moe_w_up, moe_w_down):
    batch, seq, d = x.shape
    t = batch * seq
    xf = x.reshape(t, d)
    pos_f = positions.astype(F32)
    cos, sin = _rope_tables(pos_f.reshape(t, 1))
    pos_col = jnp.broadcast_to(pos_f.reshape(batch, seq, 1), (batch, seq, LANES))
    pos_row = pos_f.reshape(batch, 1, seq)
    slopes = 2.0 ** (-8.0 * jnp.arange(1, DIFF_HEADS + 1, dtype=F32) / DIFF_HEADS)
    n_in_head = MLA_Q_LORA + MLA_KV_LORA + MLA_ROPE

    for layer in range(DEPTH):
        w1 = jnp.concatenate(
            [w_in[layer][:, :n_in_head], jnp.zeros((d, P_BLK - n_in_head), F32),
             w_in[layer][:, n_in_head:], w_gate[layer]], axis=1).astype(BF16)
        wuq = mla_w_uq[layer].reshape(MLA_Q_LORA, MLA_HEADS, MLA_NOPE + MLA_ROPE)
        wuq = jnp.concatenate([wuq[:, :, :MLA_NOPE].reshape(MLA_Q_LORA, -1),
                               wuq[:, :, MLA_NOPE:].reshape(MLA_Q_LORA, -1)], axis=1).astype(BF16)
        wukv = mla_w_ukv[layer].reshape(MLA_KV_LORA, MLA_HEADS, MLA_NOPE + MLA_V)
        wukv = jnp.concatenate([wukv[:, :, :MLA_NOPE].reshape(MLA_KV_LORA, -1),
                                wukv[:, :, MLA_NOPE:].reshape(MLA_KV_LORA, -1)], axis=1).astype(BF16)
        gkr = jnp.concatenate([mla_kn_rope_g[layer].astype(F32), jnp.zeros((LANES - MLA_ROPE,), F32)]).reshape(1, -1)

        p = _norm_proj(xf, _row(attn_norm_g[layer]), w1)
        q_bound = MLA_Q_SCALE * jnp.sqrt(_norm_bound(MLA_NOPE, mla_qn_nope_g[layer]) ** 2
                                         + _norm_bound(MLA_ROPE, mla_qn_rope_g[layer]) ** 2)
        k_bound = jnp.sqrt(_norm_bound(MLA_NOPE, mla_kn_nope_g[layer]) ** 2
                           + _norm_bound(MLA_ROPE, mla_kn_rope_g[layer]) ** 2)
        shift, use_bounded = _score_shift(q_bound, k_bound)
        shift_row = jnp.where(jnp.arange(LANES) % 64 == 0, -shift, 0.0).astype(F32).reshape(1, LANES)
        q_m, k_m, v_m = _mla_prep(
            p, cos, sin, _row(mla_q_norm_g[layer]), _row(mla_kv_norm_g[layer]), wuq, wukv,
            _row(mla_qn_nope_g[layer]), _row(mla_qn_rope_g[layer], 2), _row(mla_kn_nope_g[layer]), gkr,
            shift_row, batch=batch, seq=seq)
        o_a = _mla_attn(use_bounded.astype(jnp.int32).reshape(1), q_m, k_m, v_m).reshape(t, -1)

        lambda_init = 0.8 - 0.6 * math.exp(-0.3 * layer)
        qt_d, kn, vt_d = _diff_prep(p, _row(diff_qn_g[layer], 2), _row(diff_kn_g[layer], 2),
                                  batch=batch, seq=seq)
        lam4 = jnp.stack([diff_lam_q1[layer], diff_lam_k1[layer],
                          diff_lam_q2[layer], diff_lam_k2[layer]]).astype(F32)
        d_shift, d_bounded = _score_shift(DIFF_Q_SCALE * _norm_bound(DIFF_HEAD_DIM, diff_qn_g[layer]),
                                          _norm_bound(DIFF_HEAD_DIM, diff_kn_g[layer]))
        scal = jnp.concatenate([slopes, d_shift.reshape(1), d_bounded.astype(F32).reshape(1)])
        o_b = _diff_attn(scal, qt_d, kn.reshape(batch, seq, -1), vt_d,
                         pos_row, pos_col, lam4, _row(diff_subln_g[layer]),
                         lambda_init=lambda_init).reshape(t, -1)

        j = layer // 2
        wa, wb, wo = (w.astype(BF16) for w in (w_branch_a[layer], w_branch_b[layer], w_out[layer]))
        if layer % 2 == 0:
            xf, h = _merge(xf, o_a, o_b, p, wa, wb, wo, _row(ffn_norm_g[layer]))
            xf = _ffn(xf, h, dense_w_gate[j].astype(BF16), dense_w_up[j].astype(BF16),
                      dense_w_down[j].astype(BF16))
        else:
            wr = jnp.pad(moe_w_router[j].astype(F32), ((0, 0), (0, LANES - N_EXPERTS)))
            wr_hi = wr.astype(BF16)
            wr_lo = (wr - wr_hi.astype(F32)).astype(BF16)
            xf, h, route = _merge(xf, o_a, o_b, p, wa, wb, wo, _row(ffn_norm_g[layer]),
                                  router=(wr_hi, wr_lo))
            tok, tile_e, valid, pos = _route_plan(route, MOE_TILE_ROWS)
            y_rows = _moe_gmm(tok, tile_e, valid, h, moe_w_gate[j].astype(BF16),
                              moe_w_up[j].astype(BF16), moe_w_down[j].astype(BF16))
            xf = _moe_combine(xf, route, y_rows, pos)
    return xf.reshape(batch, seq, d)
```
